```python
import jax, jax.numpy as jnp
from jax import lax
import numpy as np

D_MODEL = 1024
BATCH = 32
SEQ = 256
DEPTH = 2
DEC_BATCH = 8
DEC_SEQ = 4096
PAST_LEN = 512

GRID_W = 64
H_A = 8
Q_RANK = 256
KV_RANK = 128
NOPE = 64
ROPE_DIM = 32
V_A = 64
ROPE_BASE = 10000.0
Q_BLOCK = 128
H_B = 4
DH_B = 128
CHUNK = 64
MB = H_B * DH_B
MIX_W = H_A * V_A + MB
AB_SPLITS = tuple(int(s) for s in np.cumsum([Q_RANK, KV_RANK, ROPE_DIM, MB, MB, MB, MB]))
AB_IN = Q_RANK + KV_RANK + ROPE_DIM + 4 * MB + 4 * H_B
CONV_K = 31
N_EXPERTS = 32
TOP_K = 4
D_FF = 1024
SWIGLU_LIMIT = 7.0
SWIGLU_ALPHA = 1.702
MOE_BLOCK = 256
N_AB = (DEPTH + 1) // 2
N_C = DEPTH // 2

kernel_name = "hybrid_mla_mlstm_conformer_moe_dit_step"

F32 = jnp.float32


def rms_norm(x, g, eps=1e-6):
    xf = x.astype(F32)
    y = xf * lax.rsqrt(jnp.mean(xf * xf, axis=-1, keepdims=True) + eps)
    return (y * g.astype(F32)).astype(x.dtype)


def layer_norm(x, g, b, eps=1e-5):
    xf = x.astype(F32)
    mu = jnp.mean(xf, axis=-1, keepdims=True)
    var = jnp.mean(jnp.square(xf - mu), axis=-1, keepdims=True)
    return ((xf - mu) * lax.rsqrt(var + eps) * g.astype(F32) + b.astype(F32)).astype(x.dtype)


def ada_modulation(cond, w, b):
    mod = jax.nn.silu(cond) @ w + b
    return jnp.split(mod[:, None, :], 6, axis=-1)


def modulate(h, shift, scale):
    return h * (1 + scale) + shift


def axial_rope_tables(n_tokens, dtype):
    rows = n_tokens // GRID_W
    pos_row = jnp.repeat(jnp.arange(rows, dtype=F32), GRID_W)
    pos_col = jnp.tile(jnp.arange(GRID_W, dtype=F32), rows)
    n_freq = ROPE_DIM // 4
    inv_freq = ROPE_BASE ** (-jnp.arange(n_freq, dtype=F32) / n_freq)
    ang = jnp.stack([pos_row[:, None] * inv_freq, pos_col[:, None] * inv_freq], axis=1)
    ang = ang[:, None]
    return jnp.cos(ang).astype(dtype), jnp.sin(ang).astype(dtype)


def apply_axial_rope(x, cos, sin):
    xr = x.reshape(x.shape[:-1] + (2, 2, ROPE_DIM // 4))
    x1, x2 = xr[..., 0, :], xr[..., 1, :]
    out = jnp.stack([x1 * cos - x2 * sin, x2 * cos + x1 * sin], axis=-2)
    return out.reshape(x.shape)


def block_attention(q, k, v):
    B_, S_, H_, dq = q.shape
    nb = S_ // Q_BLOCK
    qb = jnp.moveaxis(q.reshape(B_, nb, Q_BLOCK, H_, dq), 1, 0)
    scale = dq ** -0.5

    def one(qblk):
        s = jnp.einsum('bqhd,bkhd->bhqk', qblk, k, preferred_element_type=F32) * scale
        p = jax.nn.softmax(s, axis=-1).astype(v.dtype)
        return jnp.einsum('bhqk,bkhd->bqhd', p, v)

    o = lax.map(one, qb)
    return jnp.moveaxis(o, 0, 1).reshape(B_, S_, H_, v.shape[-1])


def mla_keys(ckv, krope, w_ukv):
    B_, L_, _ = ckv.shape
    kv = (ckv @ w_ukv).reshape(B_, L_, H_A, NOPE + V_A)
    k = jnp.concatenate([kv[..., :NOPE], jnp.broadcast_to(krope, (B_, L_, H_A, ROPE_DIM)).astype(kv.dtype)], axis=-1)
    return k, kv[..., NOPE:]


def mlstm_scan(q, k, v, li, lf, C0, n0, m0):
    B_, H_, S_, dk = q.shape
    dv = v.shape[-1]
    nc = S_ // CHUNK

    def to_chunks(a):
        return jnp.moveaxis(a.reshape((B_, H_, nc, CHUNK) + a.shape[3:]), 2, 0)

    mask = jnp.tril(jnp.ones((CHUNK, CHUNK), dtype=bool))

    def step(carry, inp):
        C, n, m = carry
        qb, kb, vb, ib, fb = inp
        b = jnp.cumsum(fb, axis=-1)
        a = b + m[..., None]
        D = jnp.where(mask, b[..., :, None] - b[..., None, :] + ib[..., None, :], -jnp.inf)
        mt = jnp.maximum(a, jnp.max(D, axis=-1))
        w_inter = jnp.exp(a - mt)
        s = jnp.einsum('bhtd,bhsd->bhts', qb, kb) * jnp.exp(D - mt[..., None])
        num = w_inter[..., None] * jnp.einsum('bhtd,bhde->bhte', qb, C) + jnp.einsum('bhts,bhse->bhte', s, vb)
        den = w_inter * jnp.einsum('bhtd,bhd->bht', qb, n) + jnp.sum(s, axis=-1)
        h = num / jnp.maximum(jnp.abs(den), jnp.exp(-mt))[..., None]
        bL = b[..., -1]
        g = bL[..., None] - b + ib
        m_new = jnp.maximum(bL + m, jnp.max(g, axis=-1))
        wC = jnp.exp(bL + m - m_new)
        ws = jnp.exp(g - m_new[..., None])
        kw = kb * ws[..., None]
        C_new = wC[..., None, None] * C + jnp.einsum('bhsd,bhse->bhde', kw, vb)
        n_new = wC[..., None] * n + jnp.sum(kw, axis=2)
        return (C_new, n_new, m_new), h

    init = (C0.astype(F32), n0.astype(F32), m0.astype(F32))
    (C, n, m), hs = lax.scan(step, init, (to_chunks(q), to_chunks(k), to_chunks(v), to_chunks(li), to_chunks(lf)))
    h = jnp.moveaxis(hs, 0, 2).reshape(B_, H_, S_, dv)
    return h, C, n, m


def mlstm_mixer(mq, mk, mv, mo, zg, norm_g, C0, n0, m0):
    B_, S_, _ = mq.shape

    def heads(a):
        return a.reshape(B_, S_, H_B, DH_B).transpose(0, 2, 1, 3).astype(F32)

    q, k, v = heads(mq), heads(mk) * (DH_B ** -0.5), heads(mv)
    g = zg.astype(F32).reshape(B_, S_, 4, H_B).transpose(2, 0, 3, 1)
    li_f, lf_f = g[0], jax.nn.log_sigmoid(g[1])
    li_b, lf_b = g[2], jax.nn.log_sigmoid(g[3])

    def flip(a):
        return jnp.flip(a, axis=2)

    h_f, C_f, n_f, m_f = mlstm_scan(q, k, v, li_f, lf_f, C0[:, 0], n0[:, 0], m0[:, 0])
    h_b, C_b, n_b, m_b = mlstm_scan(flip(q), flip(k), flip(v), flip(li_b), flip(lf_b), C0[:, 1], n0[:, 1], m0[:, 1])
    h = h_f + flip(h_b)
    mu = jnp.mean(h, axis=-1, keepdims=True)
    var = jnp.mean(jnp.square(h - mu), axis=-1, keepdims=True)
    h = (h - mu) * lax.rsqrt(var + 1e-5)
    h = h.transpose(0, 2, 1, 3).reshape(B_, S_, MB) * norm_g.astype(F32)
    out = (h * jax.nn.sigmoid(mo.astype(F32))).astype(mq.dtype)
    return out, jnp.stack([C_f, C_b], axis=1), jnp.stack([n_f, n_b], axis=1), jnp.stack([m_f, m_b], axis=1)


def ab_mixer(h, rope, ctx, w_in, q_norm_g, w_uq, kv_norm_g, w_ukv, gate_b, mh_norm_g, w_out):
    B_, S_, _ = h.shape
    zq, zkv, zkr, mq, mk, mv, mo, zg = jnp.split(h @ w_in, AB_SPLITS, axis=-1)
    q = (rms_norm(zq, q_norm_g) @ w_uq).reshape(B_, S_, H_A, NOPE + ROPE_DIM)
    ckv = rms_norm(zkv, kv_norm_g)
    krope = zkr[:, :, None, :]
    if rope is not None:
        cos, sin = rope
        q = jnp.concatenate([q[..., :NOPE], apply_axial_rope(q[..., NOPE:], cos, sin)], axis=-1)
        krope = apply_axial_rope(krope, cos, sin)
    k, v = mla_keys(ckv, krope, w_ukv)
    if ctx is None:
        C0 = jnp.zeros((B_, 2, H_B, DH_B, DH_B), F32)
        n0 = jnp.zeros((B_, 2, H_B, DH_B), F32)
        m0 = jnp.zeros((B_, 2, H_B), F32)
    else:
        ctx_ckv, ctx_krope, C0, n0, m0 = ctx
        k_c, v_c = mla_keys(ctx_ckv.astype(ckv.dtype), ctx_krope[:, :, None, :], w_ukv)
        k = jnp.concatenate([k_c.astype(k.dtype), k], axis=1)
        v = jnp.concatenate([v_c.astype(v.dtype), v], axis=1)
    o_a = block_attention(q, k, v).reshape(B_, S_, H_A * V_A)
    o_b, C, n, m = mlstm_mixer(mq, mk, mv, mo, zg + gate_b, mh_norm_g, C0, n0, m0)
    out = jnp.concatenate([o_a, o_b], axis=-1) @ w_out
    return out, (ckv, zkr, C, n, m)


def conv_module(h, w_pw1, w_dw, b_dw, ln_g, ln_b, w_pw2):
    a, g = jnp.split(h @ w_pw1, 2, axis=-1)
    u = a * jax.nn.sigmoid(g)
    u = lax.conv_general_dilated(u, w_dw[:, None, :].astype(u.dtype), window_strides=(1,),
                                 padding=((CONV_K // 2, CONV_K // 2),),
                                 dimension_numbers=('NWC', 'WIO', 'NWC'),
                                 feature_group_count=D_MODEL) + b_dw
    u = jax.nn.silu(layer_norm(u, ln_g, ln_b))
    return u @ w_pw2


def moe_ffn(x, w_r, b_r, w_in, b_in, w_out, b_out):
    B_, S_, D_ = x.shape
    xt = x.reshape(-1, D_)
    T = xt.shape[0]
    logits = (xt @ w_r + b_r).astype(F32)
    top_logit, top_e = lax.top_k(logits, TOP_K)
    gate = jax.nn.softmax(top_logit, axis=-1)
    A = T * TOP_K
    flat_e = top_e.reshape(A)
    flat_tok = jnp.arange(A, dtype=jnp.int32) // TOP_K
    order = jnp.argsort(flat_e)
    e_sorted = flat_e[order]
    counts = jnp.zeros((N_EXPERTS,), jnp.int32).at[flat_e].add(1)
    starts = jnp.cumsum(counts) - counts
    padded = (counts + MOE_BLOCK - 1) // MOE_BLOCK * MOE_BLOCK
    pad_ends = jnp.cumsum(padded)
    pad_starts = pad_ends - padded
    dest = pad_starts[e_sorted] + (jnp.arange(A, dtype=jnp.int32) - starts[e_sorted])
    n_blocks = -(-A // MOE_BLOCK) + N_EXPERTS
    tok_sorted = flat_tok[order]
    rows_tok = jnp.zeros((n_blocks * MOE_BLOCK,), jnp.int32).at[dest].set(tok_sorted)
    block_e = jnp.minimum(jnp.searchsorted(pad_ends, jnp.arange(n_blocks, dtype=jnp.int32) * MOE_BLOCK, side='right'),
                          N_EXPERTS - 1).astype(jnp.int32)

    def expert_block(args):
        tok, e = args
        hb = xt[tok] @ w_in[e] + b_in[e]
        g, u = jnp.split(hb, 2, axis=-1)
        g = jnp.minimum(g, SWIGLU_LIMIT)
        u = jnp.clip(u, -SWIGLU_LIMIT, SWIGLU_LIMIT)
        act = g * jax.nn.sigmoid(SWIGLU_ALPHA * g) * (u + 1)
        return act @ w_out[e] + b_out[e]

    y_rows = lax.map(expert_block, (rows_tok.reshape(n_blocks, MOE_BLOCK), block_e)).reshape(-1, D_)
    contrib = y_rows[dest] * gate.reshape(A)[order][:, None].astype(y_rows.dtype)
    y = jnp.zeros_like(xt).at[tok_sorted].add(contrib)
    return y.reshape(B_, S_, D_)


def setup_inputs(seed: int = 0) -> dict:
    key = jax.random.key(seed)
    ks = iter(jax.random.split(key, 48))

    def nrm(shape, scale):
        return jax.random.normal(next(ks), shape, F32) * scale

    def gain(shape):
        return 1.0 + nrm(shape, 0.02)

    D = D_MODEL
    fb_f = jnp.linspace(3.0, 6.0, H_B)[None, :] + nrm((N_AB, H_B), 0.1)
    fb_b = jnp.linspace(3.0, 6.0, H_B)[None, :] + nrm((N_AB, H_B), 0.1)
    gate_b = jnp.stack([nrm((N_AB, H_B), 0.1), fb_f, nrm((N_AB, H_B), 0.1), fb_b], axis=1).reshape(N_AB, 4 * H_B)
    return {
        'x_prompt': nrm((BATCH, SEQ, D), 1.0),
        'x_sample': nrm((DEC_BATCH, DEC_SEQ, D), 1.0),
        'c': nrm((DEC_BATCH, D), 1.0),
        'cache_mla_ckv': nrm((DEC_BATCH, N_AB, PAST_LEN, KV_RANK), 1.0),
        'cache_mla_krope': nrm((DEC_BATCH, N_AB, PAST_LEN, ROPE_DIM), 1.0),
        'state_mlstm_C': nrm((DEC_BATCH, N_AB, 2, H_B, DH_B, DH_B), 0.1),
        'state_mlstm_n': nrm((DEC_BATCH, N_AB, 2, H_B, DH_B), 0.1),
        'state_mlstm_m': nrm((DEC_BATCH, N_AB, 2, H_B), 1.0),
        'c_ctx': nrm((D,), 1.0),
        'ada_w': nrm((DEPTH, D, 6 * D), 0.5 * D ** -0.5),
        'ada_b': nrm((DEPTH, 6 * D), 0.02),
        'norm1_g': gain((DEPTH, D)),
        'norm2_g': gain((DEPTH, D)),
        'ab_w_in': nrm((N_AB, D, AB_IN), D ** -0.5),
        'mla_q_norm_g': gain((N_AB, Q_RANK)),
        'mla_w_uq': nrm((N_AB, Q_RANK, H_A * (NOPE + ROPE_DIM)), Q_RANK ** -0.5),
        'mla_kv_norm_g': gain((N_AB, KV_RANK)),
        'mla_w_ukv': nrm((N_AB, KV_RANK, H_A * (NOPE + V_A)), KV_RANK ** -0.5),
        'mlstm_gate_b': gate_b,
        'mlstm_norm_g': gain((N_AB, MB)),
        'ab_w_out': nrm((N_AB, MIX_W, D), MIX_W ** -0.5),
        'conv_w_pw1': nrm((N_C, D, 2 * D), D ** -0.5),
        'conv_w_dw': nrm((N_C, CONV_K, D), CONV_K ** -0.5),
        'conv_b_dw': nrm((N_C, D), 0.01),
        'conv_ln_g': gain((N_C, D)),
        'conv_ln_b': nrm((N_C, D), 0.01),
        'conv_w_pw2': nrm((N_C, D, D), D ** -0.5),
        'router_w': nrm((DEPTH, D, N_EXPERTS), D ** -0.5),
        'router_b': nrm((DEPTH, N_EXPERTS), 0.01),
        'moe_w_in': nrm((DEPTH, N_EXPERTS, D, 2 * D_FF), D ** -0.5),
        'moe_b_in': nrm((DEPTH, N_EXPERTS, 2 * D_FF), 0.01),
        'moe_w_out': nrm((DEPTH, N_EXPERTS, D_FF, D), D_FF ** -0.5),
        'moe_b_out': nrm((DEPTH, N_EXPERTS, D), 0.01),
        'final_norm_g': gain((D,)),
    }


def reference(x_prompt, x_sample, c, cache_mla_ckv, cache_mla_krope, state_mlstm_C, state_mlstm_n, state_mlstm_m,
              c_ctx, ada_w, ada_b, norm1_g, norm2_g, ab_w_in, mla_q_norm_g, mla_w_uq, mla_kv_norm_g, mla_w_ukv,
              mlstm_gate_b, mlstm_norm_g, ab_w_out, conv_w_pw1, conv_w_dw, conv_b_dw, conv_ln_g, conv_ln_b,
              conv_w_pw2, router_w, router_b, moe_w_in, moe_b_in, moe_w_out, moe_b_out, final_norm_g):
    rope_lat = axial_rope_tables(x_sample.shape[1], x_sample.dtype)
    xp, xs = x_prompt, x_sample
    ckv_l, kr_l, C_l, n_l, m_l = [], [], [], [], []
    for l in range(DEPTH):
        j = l // 2
        mp = ada_modulation(c_ctx[None, :].astype(xp.dtype), ada_w[l], ada_b[l])
        ms = ada_modulation(c, ada_w[l], ada_b[l])
        hp = modulate(rms_norm(xp, norm1_g[l]), mp[0], mp[1])
        hs = modulate(rms_norm(xs, norm1_g[l]), ms[0], ms[1])
        if l % 2 == 0:
            prm = (ab_w_in[j], mla_q_norm_g[j], mla_w_uq[j], mla_kv_norm_g[j], mla_w_ukv[j],
                   mlstm_gate_b[j], mlstm_norm_g[j], ab_w_out[j])
            op, (ckv, kr, Cs, ns, ms_) = ab_mixer(hp, None, None, *prm)
            ctx = (cache_mla_ckv[:, j], cache_mla_krope[:, j], state_mlstm_C[:, j], state_mlstm_n[:, j],
                   state_mlstm_m[:, j])
            os_, _ = ab_mixer(hs, rope_lat, ctx, *prm)
            ckv_l.append(ckv); kr_l.append(kr); C_l.append(Cs); n_l.append(ns); m_l.append(ms_)
        else:
            prm = (conv_w_pw1[j], conv_w_dw[j], conv_b_dw[j], conv_ln_g[j], conv_ln_b[j], conv_w_pw2[j])
            op = conv_module(hp, *prm)
            os_ = conv_module(hs, *prm)
        xp = xp + mp[2] * op
        xs = xs + ms[2] * os_
        moe_prm = (router_w[l], router_b[l], moe_w_in[l], moe_b_in[l], moe_w_out[l], moe_b_out[l])
        xp = xp + mp[5] * moe_ffn(modulate(rms_norm(xp, norm2_g[l]), mp[3], mp[4]), *moe_prm)
        xs = xs + ms[5] * moe_ffn(modulate(rms_norm(xs, norm2_g[l]), ms[3], ms[4]), *moe_prm)
    y_prompt = rms_norm(xp, final_norm_g)
    y_sample = rms_norm(xs, final_norm_g)
    new_mla_ckv = jnp.stack(ckv_l, axis=1)
    new_mla_krope = jnp.stack(kr_l, axis=1)
    new_mlstm_C = jnp.stack(C_l, axis=1)
    new_mlstm_n = jnp.stack(n_l, axis=1)
    new_mlstm_m = jnp.stack(m_l, axis=1)
    return (y_prompt, y_sample, new_mla_ckv, new_mla_krope, new_mlstm_C, new_mlstm_n, new_mlstm_m)
```

```python
import functools

import jax
import jax.numpy as jnp
import numpy as np
from jax import lax
from jax.experimental import pallas as pl
from jax.experimental.pallas import tpu as pltpu

F32 = jnp.float32
BF16 = jnp.bfloat16
I32 = jnp.int32

D = 1024
GRID_W = 64
H_A = 8
Q_RANK = 256
KV_RANK = 128
NOPE = 64
ROPE_DIM = 32
V_A = 64
ROPE_BASE = 10000.0
H_B = 4
DH_B = 128
MB = H_B * DH_B
CONV_K = 31
N_EXPERTS = 32
TOP_K = 4
D_FF = 1024
SWIGLU_LIMIT = 7.0
SWIGLU_ALPHA = 1.702

LANES = 128
HEAD_PAD = 128
ROPE_LANE0 = 64
N_COND = 16
CTX_ROW = 8

TM = 256
FB = 256
LC = 128
TQ = 256
VMEM_LIMIT = 56 * 1024 * 1024

NEG_INF = float("-inf")


def _cparams(sem):
    return pltpu.CompilerParams(dimension_semantics=sem, vmem_limit_bytes=VMEM_LIMIT)


def _sigmoid(x):
    return 1.0 / (1.0 + jnp.exp(-x))


def _rms(x, g, eps=1e-6):
    return x * lax.rsqrt(jnp.mean(x * x, axis=-1, keepdims=True) + eps) * g


def _lane(shape):
    return lax.broadcasted_iota(I32, shape, len(shape) - 1)


def _mod_kernel(cond_ref, w_ref, b_ref, o_ref):
    c = cond_ref[...]
    s = (c * _sigmoid(c)).astype(BF16)
    o_ref[...] = jnp.dot(s, w_ref[...].astype(BF16), preferred_element_type=F32) + b_ref[...]


def _modulation(cond, ada_w, ada_b):
    depth = ada_w.shape[0]
    nj = ada_w.shape[2] // D
    out = pl.pallas_call(
        _mod_kernel,
        grid=(depth, nj),
        in_specs=[
            pl.BlockSpec((N_COND, D), lambda l, j: (0, 0)),
            pl.BlockSpec((None, D, D), lambda l, j: (l, 0, j)),
            pl.BlockSpec((None, 1, D), lambda l, j: (l, 0, j)),
        ],
        out_specs=pl.BlockSpec((None, N_COND, D), lambda l, j: (l, 0, j)),
        out_shape=jax.ShapeDtypeStruct((depth, N_COND, nj * D), F32),
        compiler_params=_cparams(("arbitrary", "arbitrary")),
        name="ada_modulation",
    )(cond, ada_w, ada_b.reshape(depth, 1, nj * D))
    return out.reshape(depth, N_COND, nj, D)


def _start_row_gather(dest_hbm, y_hbm, idx_smem, buf, sem_idx, sem, tile):
    cp = pltpu.make_async_copy(dest_hbm.at[pl.ds(tile * (TM * TOP_K), TM * TOP_K)], idx_smem, sem_idx)
    cp.start()
    cp.wait()

    def body(r, carry):
        for k in range(TOP_K):
            d = idx_smem[r * TOP_K + k]
            pltpu.make_async_copy(y_hbm.at[pl.ds(d, 1), :], buf.at[k, pl.ds(r, 1), :], sem).start()
        return carry

    lax.fori_loop(0, TM, body, 0, unroll=8)


def _wait_row_gather(y_hbm, buf, sem):
    for k in range(TOP_K):
        pltpu.make_async_copy(y_hbm.at[pl.ds(0, TM), :], buf.at[k], sem).wait()


def _combine(route, buf):
    y = route[:, 4:5] * buf[0]
    for k in range(1, TOP_K):
        y = y + route[:, 4 + k:5 + k] * buf[k]
    return y


def _moe_prologue(x1, mod_ref, n2g, wr_ref, br_ref, cnt_ref, pk_ref, route_ref):
    h2 = _rms(x1, n2g) * (1.0 + mod_ref[4:5, :]) + mod_ref[3:4, :]
    h2b = h2.astype(BF16)
    hi = pltpu.bitcast(h2b[:, :D // 2].astype(F32), jnp.uint32)
    lo = pltpu.bitcast(h2b[:, D // 2:].astype(F32), jnp.uint32)
    pk_ref[...] = hi | (lo >> 16)

    logits = jnp.dot(h2b, wr_ref[...], preferred_element_type=F32) + br_ref[...]
    tm = logits.shape[0]
    lane = _lane((tm, LANES))
    lanef = lane.astype(F32)
    l = jnp.where(lane < N_EXPERTS, logits, NEG_INF)
    tops, idxs, sels = [], [], []
    for _ in range(TOP_K):
        mv = jnp.max(l, axis=-1, keepdims=True)
        idx = jnp.min(jnp.where(l == mv, lanef, float(LANES)), axis=-1, keepdims=True)
        sel = lanef == idx
        l = jnp.where(sel, NEG_INF, l)
        tops.append(mv)
        idxs.append(idx)
        sels.append(sel)
    ex = [jnp.exp(t - tops[0]) for t in tops]
    den = ex[0] + ex[1] + ex[2] + ex[3]
    gates = [e / den for e in ex]

    onehot = jnp.zeros((tm, LANES), F32)
    for sel in sels:
        onehot = onehot + jnp.where(sel, 1.0, 0.0)
    row = lax.broadcasted_iota(I32, (tm, tm), 0)
    col = lax.broadcasted_iota(I32, (tm, tm), 1)
    strict = jnp.where(col < row, 1.0, 0.0).astype(BF16)
    before = jnp.dot(strict, onehot.astype(BF16), preferred_element_type=F32) + cnt_ref[0:1, :]
    ranks = [jnp.sum(jnp.where(sel, before, 0.0), axis=-1, keepdims=True) for sel in sels]
    cnt_ref[...] = cnt_ref[...] + jnp.sum(onehot, axis=0, keepdims=True)

    r = jnp.zeros((tm, LANES), F32)
    for j, colv in enumerate(idxs + gates + ranks):
        r = jnp.where(lane == j, colv, r)
    route_ref[...] = r


def _rope(x, tab_ref):
    return x * tab_ref[0] + pltpu.roll(x, LANES - 8, 1) * tab_ref[1] + pltpu.roll(x, 8, 1) * tab_ref[2]


def _ab_in_kernel(x_ref, mod_ref, n1g_ref, win_ref, qg_ref, wq_ref, kvg_ref, wk_ref, wv_ref, gb_ref, tab_ref,
                  q_ref, k_ref, v_ref, ckv_ref, misc_ref, mqkv_ref, mo_ref):
    x = x_ref[...]
    h = _rms(x, n1g_ref[...]) * (1.0 + mod_ref[1:2, :]) + mod_ref[0:1, :]
    z = jnp.dot(h.astype(BF16), win_ref[...], preferred_element_type=F32)

    qn = _rms(z[:, :Q_RANK], qg_ref[...]).astype(BF16)
    q = jnp.dot(qn, wq_ref[...], preferred_element_type=F32)
    ckv = _rms(z[:, Q_RANK:Q_RANK + KV_RANK], kvg_ref[...])
    ckv_ref[...] = ckv
    ckvb = ckv.astype(BF16)
    kn = jnp.dot(ckvb, wk_ref[...], preferred_element_type=F32)
    v_ref[...] = jnp.dot(ckvb, wv_ref[...], preferred_element_type=F32).astype(BF16)

    misc = z[:, Q_RANK + KV_RANK:Q_RANK + KV_RANK + LANES]
    misc_ref[...] = misc + gb_ref[...]
    lane = _lane(misc.shape)
    kr = jnp.where((lane >= ROPE_LANE0) & (lane < ROPE_LANE0 + ROPE_DIM), _rope(misc, tab_ref), 0.0)
    scale = float((NOPE + ROPE_DIM) ** -0.5)
    for hh in range(H_A):
        sl = slice(hh * HEAD_PAD, (hh + 1) * HEAD_PAD)
        q_ref[:, sl] = (_rope(q[:, sl], tab_ref) * scale).astype(BF16)
        k_ref[:, sl] = (kn[:, sl] + kr).astype(BF16)

    m0 = Q_RANK + KV_RANK + LANES
    mqkv_ref[:, :MB] = z[:, m0:m0 + MB].astype(BF16)
    mqkv_ref[:, MB:2 * MB] = (z[:, m0 + MB:m0 + 2 * MB] * float(DH_B ** -0.5)).astype(BF16)
    mqkv_ref[:, 2 * MB:] = z[:, m0 + 2 * MB:m0 + 3 * MB].astype(BF16)
    mo_ref[...] = z[:, m0 + 3 * MB:m0 + 4 * MB]


def _ctx_kv_kernel(ckv_ref, kr_ref, wk_ref, wv_ref, k_ref, v_ref):
    ckvb = ckv_ref[...].astype(BF16)
    kn = jnp.dot(ckvb, wk_ref[...], preferred_element_type=F32)
    v_ref[...] = jnp.dot(ckvb, wv_ref[...], preferred_element_type=F32).astype(BF16)
    kr = kr_ref[...]
    for hh in range(H_A):
        sl = slice(hh * HEAD_PAD, (hh + 1) * HEAD_PAD)
        k_ref[:, sl] = (kn[:, sl] + kr).astype(BF16)


def _attn_kernel(q_ref, k_ref, v_ref, o_ref):
    v = v_ref[...]
    outs = []
    for j in range(2):
        q = q_ref[:, j * HEAD_PAD:(j + 1) * HEAD_PAD]
        k = k_ref[:, j * HEAD_PAD:(j + 1) * HEAD_PAD]
        s = lax.dot_general(q, k, (((1,), (1,)), ((), ())), preferred_element_type=F32)
        m = jnp.max(s, axis=-1, keepdims=True)
        p = jnp.exp(s - m)
        l = jnp.sum(p, axis=-1, keepdims=True)
        o = jnp.dot(p.astype(BF16), v, preferred_element_type=F32)
        outs.append(o / l)
    lane = _lane(outs[0].shape)
    o_ref[...] = jnp.where(lane < V_A, outs[0], outs[1]).astype(BF16)


def _attention(q, k, v):
    b, s, _ = q.shape
    l = k.shape[1]
    tq = min(TQ, s)
    return pl.pallas_call(
        _attn_kernel,
        grid=(b, H_A // 2, s // tq),
        in_specs=[
            pl.BlockSpec((None, tq, 2 * HEAD_PAD), lambda bi, hp, qi: (bi, qi, hp)),
            pl.BlockSpec((None, l, 2 * HEAD_PAD), lambda bi, hp, qi: (bi, 0, hp)),
            pl.BlockSpec((None, l, 2 * V_A), lambda bi, hp, qi: (bi, 0, hp)),
        ],
        out_specs=pl.BlockSpec((None, tq, 2 * V_A), lambda bi, hp, qi: (bi, qi, hp)),
        out_shape=jax.ShapeDtypeStruct((b, s, H_A * V_A), BF16),
        compiler_params=_cparams(("arbitrary", "arbitrary", "arbitrary")),
        name="mla_attention",
    )(q, k, v)


def _log_sigmoid(x):
    return jnp.minimum(x, 0.0) - jnp.log(1.0 + jnp.exp(-jnp.abs(x)))


def _mlstm_kernel(q_ref, k_ref, v_ref, g_ref, c0_ref, n0_ref, m0_ref, h_ref, c_out, n_out, m_out,
                  c_s, n_s, m_s):
    d = pl.program_id(1)
    c = pl.program_id(2)
    fwd = d == 0

    @pl.when(c == 0)
    def _():
        c_s[...] = c0_ref[...]
        n_s[...] = n0_ref[...]
        m_s[...] = m0_ref[...]

    gates = g_ref[...]
    lf_all = _log_sigmoid(gates)
    row = lax.broadcasted_iota(I32, (LC, LC), 0)
    col = lax.broadcasted_iota(I32, (LC, LC), 1)
    sgn = jnp.where(fwd, 1, -1)
    mask = (col - row) * sgn <= 0
    eye = row == col
    cum = jnp.dot(jnp.where(mask, 1.0, 0.0), lf_all, preferred_element_type=F32,
                  precision=lax.Precision.HIGHEST)

    for hh in range(H_B):
        sl = slice(hh * DH_B, (hh + 1) * DH_B)
        q = q_ref[:, sl]
        k = k_ref[:, sl]
        v = v_ref[:, sl]
        b_col = jnp.where(fwd, cum[:, 4 + hh:5 + hh], cum[:, 12 + hh:13 + hh])
        i_col = jnp.where(fwd, gates[:, hh:hh + 1], gates[:, 8 + hh:9 + hh])
        m_prev = m_s[hh][:, 0:1]
        r_row = jnp.sum(jnp.where(eye, i_col - b_col, 0.0), axis=0, keepdims=True)
        dm = jnp.where(mask, b_col + r_row, NEG_INF)
        a_col = b_col + m_prev
        mt = jnp.maximum(a_col, jnp.max(dm, axis=-1, keepdims=True))
        w_inter = jnp.exp(a_col - mt)
        qk = lax.dot_general(q, k, (((1,), (1,)), ((), ())), preferred_element_type=F32)
        s = qk * jnp.exp(dm - mt)
        c_prev = c_s[hh]
        n_prev = n_s[hh]
        num = w_inter * jnp.dot(q, c_prev.astype(BF16), preferred_element_type=F32) \
            + jnp.dot(s.astype(BF16), v, preferred_element_type=F32)
        qf = q.astype(F32)
        den = w_inter * jnp.sum(qf * n_prev, axis=-1, keepdims=True) + jnp.sum(s, axis=-1, keepdims=True)
        h_ref[:, sl] = num / jnp.maximum(jnp.abs(den), jnp.exp(-mt))

        b_tot = jnp.where(fwd, b_col[LC - 1:LC, :], b_col[0:1, :])
        g_col = b_tot - b_col + i_col
        m_new = jnp.maximum(b_tot + m_prev, jnp.max(g_col, axis=0, keepdims=True))
        w_c = jnp.exp(b_tot + m_prev - m_new)
        kw = k.astype(F32) * jnp.exp(g_col - m_new)
        c_s[hh] = w_c * c_prev + jnp.dot(kw.T.astype(BF16), v, preferred_element_type=F32)
        n_s[hh] = w_c * n_prev + jnp.sum(kw, axis=0, keepdims=True)
        m_s[hh] = jnp.broadcast_to(m_new, (1, LANES))

    @pl.when(c == pl.num_programs(2) - 1)
    def _():
        c_out[...] = c_s[...]
        n_out[...] = n_s[...]
        m_out[...] = m_s[...]


def _mlstm(mqkv, misc, c0, n0, m0, row0, nb, s):
    nc = s // LC
    blk0 = row0 // LC
    t = mqkv.shape[0]

    def rows(bi, di, ci):
        return blk0 + bi * nc + ci + di * (nc - 1 - 2 * ci)

    state_spec = lambda shape: pl.BlockSpec((None, None) + shape, lambda bi, di, ci: (bi, di) + (0,) * len(shape))
    return pl.pallas_call(
        _mlstm_kernel,
        grid=(nb, 2, nc),
        in_specs=[
            pl.BlockSpec((LC, MB), lambda bi, di, ci: (rows(bi, di, ci), 0)),
            pl.BlockSpec((LC, MB), lambda bi, di, ci: (rows(bi, di, ci), 1)),
            pl.BlockSpec((LC, MB), lambda bi, di, ci: (rows(bi, di, ci), 2)),
            pl.BlockSpec((LC, LANES), lambda bi, di, ci: (rows(bi, di, ci), 0)),
            state_spec((H_B, DH_B, DH_B)),
            state_spec((H_B, 1, DH_B)),
            state_spec((H_B, 1, LANES)),
        ],
        out_specs=[
            pl.BlockSpec((None, LC, MB), lambda bi, di, ci: (di, bi * nc + ci + di * (nc - 1 - 2 * ci), 0)),
            state_spec((H_B, DH_B, DH_B)),
            state_spec((H_B, 1, DH_B)),
            state_spec((H_B, 1, LANES)),
        ],
        out_shape=[
            jax.ShapeDtypeStruct((2, nb * s, MB), F32),
            jax.ShapeDtypeStruct((nb, 2, H_B, DH_B, DH_B), F32),
            jax.ShapeDtypeStruct((nb, 2, H_B, 1, DH_B), F32),
            jax.ShapeDtypeStruct((nb, 2, H_B, 1, LANES), F32),
        ],
        scratch_shapes=[
            pltpu.VMEM((H_B, DH_B, DH_B), F32),
            pltpu.VMEM((H_B, 1, DH_B), F32),
            pltpu.VMEM((H_B, 1, LANES), F32),
        ],
        compiler_params=_cparams(("arbitrary", "arbitrary", "arbitrary")),
        name="mlstm_scan",
    )(mqkv, mqkv, mqkv, misc, c0, n0, m0)


def _ab_out_kernel(oa_ref, hf_ref, hb_ref, mo_ref, mg_ref, wout_ref, x_ref, mod_ref, n2g_ref, wr_ref, br_ref,
                   x1_ref, pk_ref, route_ref, cnt_ref):
    @pl.when(pl.program_id(0) == 0)
    def _():
        cnt_ref[...] = jnp.zeros_like(cnt_ref)

    hsum = hf_ref[...] + hb_ref[...]
    parts = []
    for hh in range(H_B):
        hh_ = hsum[:, hh * DH_B:(hh + 1) * DH_B]
        mu = jnp.mean(hh_, axis=-1, keepdims=True)
        var = jnp.mean(jnp.square(hh_ - mu), axis=-1, keepdims=True)
        parts.append((hh_ - mu) * lax.rsqrt(var + 1e-5))
    hn = jnp.concatenate(parts, axis=-1) * mg_ref[...]
    ob = (hn * _sigmoid(mo_ref[...])).astype(BF16)
    o = jnp.dot(oa_ref[...], wout_ref[:H_A * V_A, :], preferred_element_type=F32) \
        + jnp.dot(ob, wout_ref[H_A * V_A:, :], preferred_element_type=F32)
    x1 = x_ref[...] + mod_ref[2:3, :] * o
    x1_ref[...] = x1
    _moe_prologue(x1, mod_ref, n2g_ref[...], wr_ref, br_ref, cnt_ref, pk_ref, route_ref)


def _conv_in_kernel(dest_hbm, y_hbm, x1_ref, route_ref, modp_ref, mod_ref, n1g_ref, w_ref,
                    x2_ref, u_ref, idx_smem, buf, sem_idx, sem):
    i = pl.program_id(0)
    _start_row_gather(dest_hbm, y_hbm, idx_smem, buf, sem_idx, sem, i)
    _wait_row_gather(y_hbm, buf, sem)
    x2 = x1_ref[...] + modp_ref[5:6, :] * _combine(route_ref[...], buf)
    x2_ref[...] = x2
    h = _rms(x2, n1g_ref[...]) * (1.0 + mod_ref[1:2, :]) + mod_ref[0:1, :]
    z = jnp.dot(h.astype(BF16), w_ref[...], preferred_element_type=F32)
    u_ref[...] = z[:, :D] * _sigmoid(z[:, D:])


HALO = 16


def _conv_out_kernel(nprompt_tiles, tiles_per_seq, u_ref, up_ref, un_ref, wdw_ref, bdw_ref, lg_ref, lb_ref, w2_ref,
                     x_ref, mod_ref, n2g_ref, wr_ref, br_ref, x1_ref, pk_ref, route_ref, cnt_ref, ext_ref):
    i = pl.program_id(0)

    @pl.when(i == 0)
    def _():
        cnt_ref[...] = jnp.zeros_like(cnt_ref)

    j = i - nprompt_tiles
    in_prompt = i < nprompt_tiles
    first = in_prompt | (j % tiles_per_seq == 0)
    last = in_prompt | (j % tiles_per_seq == tiles_per_seq - 1)
    ext_ref[0:HALO, :] = jnp.where(first, 0.0, up_ref[...])
    ext_ref[HALO:HALO + TM, :] = u_ref[...]
    ext_ref[HALO + TM:, :] = jnp.where(last, 0.0, un_ref[...])

    acc = jnp.zeros((TM, D), F32) + bdw_ref[...]
    for b in range(8):
        eb = ext_ref[b + 1:b + 1 + TM + 24, :]
        for a in range(4):
            tap = 8 * a + b
            if tap < CONV_K:
                acc = acc + wdw_ref[tap:tap + 1, :] * eb[8 * a:8 * a + TM, :]
    mu = jnp.mean(acc, axis=-1, keepdims=True)
    var = jnp.mean(jnp.square(acc - mu), axis=-1, keepdims=True)
    ln = (acc - mu) * lax.rsqrt(var + 1e-5) * lg_ref[...] + lb_ref[...]
    act = (ln * _sigmoid(ln)).astype(BF16)
    o = jnp.dot(act, w2_ref[...], preferred_element_type=F32)
    x1 = x_ref[...] + mod_ref[2:3, :] * o
    x1_ref[...] = x1
    _moe_prologue(x1, mod_ref, n2g_ref[...], wr_ref, br_ref, cnt_ref, pk_ref, route_ref)


def _dispatch_kernel(dest_hbm, pk_ref, xs_in, xs_hbm, idx_smem, sem_idx, sem):
    del xs_in
    i = pl.program_id(0)
    cp = pltpu.make_async_copy(dest_hbm.at[pl.ds(i * (TM * TOP_K), TM * TOP_K)], idx_smem, sem_idx)
    cp.start()
    cp.wait()

    def body(r, carry):
        for k in range(TOP_K):
            d = idx_smem[r * TOP_K + k]
            pltpu.make_async_copy(pk_ref.at[pl.ds(r, 1), :], xs_hbm.at[pl.ds(d, 1), :], sem).start()
        return carry

    lax.fori_loop(0, TM, body, 0, unroll=8)
    for k in range(TOP_K):
        pltpu.make_async_copy(pk_ref, xs_hbm.at[pl.ds(0, TM), :], sem).wait()


def _dispatch(dest, pk, n_rows):
    t = pk.shape[0]
    zeros = jnp.zeros((n_rows, D // 2), jnp.uint32)
    return pl.pallas_call(
        _dispatch_kernel,
        grid=(t // TM,),
        in_specs=[
            pl.BlockSpec(memory_space=pl.ANY),
            pl.BlockSpec((TM, D // 2), lambda i: (i, 0)),
            pl.BlockSpec(memory_space=pl.ANY),
        ],
        out_specs=pl.BlockSpec(memory_space=pl.ANY),
        out_shape=jax.ShapeDtypeStruct((n_rows, D // 2), jnp.uint32),
        scratch_shapes=[pltpu.SMEM((TM * TOP_K,), I32), pltpu.SemaphoreType.DMA, pltpu.SemaphoreType.DMA],
        input_output_aliases={2: 0},
        compiler_params=_cparams(("arbitrary",)),
        name="moe_dispatch",
    )(dest, pk, zeros)


def _ffn_kernel(be_ref, nused_ref, xs_ref, win_ref, bin_ref, wout_ref, bout_ref, y_ref, win_s, wout_s):
    i = pl.program_id(0)
    e = be_ref[i]
    e_prev = be_ref[jnp.maximum(i - 1, 0)]

    @pl.when((i == 0) | (e != e_prev))
    def _():
        win_s[...] = win_ref[...].astype(BF16)
        wout_s[...] = wout_ref[...].astype(BF16)

    @pl.when(i < nused_ref[0])
    def _():
        w = xs_ref[...]
        xa = pltpu.bitcast(w & jnp.uint32(0xFFFF0000), F32).astype(BF16)
        xb = pltpu.bitcast(w << 16, F32).astype(BF16)
        hb = jnp.dot(xa, win_s[:D // 2, :], preferred_element_type=F32) \
            + jnp.dot(xb, win_s[D // 2:, :], preferred_element_type=F32) + bin_ref[...]
        g = jnp.minimum(hb[:, :D_FF], SWIGLU_LIMIT)
        u = jnp.clip(hb[:, D_FF:], -SWIGLU_LIMIT, SWIGLU_LIMIT)
        act = g * _sigmoid(SWIGLU_ALPHA * g) * (u + 1.0)
        y_ref[...] = jnp.dot(act.astype(BF16), wout_s[...], preferred_element_type=F32) + bout_ref[...]

    @pl.when(i >= nused_ref[0])
    def _():
        y_ref[...] = jnp.zeros_like(y_ref)


def _expert_ffn(block_e, n_used, xs, w_in, b_in, w_out, b_out):
    n_blocks = xs.shape[0] // FB
    return pl.pallas_call(
        _ffn_kernel,
        grid_spec=pltpu.PrefetchScalarGridSpec(
            num_scalar_prefetch=2,
            grid=(n_blocks,),
            in_specs=[
                pl.BlockSpec((FB, D // 2), lambda i, be, nu: (i, 0)),
                pl.BlockSpec((None, D, 2 * D_FF), lambda i, be, nu: (be[i], 0, 0)),
                pl.BlockSpec((None, 1, 2 * D_FF), lambda i, be, nu: (be[i], 0, 0)),
                pl.BlockSpec((None, D_FF, D), lambda i, be, nu: (be[i], 0, 0)),
                pl.BlockSpec((None, 1, D), lambda i, be, nu: (be[i], 0, 0)),
            ],
            out_specs=pl.BlockSpec((FB, D), lambda i, be, nu: (i, 0)),
            scratch_shapes=[pltpu.VMEM((D, 2 * D_FF), BF16), pltpu.VMEM((D_FF, D), BF16)],
        ),
        out_shape=jax.ShapeDtypeStruct((n_blocks * FB, D), F32),
        compiler_params=_cparams(("arbitrary",)),
        name="moe_expert_ffn",
    )(block_e, n_used, xs, w_in, b_in.reshape(N_EXPERTS, 1, 2 * D_FF), w_out, b_out.reshape(N_EXPERTS, 1, D))


def _moe_plan(route, cnt):
    t = route.shape[0]
    n_blocks = t * TOP_K // FB + N_EXPERTS
    e = route[:, 0:TOP_K].astype(I32)
    rank = route[:, 8:8 + TOP_K].astype(I32)
    counts = cnt[0, :N_EXPERTS].astype(I32)
    padded = (counts + FB - 1) // FB * FB
    pad_ends = jnp.cumsum(padded)
    pad_starts = pad_ends - padded
    start_of = jnp.sum(jnp.where(e[..., None] == jnp.arange(N_EXPERTS, dtype=I32), pad_starts, 0), axis=-1)
    dest = (start_of + rank).reshape(t * TOP_K)
    blk = jnp.arange(n_blocks, dtype=I32) * FB
    block_e = jnp.minimum(jnp.sum((blk[:, None] >= pad_ends[None, :]).astype(I32), axis=-1), N_EXPERTS - 1)
    n_used = (pad_ends[-1] // FB).reshape(1)
    return dest, block_e.astype(I32), n_used.astype(I32), n_blocks * FB


def _moe(route, cnt, pk, w_in, b_in, w_out, b_out):
    dest, block_e, n_used, n_rows = _moe_plan(route, cnt)
    xs = _dispatch(dest, pk, n_rows)
    y = _expert_ffn(block_e, n_used, xs, w_in, b_in, w_out, b_out)
    return dest, y


def _final_kernel(dest_hbm, y_hbm, x1_ref, route_ref, modp_ref, g_ref, o_ref, idx_smem, buf, sem_idx, sem):
    i = pl.program_id(0)
    _start_row_gather(dest_hbm, y_hbm, idx_smem, buf, sem_idx, sem, i)
    _wait_row_gather(y_hbm, buf, sem)
    x2 = x1_ref[...] + modp_ref[5:6, :] * _combine(route_ref[...], buf)
    o_ref[...] = _rms(x2, g_ref[...])


def _rope_tables(n_tokens):
    rows = n_tokens // GRID_W
    pos_row = jnp.repeat(jnp.arange(rows, dtype=F32), GRID_W)
    pos_col = jnp.tile(jnp.arange(GRID_W, dtype=F32), rows)
    n_freq = ROPE_DIM // 4
    inv_freq = ROPE_BASE ** (-jnp.arange(n_freq, dtype=F32) / n_freq)
    ang = jnp.stack([pos_row[:, None] * inv_freq, pos_col[:, None] * inv_freq], axis=1)
    cos, sin = jnp.cos(ang), jnp.sin(ang)
    zero = jnp.zeros_like(sin)
    c = jnp.stack([cos, cos], axis=2).reshape(n_tokens, ROPE_DIM)
    sm = jnp.stack([-sin, zero], axis=2).reshape(n_tokens, ROPE_DIM)
    sp = jnp.stack([zero, sin], axis=2).reshape(n_tokens, ROPE_DIM)

    def place(a, fill):
        left = jnp.full((n_tokens, ROPE_LANE0), fill, F32)
        right = jnp.full((n_tokens, LANES - ROPE_LANE0 - ROPE_DIM), fill, F32)
        body = jnp.concatenate([left, a, right], axis=1)
        ident = jnp.full((TM, LANES), fill, F32)
        return jnp.concatenate([body, ident], axis=0)

    return jnp.stack([place(c, 1.0), place(sm, 0.0), place(sp, 0.0)], axis=0)


def kernel(x_prompt, x_sample, c, cache_mla_ckv, cache_mla_krope, state_mlstm_C, state_mlstm_n, state_mlstm_m, c_ctx, ada_w, ada_b, norm1_g, norm2_g, ab_w_in, mla_q_norm_g, mla_w_uq, mla_kv_norm_g, mla_w_ukv, mlstm_gate_b, mlstm_norm_g, ab_w_out, conv_w_pw1, conv_w_dw, conv_b_dw, conv_ln_g, conv_ln_b, conv_w_pw2, router_w, router_b, moe_w_in, moe_b_in, moe_w_out, moe_b_out, final_norm_g):
    bp, sp_, _ = x_prompt.shape
    bs, ss, _ = x_sample.shape
    past = cache_mla_ckv.shape[2]
    tp, ts = bp * sp_, bs * ss
    t = tp + ts
    nt = t // TM
    npt = tp // TM
    tps = ss // TM
    assert sp_ == TM and ss % TM == 0 and bs <= CTX_ROW

    x = jnp.concatenate([x_prompt.reshape(tp, D), x_sample.reshape(ts, D)], axis=0)
    cond = jnp.zeros((N_COND, D), F32).at[:bs].set(c).at[CTX_ROW].set(c_ctx)
    mod = _modulation(cond, ada_w, ada_b)

    def mod_row(i):
        return jnp.where(i < npt, CTX_ROW, (i - npt) // tps)

    def mod_spec(layer):
        return pl.BlockSpec((None, None, 6, D), lambda i: (layer, mod_row(i), 0, 0))

    row_spec = lambda w: pl.BlockSpec((TM, w), lambda i: (i, 0))
    full = lambda a: pl.BlockSpec(a.shape, lambda i: (0,) * a.ndim)
    vec = lambda a: a.reshape(1, -1)

    def router_args(layer):
        wr = jnp.pad(router_w[layer], ((0, 0), (0, LANES - N_EXPERTS))).astype(BF16)
        br = jnp.pad(router_b[layer], (0, LANES - N_EXPERTS)).reshape(1, LANES)
        return wr, br

    moe_out_shapes = [
        jax.ShapeDtypeStruct((t, D), F32),
        jax.ShapeDtypeStruct((t, D // 2), jnp.uint32),
        jax.ShapeDtypeStruct((t, LANES), F32),
        jax.ShapeDtypeStruct((8, LANES), F32),
    ]
    moe_out_specs = [row_spec(D), row_spec(D // 2), row_spec(LANES), pl.BlockSpec((8, LANES), lambda i: (0, 0))]

    j = 0
    w = ab_w_in[j]
    zc = lambda n: jnp.zeros((D, n), F32)
    misc_w = jnp.concatenate([w[:, 2464:2480], zc(ROPE_LANE0 - 16), w[:, 384:416], zc(LANES - ROPE_LANE0 - ROPE_DIM)], axis=1)
    w_in = jnp.concatenate([w[:, :384], misc_w, w[:, 416:2464]], axis=1).astype(BF16)
    wq = jnp.pad(mla_w_uq[j].reshape(Q_RANK, H_A, NOPE + ROPE_DIM), ((0, 0), (0, 0), (0, HEAD_PAD - NOPE - ROPE_DIM)))
    wq = wq.reshape(Q_RANK, H_A * HEAD_PAD).astype(BF16)
    wkv = mla_w_ukv[j].reshape(KV_RANK, H_A, NOPE + V_A)
    wk = jnp.pad(wkv[:, :, :NOPE], ((0, 0), (0, 0), (0, HEAD_PAD - NOPE))).reshape(KV_RANK, H_A * HEAD_PAD).astype(BF16)
    wv = wkv[:, :, NOPE:].reshape(KV_RANK, H_A * V_A).astype(BF16)
    gate_b = jnp.pad(mlstm_gate_b[j], (0, LANES - 4 * H_B)).reshape(1, LANES)
    tabs = _rope_tables(ss)
    pos_blocks = ss // TM

    def tab_block(i):
        return jnp.where(i < npt, pos_blocks, (i - npt) % tps)

    ins = [x, mod, vec(norm1_g[0]), w_in, vec(mla_q_norm_g[j]), wq, vec(mla_kv_norm_g[j]), wk, wv, gate_b, tabs]
    in_specs = [row_spec(D), mod_spec(0)] + [full(a) for a in ins[2:10]] + [
        pl.BlockSpec((3, TM, LANES), lambda i: (0, tab_block(i), 0))]
    q, k, v, ckv, misc, mqkv, mo = pl.pallas_call(
        _ab_in_kernel,
        grid=(nt,),
        in_specs=in_specs,
        out_specs=[row_spec(H_A * HEAD_PAD), row_spec(H_A * HEAD_PAD), row_spec(H_A * V_A), row_spec(KV_RANK),
                   row_spec(LANES), row_spec(3 * MB), row_spec(MB)],
        out_shape=[
            jax.ShapeDtypeStruct((t, H_A * HEAD_PAD), BF16), jax.ShapeDtypeStruct((t, H_A * HEAD_PAD), BF16),
            jax.ShapeDtypeStruct((t, H_A * V_A), BF16), jax.ShapeDtypeStruct((t, KV_RANK), F32),
            jax.ShapeDtypeStruct((t, LANES), F32), jax.ShapeDtypeStruct((t, 3 * MB), BF16),
            jax.ShapeDtypeStruct((t, MB), F32)],
        compiler_params=_cparams(("arbitrary",)),
        name="ab_in_proj",
    )(*ins)

    nctx = bs * past
    ctx_ckv = cache_mla_ckv[:, j].reshape(nctx, KV_RANK)
    ctx_kr = jnp.pad(cache_mla_krope[:, j].reshape(nctx, ROPE_DIM), ((0, 0), (ROPE_LANE0, LANES - ROPE_LANE0 - ROPE_DIM)))
    k_ctx, v_ctx = pl.pallas_call(
        _ctx_kv_kernel,
        grid=(nctx // TM,),
        in_specs=[row_spec(KV_RANK), row_spec(LANES), full(wk), full(wv)],
        out_specs=[row_spec(H_A * HEAD_PAD), row_spec(H_A * V_A)],
        out_shape=[jax.ShapeDtypeStruct((nctx, H_A * HEAD_PAD), BF16), jax.ShapeDtypeStruct((nctx, H_A * V_A), BF16)],
        compiler_params=_cparams(("arbitrary",)),
        name="ctx_kv",
    )(ctx_ckv, ctx_kr, wk, wv)

    o_p = _attention(q[:tp].reshape(bp, sp_, -1), k[:tp].reshape(bp, sp_, -1), v[:tp].reshape(bp, sp_, -1))
    k_s = jnp.concatenate([k_ctx.reshape(bs, past, -1), k[tp:].reshape(bs, ss, -1)], axis=1)
    v_s = jnp.concatenate([v_ctx.reshape(bs, past, -1), v[tp:].reshape(bs, ss, -1)], axis=1)
    o_s = _attention(q[tp:].reshape(bs, ss, -1), k_s, v_s)
    o_a = jnp.concatenate([o_p.reshape(tp, -1), o_s.reshape(ts, -1)], axis=0)

    zc0 = jnp.zeros((bp, 2, H_B, DH_B, DH_B), F32)
    zn0 = jnp.zeros((bp, 2, H_B, 1, DH_B), F32)
    zm0 = jnp.zeros((bp, 2, H_B, 1, LANES), F32)
    h_p, c_p, n_p, m_p = _mlstm(mqkv, misc, zc0, zn0, zm0, 0, bp, sp_)
    sm0 = jnp.broadcast_to(state_mlstm_m[:, j][..., None, None], (bs, 2, H_B, 1, LANES))
    h_s, _, _, _ = _mlstm(mqkv, misc, state_mlstm_C[:, j], state_mlstm_n[:, j][:, :, :, None, :], sm0, tp, bs, ss)
    h_dir = jnp.concatenate([h_p, h_s], axis=1)

    wr, br = router_args(0)
    ins = [o_a, h_dir[0], h_dir[1], mo, vec(mlstm_norm_g[j]), ab_w_out[j].astype(BF16), x, mod, vec(norm2_g[0]), wr, br]
    in_specs = [row_spec(H_A * V_A), row_spec(MB), row_spec(MB), row_spec(MB), full(ins[4]), full(ins[5]), row_spec(D),
                mod_spec(0), full(ins[8]), full(wr), full(br)]
    x1, pk, route, cnt = pl.pallas_call(
        _ab_out_kernel,
        grid=(nt,),
        in_specs=in_specs,
        out_specs=moe_out_specs,
        out_shape=moe_out_shapes,
        compiler_params=_cparams(("arbitrary",)),
        name="ab_out_proj",
    )(*ins)
    dest, y = _moe(route, cnt, pk, moe_w_in[0], moe_b_in[0], moe_w_out[0], moe_b_out[0])

    gather_scratch = [pltpu.SMEM((TM * TOP_K,), I32), pltpu.VMEM((TOP_K, TM, D), F32),
                      pltpu.SemaphoreType.DMA, pltpu.SemaphoreType.DMA]
    any_spec = pl.BlockSpec(memory_space=pl.ANY)
    ins = [dest, y, x1, route, mod, mod, vec(norm1_g[1]), conv_w_pw1[0].astype(BF16)]
    x2, u = pl.pallas_call(
        _conv_in_kernel,
        grid=(nt,),
        in_specs=[any_spec, any_spec, row_spec(D), row_spec(LANES), mod_spec(0), mod_spec(1), full(ins[6]), full(ins[7])],
        out_specs=[row_spec(D), row_spec(D)],
        out_shape=[jax.ShapeDtypeStruct((t, D), F32), jax.ShapeDtypeStruct((t, D), F32)],
        scratch_shapes=gather_scratch,
        compiler_params=_cparams(("arbitrary",)),
        name="conv_in_proj",
    )(*ins)

    hpt = TM // HALO
    nhalo = t // HALO
    wr, br = router_args(1)
    ins = [u, u, u, conv_w_dw[0], vec(conv_b_dw[0]), vec(conv_ln_g[0]), vec(conv_ln_b[0]), conv_w_pw2[0].astype(BF16),
           x2, mod, vec(norm2_g[1]), wr, br]
    in_specs = [row_spec(D),
                pl.BlockSpec((HALO, D), lambda i: (jnp.maximum(i * hpt - 1, 0), 0)),
                pl.BlockSpec((HALO, D), lambda i: (jnp.minimum((i + 1) * hpt, nhalo - 1), 0)),
                full(ins[3]), full(ins[4]), full(ins[5]), full(ins[6]), full(ins[7]), row_spec(D), mod_spec(1),
                full(ins[10]), full(wr), full(br)]
    x1, pk, route, cnt = pl.pallas_call(
        functools.partial(_conv_out_kernel, npt, tps),
        grid=(nt,),
        in_specs=in_specs,
        out_specs=moe_out_specs,
        out_shape=moe_out_shapes,
        scratch_shapes=[pltpu.VMEM((TM + 2 * HALO, D), F32)],
        compiler_params=_cparams(("arbitrary",)),
        name="conv_out_proj",
    )(*ins)
    dest, y = _moe(route, cnt, pk, moe_w_in[1], moe_b_in[1], moe_w_out[1], moe_b_out[1])

    ins = [dest, y, x1, route, mod, vec(final_norm_g)]
    out = pl.pallas_call(
        _final_kernel,
        grid=(nt,),
        in_specs=[any_spec, any_spec, row_spec(D), row_spec(LANES), mod_spec(1), full(ins[5])],
        out_specs=row_spec(D),
        out_shape=jax.ShapeDtypeStruct((t, D), F32),
        scratch_shapes=gather_scratch,
        compiler_params=_cparams(("arbitrary",)),
        name="final_norm",
    )(*ins)

    y_prompt = out[:tp].reshape(bp, sp_, D)
    y_sample = out[tp:].reshape(bs, ss, D)
    new_ckv = ckv[:tp].reshape(bp, 1, sp_, KV_RANK)
    new_krope = misc[:tp, ROPE_LANE0:ROPE_LANE0 + ROPE_DIM].reshape(bp, 1, sp_, ROPE_DIM)
    new_c = c_p[:, None]
    new_n = n_p[:, None, :, :, 0, :]
    new_m = m_p[:, None, :, :, 0, 0]
    return (y_prompt, y_sample, new_ckv, new_krope, new_c, new_n, new_m)
```

```python
import functools

import jax
import jax.numpy as jnp
from jax import lax
from jax.experimental import pallas as pl
from jax.experimental.pallas import tpu as pltpu

F32 = jnp.float32
BF16 = jnp.bfloat16
I32 = jnp.int32
U32 = jnp.uint32

D = 1024
GRID_W = 64
H_A = 8
Q_RANK = 256
KV_RANK = 128
NOPE = 64
ROPE_DIM = 32
V_A = 64
ROPE_BASE = 10000.0
H_B = 4
DH_B = 128
MB = H_B * DH_B
CONV_K = 31
N_EXPERTS = 32
TOP_K = 4
D_FF = 1024
SWIGLU_LIMIT = 7.0
SWIGLU_ALPHA = 1.702

LANES = 128
SUBLANES = 8
HEAD_PAD = 128
ROPE_LANE0 = 64
N_COND = 16
CTX_ROW = 8

TM = 256
FB = 512
LC = 128
TQ = 256
CH = SUBLANES
SLOTS = TOP_K * TM + N_EXPERTS * CH
TAB = 1024
VMEM_LIMIT = 56 * 1024 * 1024

NEG_INF = float("-inf")


def _cparams(sem):
    return pltpu.CompilerParams(dimension_semantics=sem, vmem_limit_bytes=VMEM_LIMIT)


def _sigmoid(x):
    return 1.0 / (1.0 + jnp.exp(-x))


def _rms(x, g, eps=1e-6):
    return x * lax.rsqrt(jnp.mean(x * x, axis=-1, keepdims=True) + eps) * g


def _lane(shape):
    return lax.broadcasted_iota(I32, shape, len(shape) - 1)


def _pack_pairs(x):
    w = x.shape[1] // 2
    return pltpu.bitcast(x[:, :w], U32) | (pltpu.bitcast(x[:, w:], U32) >> 16)


def _unpack_pairs(wd):
    a = pltpu.bitcast(wd & jnp.uint32(0xFFFF0000), F32).astype(BF16)
    b = pltpu.bitcast(wd << 16, F32).astype(BF16)
    return a, b


def _mod_kernel(cond_ref, w_ref, b_ref, o_ref):
    c = cond_ref[...]
    s = (c * _sigmoid(c)).astype(BF16)
    o_ref[...] = jnp.dot(s, w_ref[...].astype(BF16), preferred_element_type=F32) + b_ref[...]


def _modulation(cond, ada_w, ada_b):
    depth = ada_w.shape[0]
    nj = ada_w.shape[2] // D
    out = pl.pallas_call(
        _mod_kernel,
        grid=(depth, nj),
        in_specs=[
            pl.BlockSpec((N_COND, D), lambda l, j: (0, 0)),
            pl.BlockSpec((None, D, D), lambda l, j: (l, 0, j)),
            pl.BlockSpec((None, 1, D), lambda l, j: (l, 0, j)),
        ],
        out_specs=pl.BlockSpec((None, N_COND, D), lambda l, j: (l, 0, j)),
        out_shape=jax.ShapeDtypeStruct((depth, N_COND, nj * D), F32),
        compiler_params=_cparams(("arbitrary", "arbitrary")),
        name="ada_modulation",
    )(cond, ada_w, ada_b.reshape(depth, 1, nj * D))
    return out.reshape(depth, N_COND, nj, D)


def _load_table(tab_hbm, tab_smem, sem_tab, tile):
    cp = pltpu.make_async_copy(tab_hbm.at[pl.ds(tile * TAB, TAB)], tab_smem, sem_tab)
    cp.start()
    cp.wait()


def _move_chunks(tab_smem, copy_for):
    total = jnp.int32(0)
    for e in range(N_EXPERTS):
        nch = tab_smem[e]
        row = tab_smem[N_EXPERTS + e]
        slot = tab_smem[2 * N_EXPERTS + e]

        def body(j, carry, row=row, slot=slot):
            copy_for(pl.multiple_of(row + j * CH, CH), pl.multiple_of(slot + j * CH, CH)).start()
            return carry

        lax.fori_loop(0, nch, body, 0)
        total = total + nch

    def wait_body(j, carry):
        copy_for(0, 0).wait()
        return carry

    lax.fori_loop(0, total, wait_body, 0)


def _gather_expert_rows(tab_hbm, ys_hbm, tab_smem, yc, sem_tab, sem, tile):
    @pl.when(tile == 0)
    def _():
        yc[...] = jnp.zeros_like(yc)

    _load_table(tab_hbm, tab_smem, sem_tab, tile)
    _move_chunks(tab_smem, lambda row, slot: pltpu.make_async_copy(
        ys_hbm.at[pl.ds(row, CH), :], yc.at[pl.ds(slot, CH), :], sem))


def _combine(route, yc):
    lane = _lane((route.shape[0], SLOTS)).astype(F32)
    g = jnp.where(lane == route[:, 8:9], route[:, 4:5], 0.0)
    for k in range(1, TOP_K):
        g = g + jnp.where(lane == route[:, 8 + k:9 + k], route[:, 4 + k:5 + k], 0.0)
    gb = g.astype(BF16)
    ya, yb = _unpack_pairs(yc[...])
    return jnp.concatenate([jnp.dot(gb, ya, preferred_element_type=F32),
                            jnp.dot(gb, yb, preferred_element_type=F32)], axis=-1)


def _moe_route(x1, mod_ref, n2g, wr_ref, br_ref, xc_ref, route_ref, cnt_ref):
    h2 = _rms(x1, n2g) * (1.0 + mod_ref[4:5, :]) + mod_ref[3:4, :]
    h2b = h2.astype(BF16)
    logits = jnp.dot(h2b, wr_ref[...], preferred_element_type=F32) + br_ref[...]
    tm = logits.shape[0]
    lane = _lane((tm, LANES))
    lanef = lane.astype(F32)
    l = jnp.where(lane < N_EXPERTS, logits, NEG_INF)
    tops, idxs, sels = [], [], []
    for _ in range(TOP_K):
        mv = jnp.max(l, axis=-1, keepdims=True)
        idx = jnp.min(jnp.where(l == mv, lanef, float(LANES)), axis=-1, keepdims=True)
        sel = lanef == idx
        l = jnp.where(sel, NEG_INF, l)
        tops.append(mv)
        idxs.append(idx)
        sels.append(sel)
    ex = [jnp.exp(t - tops[0]) for t in tops]
    den = ex[0] + ex[1] + ex[2] + ex[3]
    gates = [e / den for e in ex]

    onehot = jnp.zeros((tm, LANES), F32)
    for sel in sels:
        onehot = onehot + jnp.where(sel, 1.0, 0.0)
    row = lax.broadcasted_iota(I32, (tm, tm), 0)
    col = lax.broadcasted_iota(I32, (tm, tm), 1)
    strict = jnp.where(col < row, 1.0, 0.0).astype(BF16)
    before = jnp.dot(strict, onehot.astype(BF16), preferred_element_type=F32)
    n_e = jnp.sum(onehot, axis=0, keepdims=True)
    cnt_ref[...] = jnp.broadcast_to(n_e, cnt_ref.shape)
    nch = jnp.right_shift(n_e.astype(I32) + (CH - 1), CH.bit_length() - 1).astype(F32)
    r128 = lax.broadcasted_iota(I32, (LANES, LANES), 0)
    c128 = lax.broadcasted_iota(I32, (LANES, LANES), 1)
    upper = jnp.where(r128 < c128, 1.0, 0.0).astype(BF16)
    seg0 = float(CH) * jnp.dot(jnp.broadcast_to(nch, (SUBLANES, LANES)).astype(BF16), upper,
                               preferred_element_type=F32)[0:1, :]
    slot_of = seg0 + before
    slots = [jnp.sum(jnp.where(sel, slot_of, 0.0), axis=-1, keepdims=True) for sel in sels]

    r = jnp.zeros((tm, LANES), F32)
    for j, colv in enumerate(idxs + gates + slots):
        r = jnp.where(lane == j, colv, r)
    route_ref[...] = r

    lane_s = _lane((tm, SLOTS)).astype(F32)
    pt = jnp.where(lane_s == slots[0], 1.0, 0.0)
    for k in range(1, TOP_K):
        pt = pt + jnp.where(lane_s == slots[k], 1.0, 0.0)
    xc = jnp.dot(pt.T.astype(BF16), h2b, preferred_element_type=F32)
    xc_ref[...] = _pack_pairs(xc)


def _rope(x, tab_ref):
    return x * tab_ref[0] + pltpu.roll(x, LANES - 8, 1) * tab_ref[1] + pltpu.roll(x, 8, 1) * tab_ref[2]


def _ab_in_kernel(npt, xp_ref, xs_ref, mod_ref, n1g_ref, win_ref, qg_ref, wq_ref, kvg_ref, wk_ref, wv_ref, gb_ref,
                  tab_ref, q_ref, k_ref, v_ref, ckv_ref, misc_ref, mqkv_ref, mo_ref):
    x = jnp.where(pl.program_id(0) < npt, xp_ref[...], xs_ref[...])
    h = _rms(x, n1g_ref[...]) * (1.0 + mod_ref[1:2, :]) + mod_ref[0:1, :]
    z = jnp.dot(h.astype(BF16), win_ref[...], preferred_element_type=F32)

    qn = _rms(z[:, :Q_RANK], qg_ref[...]).astype(BF16)
    q = jnp.dot(qn, wq_ref[...], preferred_element_type=F32)
    ckv = _rms(z[:, Q_RANK:Q_RANK + KV_RANK], kvg_ref[...])
    ckv_ref[...] = ckv
    ckvb = ckv.astype(BF16)
    kn = jnp.dot(ckvb, wk_ref[...], preferred_element_type=F32)
    v_ref[...] = jnp.dot(ckvb, wv_ref[...], preferred_element_type=F32).astype(BF16)

    misc = z[:, Q_RANK + KV_RANK:Q_RANK + KV_RANK + LANES]
    misc_ref[...] = misc + gb_ref[...]
    lane = _lane(misc.shape)
    kr = jnp.where((lane >= ROPE_LANE0) & (lane < ROPE_LANE0 + ROPE_DIM), _rope(misc, tab_ref), 0.0)
    scale = float((NOPE + ROPE_DIM) ** -0.5)
    for hh in range(H_A):
        sl = slice(hh * HEAD_PAD, (hh + 1) * HEAD_PAD)
        q_ref[:, sl] = (_rope(q[:, sl], tab_ref) * scale).astype(BF16)
        k_ref[:, sl] = (kn[:, sl] + kr).astype(BF16)

    m0 = Q_RANK + KV_RANK + LANES
    mqkv_ref[:, :MB] = z[:, m0:m0 + MB].astype(BF16)
    mqkv_ref[:, MB:2 * MB] = (z[:, m0 + MB:m0 + 2 * MB] * float(DH_B ** -0.5)).astype(BF16)
    mqkv_ref[:, 2 * MB:] = z[:, m0 + 2 * MB:m0 + 3 * MB].astype(BF16)
    mo_ref[...] = z[:, m0 + 3 * MB:m0 + 4 * MB]


def _ctx_kv_kernel(ckv_ref, kr_ref, wk_ref, wv_ref, k_ref, v_ref):
    ckvb = ckv_ref[...].astype(BF16)
    kn = jnp.dot(ckvb, wk_ref[...], preferred_element_type=F32)
    v_ref[...] = jnp.dot(ckvb, wv_ref[...], preferred_element_type=F32).astype(BF16)
    kr = kr_ref[...]
    for hh in range(H_A):
        sl = slice(hh * HEAD_PAD, (hh + 1) * HEAD_PAD)
        k_ref[:, sl] = (kn[:, sl] + kr).astype(BF16)


def _attn_kernel(has_ctx, has_prev, *refs):
    q_ref, k_ref, v_ref = refs[:3]
    kc_ref, vc_ref = (refs[3], refs[4]) if has_ctx else (None, None)
    o_ref = refs[3 + 2 * has_ctx + has_prev]
    v = v_ref[...]
    outs = []
    for j in range(2):
        sl = slice(j * HEAD_PAD, (j + 1) * HEAD_PAD)
        q = q_ref[:, sl]
        nt_dims = (((1,), (1,)), ((), ()))
        s = lax.dot_general(q, k_ref[:, sl], nt_dims, preferred_element_type=F32)
        m = jnp.max(s, axis=-1, keepdims=True)
        if has_ctx:
            sc = lax.dot_general(q, kc_ref[:, sl], nt_dims, preferred_element_type=F32)
            m = jnp.maximum(m, jnp.max(sc, axis=-1, keepdims=True))
        p = jnp.exp(s - m)
        l = jnp.sum(p, axis=-1, keepdims=True)
        o = jnp.dot(p.astype(BF16), v, preferred_element_type=F32)
        if has_ctx:
            pc = jnp.exp(sc - m)
            l = l + jnp.sum(pc, axis=-1, keepdims=True)
            o = o + jnp.dot(pc.astype(BF16), vc_ref[...], preferred_element_type=F32)
        outs.append(o / l)
    lane = _lane(outs[0].shape)
    o_ref[...] = jnp.where(lane < V_A, outs[0], outs[1]).astype(BF16)


def _attention(q, k, v, row0, nb, s, ctx=None, prev=None):
    t = q.shape[0]
    tq = min(TQ, s)
    nq = s // tq
    qrow = lambda bi, hp, qi: (row0 // tq + bi * nq + qi, hp)
    krow = lambda bi, hp, qi: (row0 // s + bi, hp)
    ins = [q, k, v]
    in_specs = [pl.BlockSpec((tq, 2 * HEAD_PAD), qrow), pl.BlockSpec((s, 2 * HEAD_PAD), krow),
                pl.BlockSpec((s, 2 * V_A), krow)]
    if ctx is not None:
        past = ctx[0].shape[0] // nb
        ins += list(ctx)
        in_specs += [pl.BlockSpec((past, 2 * HEAD_PAD), lambda bi, hp, qi: (bi, hp)),
                     pl.BlockSpec((past, 2 * V_A), lambda bi, hp, qi: (bi, hp))]
    aliases = {}
    if prev is not None:
        aliases = {len(ins): 0}
        ins.append(prev)
        in_specs.append(pl.BlockSpec(memory_space=pl.ANY))
    return pl.pallas_call(
        functools.partial(_attn_kernel, ctx is not None, prev is not None),
        grid=(nb, H_A // 2, nq),
        in_specs=in_specs,
        out_specs=pl.BlockSpec((tq, 2 * V_A), qrow),
        out_shape=jax.ShapeDtypeStruct((t, H_A * V_A), BF16),
        input_output_aliases=aliases,
        compiler_params=_cparams(("arbitrary", "arbitrary", "arbitrary")),
        name="mla_attention",
    )(*ins)


def _log_sigmoid(x):
    return jnp.minimum(x, 0.0) - jnp.log(1.0 + jnp.exp(-jnp.abs(x)))


def _mlstm_kernel(has_prev, q_ref, k_ref, v_ref, g_ref, c0_ref, n0_ref, m0_ref, *refs):
    h_ref, c_out, n_out, m_out, c_s, n_s, m_s = refs[has_prev:]
    d = pl.program_id(1)
    c = pl.program_id(2)
    fwd = d == 0

    @pl.when(c == 0)
    def _():
        c_s[...] = c0_ref[...]
        n_s[...] = n0_ref[...]
        m_s[...] = m0_ref[...]

    gates = g_ref[...]
    lf_all = _log_sigmoid(gates)
    row = lax.broadcasted_iota(I32, (LC, LC), 0)
    col = lax.broadcasted_iota(I32, (LC, LC), 1)
    sgn = jnp.where(fwd, 1, -1)
    mask = (col - row) * sgn <= 0
    eye = row == col
    cum = jnp.dot(jnp.where(mask, 1.0, 0.0), lf_all, preferred_element_type=F32,
                  precision=lax.Precision.HIGHEST)

    for hh in range(H_B):
        sl = slice(hh * DH_B, (hh + 1) * DH_B)
        q = q_ref[:, sl]
        k = k_ref[:, sl]
        v = v_ref[:, sl]
        b_col = jnp.where(fwd, cum[:, 4 + hh:5 + hh], cum[:, 12 + hh:13 + hh])
        i_col = jnp.where(fwd, gates[:, hh:hh + 1], gates[:, 8 + hh:9 + hh])
        m_prev = m_s[hh][:, 0:1]
        r_row = jnp.sum(jnp.where(eye, i_col - b_col, 0.0), axis=0, keepdims=True)
        dm = jnp.where(mask, b_col + r_row, NEG_INF)
        a_col = b_col + m_prev
        mt = jnp.maximum(a_col, jnp.max(dm, axis=-1, keepdims=True))
        w_inter = jnp.exp(a_col - mt)
        qk = lax.dot_general(q, k, (((1,), (1,)), ((), ())), preferred_element_type=F32)
        s = qk * jnp.exp(dm - mt)
        c_prev = c_s[hh]
        n_prev = n_s[hh]
        num = w_inter * jnp.dot(q, c_prev.astype(BF16), preferred_element_type=F32) \
            + jnp.dot(s.astype(BF16), v, preferred_element_type=F32)
        qf = q.astype(F32)
        den = w_inter * jnp.sum(qf * n_prev, axis=-1, keepdims=True) + jnp.sum(s, axis=-1, keepdims=True)
        h_ref[:, sl] = num / jnp.maximum(jnp.abs(den), jnp.exp(-mt))

        b_tot = jnp.where(fwd, b_col[LC - 1:LC, :], b_col[0:1, :])
        g_col = b_tot - b_col + i_col
        m_new = jnp.maximum(b_tot + m_prev, jnp.max(g_col, axis=0, keepdims=True))
        w_c = jnp.exp(b_tot + m_prev - m_new)
        kw = k.astype(F32) * jnp.exp(g_col - m_new)
        c_s[hh] = w_c * c_prev + jnp.dot(kw.T.astype(BF16), v, preferred_element_type=F32)
        n_s[hh] = w_c * n_prev + jnp.sum(kw, axis=0, keepdims=True)
        m_s[hh] = jnp.broadcast_to(m_new, (1, LANES))

    @pl.when(c == pl.num_programs(2) - 1)
    def _():
        c_out[...] = c_s[...]
        n_out[...] = n_s[...]
        m_out[...] = m_s[...]


def _mlstm(mqkv, misc, c0, n0, m0, row0, nb, s, prev=None):
    nc = s // LC
    blk0 = row0 // LC
    t = mqkv.shape[0]

    def rows(bi, di, ci):
        return blk0 + bi * nc + ci + di * (nc - 1 - 2 * ci)

    state_spec = lambda shape: pl.BlockSpec((None, None) + shape, lambda bi, di, ci: (bi, di) + (0,) * len(shape))
    ins = [mqkv, mqkv, mqkv, misc, c0, n0, m0]
    in_specs = [
        pl.BlockSpec((LC, MB), lambda bi, di, ci: (rows(bi, di, ci), 0)),
        pl.BlockSpec((LC, MB), lambda bi, di, ci: (rows(bi, di, ci), 1)),
        pl.BlockSpec((LC, MB), lambda bi, di, ci: (rows(bi, di, ci), 2)),
        pl.BlockSpec((LC, LANES), lambda bi, di, ci: (rows(bi, di, ci), 0)),
        state_spec((H_B, DH_B, DH_B)),
        state_spec((H_B, 1, DH_B)),
        state_spec((H_B, 1, LANES)),
    ]
    aliases = {}
    if prev is not None:
        aliases = {len(ins): 0}
        ins.append(prev)
        in_specs.append(pl.BlockSpec(memory_space=pl.ANY))
    return pl.pallas_call(
        functools.partial(_mlstm_kernel, prev is not None),
        grid=(nb, 2, nc),
        in_specs=in_specs,
        out_specs=[
            pl.BlockSpec((None, LC, MB), lambda bi, di, ci: (di, rows(bi, di, ci), 0)),
            state_spec((H_B, DH_B, DH_B)),
            state_spec((H_B, 1, DH_B)),
            state_spec((H_B, 1, LANES)),
        ],
        out_shape=[
            jax.ShapeDtypeStruct((2, t, MB), F32),
            jax.ShapeDtypeStruct((nb, 2, H_B, DH_B, DH_B), F32),
            jax.ShapeDtypeStruct((nb, 2, H_B, 1, DH_B), F32),
            jax.ShapeDtypeStruct((nb, 2, H_B, 1, LANES), F32),
        ],
        scratch_shapes=[
            pltpu.VMEM((H_B, DH_B, DH_B), F32),
            pltpu.VMEM((H_B, 1, DH_B), F32),
            pltpu.VMEM((H_B, 1, LANES), F32),
        ],
        input_output_aliases=aliases,
        compiler_params=_cparams(("arbitrary", "arbitrary", "arbitrary")),
        name="mlstm_scan",
    )(*ins)


def _ab_out_kernel(npt, oa_ref, hf_ref, hb_ref, mo_ref, mg_ref, wout_ref, xp_ref, xs_ref, mod_ref, n2g_ref, wr_ref,
                   br_ref, x1_ref, xc_ref, route_ref, cnt_ref):
    hsum = hf_ref[...] + hb_ref[...]
    parts = []
    for hh in range(H_B):
        hh_ = hsum[:, hh * DH_B:(hh + 1) * DH_B]
        mu = jnp.mean(hh_, axis=-1, keepdims=True)
        var = jnp.mean(jnp.square(hh_ - mu), axis=-1, keepdims=True)
        parts.append((hh_ - mu) * lax.rsqrt(var + 1e-5))
    hn = jnp.concatenate(parts, axis=-1) * mg_ref[...]
    ob = (hn * _sigmoid(mo_ref[...])).astype(BF16)
    o = jnp.dot(oa_ref[...], wout_ref[:H_A * V_A, :], preferred_element_type=F32) \
        + jnp.dot(ob, wout_ref[H_A * V_A:, :], preferred_element_type=F32)
    x = jnp.where(pl.program_id(0) < npt, xp_ref[...], xs_ref[...])
    x1 = x + mod_ref[2:3, :] * o
    x1_ref[...] = x1
    _moe_route(x1, mod_ref, n2g_ref[...], wr_ref, br_ref, xc_ref, route_ref, cnt_ref)


def _conv_in_kernel(tab_hbm, ys_hbm, x1_ref, route_ref, modp_ref, mod_ref, n1g_ref, w_ref,
                    x2_ref, u_ref, tab_smem, yc, sem_tab, sem):
    _gather_expert_rows(tab_hbm, ys_hbm, tab_smem, yc, sem_tab, sem, pl.program_id(0))
    x2 = x1_ref[...] + modp_ref[5:6, :] * _combine(route_ref[...], yc)
    x2_ref[...] = x2
    h = _rms(x2, n1g_ref[...]) * (1.0 + mod_ref[1:2, :]) + mod_ref[0:1, :]
    z = jnp.dot(h.astype(BF16), w_ref[...], preferred_element_type=F32)
    u_ref[...] = z[:, :D] * _sigmoid(z[:, D:])


HALO = 16


def _conv_out_kernel(npt, tiles_per_seq, u_ref, up_ref, un_ref, wdw_ref, bdw_ref, lg_ref, lb_ref, w2_ref,
                     x_ref, mod_ref, n2g_ref, wr_ref, br_ref, x1_ref, xc_ref, route_ref, cnt_ref, ext_ref):
    i = pl.program_id(0)
    j = i - npt
    in_prompt = i < npt
    first = in_prompt | (j % tiles_per_seq == 0)
    last = in_prompt | (j % tiles_per_seq == tiles_per_seq - 1)
    ext_ref[0:HALO, :] = jnp.where(first, 0.0, up_ref[...])
    ext_ref[HALO:HALO + TM, :] = u_ref[...]
    ext_ref[HALO + TM:, :] = jnp.where(last, 0.0, un_ref[...])

    acc = jnp.zeros((TM, D), F32) + bdw_ref[...]
    for b in range(8):
        eb = ext_ref[b + 1:b + 1 + TM + 24, :]
        for a in range(4):
            tap = 8 * a + b
            if tap < CONV_K:
                acc = acc + wdw_ref[tap:tap + 1, :] * eb[8 * a:8 * a + TM, :]
    mu = jnp.mean(acc, axis=-1, keepdims=True)
    var = jnp.mean(jnp.square(acc - mu), axis=-1, keepdims=True)
    ln = (acc - mu) * lax.rsqrt(var + 1e-5) * lg_ref[...] + lb_ref[...]
    act = (ln * _sigmoid(ln)).astype(BF16)
    o = jnp.dot(act, w2_ref[...], preferred_element_type=F32)
    x1 = x_ref[...] + mod_ref[2:3, :] * o
    x1_ref[...] = x1
    _moe_route(x1, mod_ref, n2g_ref[...], wr_ref, br_ref, xc_ref, route_ref, cnt_ref)


def _dispatch_kernel(tab_hbm, xc_hbm, xs_in, xs_hbm, tab_smem, sem_tab, sem):
    del xs_in
    i = pl.program_id(0)
    _load_table(tab_hbm, tab_smem, sem_tab, i)
    _move_chunks(tab_smem, lambda row, slot: pltpu.make_async_copy(
        xc_hbm.at[pl.ds(i * SLOTS + slot, CH), :], xs_hbm.at[pl.ds(row, CH), :], sem))


def _dispatch(tab, xc, n_rows):
    nt = xc.shape[0] // SLOTS
    any_spec = pl.BlockSpec(memory_space=pl.ANY)
    return pl.pallas_call(
        _dispatch_kernel,
        grid=(nt,),
        in_specs=[any_spec, any_spec, any_spec],
        out_specs=any_spec,
        out_shape=jax.ShapeDtypeStruct((n_rows, D // 2), U32),
        scratch_shapes=[pltpu.SMEM((TAB,), I32), pltpu.SemaphoreType.DMA, pltpu.SemaphoreType.DMA],
        input_output_aliases={2: 0},
        compiler_params=_cparams(("arbitrary",)),
        name="moe_dispatch",
    )(tab, xc, jnp.zeros((n_rows, D // 2), U32))


def _ffn_kernel(be_ref, nused_ref, xs_ref, win_ref, bin_ref, wout_ref, bout_ref, y_ref, win_s, wout_s):
    i = pl.program_id(0)
    e = be_ref[i]
    e_prev = be_ref[jnp.maximum(i - 1, 0)]

    @pl.when((i == 0) | (e != e_prev))
    def _():
        win_s[...] = win_ref[...].astype(BF16)
        wout_s[...] = wout_ref[...].astype(BF16)

    @pl.when(i < nused_ref[0])
    def _():
        xa, xb = _unpack_pairs(xs_ref[...])
        hb = jnp.dot(xa, win_s[:D // 2, :], preferred_element_type=F32) \
            + jnp.dot(xb, win_s[D // 2:, :], preferred_element_type=F32) + bin_ref[...]
        g = jnp.minimum(hb[:, :D_FF], SWIGLU_LIMIT)
        u = jnp.clip(hb[:, D_FF:], -SWIGLU_LIMIT, SWIGLU_LIMIT)
        act = g * _sigmoid(SWIGLU_ALPHA * g) * (u + 1.0)
        y = jnp.dot(act.astype(BF16), wout_s[...], preferred_element_type=F32) + bout_ref[...]
        y_ref[...] = _pack_pairs(y.astype(BF16).astype(F32))

    @pl.when(i >= nused_ref[0])
    def _():
        y_ref[...] = jnp.zeros_like(y_ref)


def _expert_ffn(block_e, n_used, xs, w_in, b_in, w_out, b_out):
    n_blocks = xs.shape[0] // FB
    return pl.pallas_call(
        _ffn_kernel,
        grid_spec=pltpu.PrefetchScalarGridSpec(
            num_scalar_prefetch=2,
            grid=(n_blocks,),
            in_specs=[
                pl.BlockSpec((FB, D // 2), lambda i, be, nu: (i, 0)),
                pl.BlockSpec((None, D, 2 * D_FF), lambda i, be, nu: (be[i], 0, 0)),
                pl.BlockSpec((None, 1, 2 * D_FF), lambda i, be, nu: (be[i], 0, 0)),
                pl.BlockSpec((None, D_FF, D), lambda i, be, nu: (be[i], 0, 0)),
                pl.BlockSpec((None, 1, D), lambda i, be, nu: (be[i], 0, 0)),
            ],
            out_specs=pl.BlockSpec((FB, D // 2), lambda i, be, nu: (i, 0)),
            scratch_shapes=[pltpu.VMEM((D, 2 * D_FF), BF16), pltpu.VMEM((D_FF, D), BF16)],
        ),
        out_shape=jax.ShapeDtypeStruct((n_blocks * FB, D // 2), U32),
        compiler_params=_cparams(("arbitrary",)),
        name="moe_expert_ffn",
    )(block_e, n_used, xs, w_in, b_in.reshape(N_EXPERTS, 1, 2 * D_FF), w_out, b_out.reshape(N_EXPERTS, 1, D))


def _moe_plan(cnt):
    nt = cnt.shape[0]
    n_blocks = (nt * TM * TOP_K + nt * N_EXPERTS * (CH - 1)) // FB + N_EXPERTS + 1
    n = cnt[:, 0, :N_EXPERTS].astype(I32)
    nch = (n + CH - 1) // CH
    padded = (CH * jnp.sum(nch, axis=0) + FB - 1) // FB * FB
    pad_ends = jnp.cumsum(padded)
    pad_starts = pad_ends - padded
    first_row = pad_starts[None, :] + CH * (jnp.cumsum(nch, axis=0) - nch)
    first_slot = CH * (jnp.cumsum(nch, axis=1) - nch)
    tab = jnp.concatenate([nch, first_row, first_slot, jnp.zeros((nt, TAB - 3 * N_EXPERTS), I32)], axis=1)
    blk = jnp.arange(n_blocks, dtype=I32) * FB
    block_e = jnp.minimum(jnp.sum((blk[:, None] >= pad_ends[None, :]).astype(I32), axis=-1), N_EXPERTS - 1)
    n_used = (pad_ends[-1] // FB).reshape(1)
    return tab.reshape(nt * TAB), block_e.astype(I32), n_used.astype(I32), n_blocks * FB


def _moe(cnt, xc, w_in, b_in, w_out, b_out):
    tab, block_e, n_used, n_rows = _moe_plan(cnt)
    xs = _dispatch(tab, xc, n_rows)
    ys = _expert_ffn(block_e, n_used, xs, w_in, b_in, w_out, b_out)
    return tab, ys


def _final_kernel(npt, tab_hbm, ys_hbm, x1_ref, route_ref, modp_ref, g_ref, op_ref, os_ref, tab_smem, yc, sem_tab, sem):
    i = pl.program_id(0)
    _gather_expert_rows(tab_hbm, ys_hbm, tab_smem, yc, sem_tab, sem, i)
    x2 = x1_ref[...] + modp_ref[5:6, :] * _combine(route_ref[...], yc)
    out = _rms(x2, g_ref[...])

    @pl.when(i < npt)
    def _():
        op_ref[...] = out

    @pl.when(i >= npt)
    def _():
        os_ref[...] = out


def _rope_tables(n_tokens):
    rows = n_tokens // GRID_W
    pos_row = jnp.repeat(jnp.arange(rows, dtype=F32), GRID_W)
    pos_col = jnp.tile(jnp.arange(GRID_W, dtype=F32), rows)
    n_freq = ROPE_DIM // 4
    inv_freq = ROPE_BASE ** (-jnp.arange(n_freq, dtype=F32) / n_freq)
    ang = jnp.stack([pos_row[:, None] * inv_freq, pos_col[:, None] * inv_freq], axis=1)
    cos, sin = jnp.cos(ang), jnp.sin(ang)
    zero = jnp.zeros_like(sin)
    c = jnp.stack([cos, cos], axis=2).reshape(n_tokens, ROPE_DIM)
    sm = jnp.stack([-sin, zero], axis=2).reshape(n_tokens, ROPE_DIM)
    sp = jnp.stack([zero, sin], axis=2).reshape(n_tokens, ROPE_DIM)

    def place(a, fill):
        left = jnp.full((n_tokens, ROPE_LANE0), fill, F32)
        right = jnp.full((n_tokens, LANES - ROPE_LANE0 - ROPE_DIM), fill, F32)
        body = jnp.concatenate([left, a, right], axis=1)
        ident = jnp.full((TM, LANES), fill, F32)
        return jnp.concatenate([body, ident], axis=0)

    return jnp.stack([place(c, 1.0), place(sm, 0.0), place(sp, 0.0)], axis=0)


def kernel(x_prompt, x_sample, c, cache_mla_ckv, cache_mla_krope, state_mlstm_C, state_mlstm_n, state_mlstm_m, c_ctx, ada_w, ada_b, norm1_g, norm2_g, ab_w_in, mla_q_norm_g, mla_w_uq, mla_kv_norm_g, mla_w_ukv, mlstm_gate_b, mlstm_norm_g, ab_w_out, conv_w_pw1, conv_w_dw, conv_b_dw, conv_ln_g, conv_ln_b, conv_w_pw2, router_w, router_b, moe_w_in, moe_b_in, moe_w_out, moe_b_out, final_norm_g):
    bp, sp_, _ = x_prompt.shape
    bs, ss, _ = x_sample.shape
    past = cache_mla_ckv.shape[2]
    tp, ts = bp * sp_, bs * ss
    t = tp + ts
    nt = t // TM
    npt = tp // TM
    tps = ss // TM
    assert sp_ == TM and ss % TM == 0 and bs <= CTX_ROW and tp % ss == 0

    xp2 = x_prompt.reshape(tp, D)
    xs2 = x_sample.reshape(ts, D)
    cond = jnp.zeros((N_COND, D), F32).at[:bs].set(c).at[CTX_ROW].set(c_ctx)
    mod = _modulation(cond, ada_w, ada_b)

    def mod_row(i):
        return jnp.where(i < npt, CTX_ROW, (i - npt) // tps)

    def mod_spec(layer):
        return pl.BlockSpec((None, None, 6, D), lambda i: (layer, mod_row(i), 0, 0))

    row_spec = lambda w: pl.BlockSpec((TM, w), lambda i: (i, 0))
    prompt_spec = pl.BlockSpec((TM, D), lambda i: (jnp.minimum(i, npt - 1), 0))
    sample_spec = pl.BlockSpec((TM, D), lambda i: (jnp.maximum(i - npt, 0), 0))
    full = lambda a: pl.BlockSpec(a.shape, lambda i: (0,) * a.ndim)
    vec = lambda a: a.reshape(1, -1)
    any_spec = pl.BlockSpec(memory_space=pl.ANY)

    def router_args(layer):
        wr = jnp.pad(router_w[layer], ((0, 0), (0, LANES - N_EXPERTS))).astype(BF16)
        br = jnp.pad(router_b[layer], (0, LANES - N_EXPERTS)).reshape(1, LANES)
        return wr, br

    moe_out_shapes = [
        jax.ShapeDtypeStruct((t, D), F32),
        jax.ShapeDtypeStruct((nt * SLOTS, D // 2), U32),
        jax.ShapeDtypeStruct((t, LANES), F32),
        jax.ShapeDtypeStruct((nt, SUBLANES, LANES), F32),
    ]
    moe_out_specs = [row_spec(D), pl.BlockSpec((SLOTS, D // 2), lambda i: (i, 0)), row_spec(LANES),
                     pl.BlockSpec((None, SUBLANES, LANES), lambda i: (i, 0, 0))]

    j = 0
    w = ab_w_in[j]
    zc = lambda n: jnp.zeros((D, n), F32)
    misc_w = jnp.concatenate([w[:, 2464:2480], zc(ROPE_LANE0 - 16), w[:, 384:416], zc(LANES - ROPE_LANE0 - ROPE_DIM)], axis=1)
    w_in = jnp.concatenate([w[:, :384], misc_w, w[:, 416:2464]], axis=1).astype(BF16)
    wq = jnp.pad(mla_w_uq[j].reshape(Q_RANK, H_A, NOPE + ROPE_DIM), ((0, 0), (0, 0), (0, HEAD_PAD - NOPE - ROPE_DIM)))
    wq = wq.reshape(Q_RANK, H_A * HEAD_PAD).astype(BF16)
    wkv = mla_w_ukv[j].reshape(KV_RANK, H_A, NOPE + V_A)
    wk = jnp.pad(wkv[:, :, :NOPE], ((0, 0), (0, 0), (0, HEAD_PAD - NOPE))).reshape(KV_RANK, H_A * HEAD_PAD).astype(BF16)
    wv = wkv[:, :, NOPE:].reshape(KV_RANK, H_A * V_A).astype(BF16)
    gate_b = jnp.pad(mlstm_gate_b[j], (0, LANES - 4 * H_B)).reshape(1, LANES)
    tabs = _rope_tables(ss)
    pos_blocks = ss // TM

    def tab_block(i):
        return jnp.where(i < npt, pos_blocks, (i - npt) % tps)

    ins = [xp2, xs2, mod, vec(norm1_g[0]), w_in, vec(mla_q_norm_g[j]), wq, vec(mla_kv_norm_g[j]), wk, wv, gate_b, tabs]
    in_specs = [prompt_spec, sample_spec, mod_spec(0)] + [full(a) for a in ins[3:11]] + [
        pl.BlockSpec((3, TM, LANES), lambda i: (0, tab_block(i), 0))]
    q, k, v, ckv, misc, mqkv, mo = pl.pallas_call(
        functools.partial(_ab_in_kernel, npt),
        grid=(nt,),
        in_specs=in_specs,
        out_specs=[row_spec(H_A * HEAD_PAD), row_spec(H_A * HEAD_PAD), row_spec(H_A * V_A), row_spec(KV_RANK),
                   row_spec(LANES), row_spec(3 * MB), row_spec(MB)],
        out_shape=[
            jax.ShapeDtypeStruct((t, H_A * HEAD_PAD), BF16), jax.ShapeDtypeStruct((t, H_A * HEAD_PAD), BF16),
            jax.ShapeDtypeStruct((t, H_A * V_A), BF16), jax.ShapeDtypeStruct((t, KV_RANK), F32),
            jax.ShapeDtypeStruct((t, LANES), F32), jax.ShapeDtypeStruct((t, 3 * MB), BF16),
            jax.ShapeDtypeStruct((t, MB), F32)],
        compiler_params=_cparams(("arbitrary",)),
        name="ab_in_proj",
    )(*ins)

    nctx = bs * past
    ctx_ckv = cache_mla_ckv[:, j].reshape(nctx, KV_RANK)
    ctx_kr = jnp.pad(cache_mla_krope[:, j].reshape(nctx, ROPE_DIM), ((0, 0), (ROPE_LANE0, LANES - ROPE_LANE0 - ROPE_DIM)))
    k_ctx, v_ctx = pl.pallas_call(
        _ctx_kv_kernel,
        grid=(nctx // TM,),
        in_specs=[row_spec(KV_RANK), row_spec(LANES), full(wk), full(wv)],
        out_specs=[row_spec(H_A * HEAD_PAD), row_spec(H_A * V_A)],
        out_shape=[jax.ShapeDtypeStruct((nctx, H_A * HEAD_PAD), BF16), jax.ShapeDtypeStruct((nctx, H_A * V_A), BF16)],
        compiler_params=_cparams(("arbitrary",)),
        name="ctx_kv",
    )(ctx_ckv, ctx_kr, wk, wv)

    o_a = _attention(q, k, v, 0, bp, sp_, prev=jnp.zeros((t, H_A * V_A), BF16))
    o_a = _attention(q, k, v, tp, bs, ss, ctx=(k_ctx, v_ctx), prev=o_a)

    zc0 = jnp.zeros((bp, 2, H_B, DH_B, DH_B), F32)
    zn0 = jnp.zeros((bp, 2, H_B, 1, DH_B), F32)
    zm0 = jnp.zeros((bp, 2, H_B, 1, LANES), F32)
    h_dir, c_p, n_p, m_p = _mlstm(mqkv, misc, zc0, zn0, zm0, 0, bp, sp_, prev=jnp.zeros((2, t, MB), F32))
    sm0 = jnp.broadcast_to(state_mlstm_m[:, j][..., None, None], (bs, 2, H_B, 1, LANES))
    h_dir, _, _, _ = _mlstm(mqkv, misc, state_mlstm_C[:, j], state_mlstm_n[:, j][:, :, :, None, :], sm0, tp, bs, ss,
                            prev=h_dir)

    wr, br = router_args(0)
    ins = [o_a, h_dir, h_dir, mo, vec(mlstm_norm_g[j]), ab_w_out[j].astype(BF16), xp2, xs2, mod, vec(norm2_g[0]), wr, br]
    in_specs = [row_spec(H_A * V_A), pl.BlockSpec((None, TM, MB), lambda i: (0, i, 0)),
                pl.BlockSpec((None, TM, MB), lambda i: (1, i, 0)), row_spec(MB), full(ins[4]), full(ins[5]),
                prompt_spec, sample_spec, mod_spec(0), full(ins[9]), full(wr), full(br)]
    x1, xc, route, cnt = pl.pallas_call(
        functools.partial(_ab_out_kernel, npt),
        grid=(nt,),
        in_specs=in_specs,
        out_specs=moe_out_specs,
        out_shape=moe_out_shapes,
        compiler_params=_cparams(("arbitrary",)),
        name="ab_out_proj",
    )(*ins)
    tab, ys = _moe(cnt, xc, moe_w_in[0], moe_b_in[0], moe_w_out[0], moe_b_out[0])

    gather_scratch = [pltpu.SMEM((TAB,), I32), pltpu.VMEM((SLOTS, D // 2), U32),
                      pltpu.SemaphoreType.DMA, pltpu.SemaphoreType.DMA]
    ins = [tab, ys, x1, route, mod, mod, vec(norm1_g[1]), conv_w_pw1[0].astype(BF16)]
    x2, u = pl.pallas_call(
        _conv_in_kernel,
        grid=(nt,),
        in_specs=[any_spec, any_spec, row_spec(D), row_spec(LANES), mod_spec(0), mod_spec(1), full(ins[6]), full(ins[7])],
        out_specs=[row_spec(D), row_spec(D)],
        out_shape=[jax.ShapeDtypeStruct((t, D), F32), jax.ShapeDtypeStruct((t, D), F32)],
        scratch_shapes=gather_scratch,
        compiler_params=_cparams(("arbitrary",)),
        name="conv_in_proj",
    )(*ins)

    hpt = TM // HALO
    nhalo = t // HALO
    wr, br = router_args(1)
    ins = [u, u, u, conv_w_dw[0], vec(conv_b_dw[0]), vec(conv_ln_g[0]), vec(conv_ln_b[0]), conv_w_pw2[0].astype(BF16),
           x2, mod, vec(norm2_g[1]), wr, br]
    in_specs = [row_spec(D),
                pl.BlockSpec((HALO, D), lambda i: (jnp.maximum(i * hpt - 1, 0), 0)),
                pl.BlockSpec((HALO, D), lambda i: (jnp.minimum((i + 1) * hpt, nhalo - 1), 0)),
                full(ins[3]), full(ins[4]), full(ins[5]), full(ins[6]), full(ins[7]), row_spec(D), mod_spec(1),
                full(ins[10]), full(wr), full(br)]
    x1, xc, route, cnt = pl.pallas_call(
        functools.partial(_conv_out_kernel, npt, tps),
        grid=(nt,),
        in_specs=in_specs,
        out_specs=moe_out_specs,
        out_shape=moe_out_shapes,
        scratch_shapes=[pltpu.VMEM((TM + 2 * HALO, D), F32)],
        compiler_params=_cparams(("arbitrary",)),
        name="conv_out_proj",
    )(*ins)
    tab, ys = _moe(cnt, xc, moe_w_in[1], moe_b_in[1], moe_w_out[1], moe_b_out[1])

    ins = [tab, ys, x1, route, mod, vec(final_norm_g)]
    y_p, y_s = pl.pallas_call(
        functools.partial(_final_kernel, npt),
        grid=(nt,),
        in_specs=[any_spec, any_spec, row_spec(D), row_spec(LANES), mod_spec(1), full(ins[5])],
        out_specs=[prompt_spec, sample_spec],
        out_shape=[jax.ShapeDtypeStruct((tp, D), F32), jax.ShapeDtypeStruct((ts, D), F32)],
        scratch_shapes=gather_scratch,
        compiler_params=_cparams(("arbitrary",)),
        name="final_norm",
    )(*ins)

    y_prompt = y_p.reshape(bp, sp_, D)
    y_sample = y_s.reshape(bs, ss, D)
    new_ckv = ckv[:tp].reshape(bp, 1, sp_, KV_RANK)
    new_krope = misc[:tp, ROPE_LANE0:ROPE_LANE0 + ROPE_DIM].reshape(bp, 1, sp_, ROPE_DIM)
    new_c = c_p[:, None]
    new_n = n_p[:, None, :, :, 0, :]
    new_m = m_p[:, None, :, :, 0, 0]
    return (y_prompt, y_sample, new_ckv, new_krope, new_c, new_n, new_m)
```

```python
import functools
import math

import jax
import jax.numpy as jnp
from jax import lax
from jax.experimental import pallas as pl
from jax.experimental.pallas import tpu as pltpu

F32 = jnp.float32
BF16 = jnp.bfloat16
I32 = jnp.int32
U32 = jnp.uint32

D = 1024
GRID_W = 64
H_A = 8
Q_RANK = 256
KV_RANK = 128
NOPE = 64
ROPE_DIM = 32
V_A = 64
ROPE_BASE = 10000.0
H_B = 4
DH_B = 128
MB = H_B * DH_B
CONV_K = 31
N_EXPERTS = 32
TOP_K = 4
D_FF = 1024
SWIGLU_LIMIT = 7.0
SWIGLU_ALPHA = 1.702

LANES = 128
SUBLANES = 8
HEAD_PAD = 128
ROPE_LANE0 = 64
N_COND = 16
CTX_ROW = 8

TM = 256
FB = 512
LC = 128
TQ = 256
CH = SUBLANES
SLOTS = TOP_K * TM + N_EXPERTS * CH
TAB = 1024
VMEM_LIMIT = 56 * 1024 * 1024

NEG_INF = float("-inf")
LOG2_E = math.log2(math.e)


def _cparams(sem):
    return pltpu.CompilerParams(dimension_semantics=sem, vmem_limit_bytes=VMEM_LIMIT)


def _sigmoid(x):
    return 1.0 / (1.0 + jnp.exp(-x))


def _rms(x, g, eps=1e-6):
    return x * lax.rsqrt(jnp.mean(x * x, axis=-1, keepdims=True) + eps) * g


def _lane(shape):
    return lax.broadcasted_iota(I32, shape, len(shape) - 1)


def _pack_pairs(x):
    w = x.shape[1] // 2
    return pltpu.bitcast(x[:, :w], U32) | (pltpu.bitcast(x[:, w:], U32) >> 16)


def _unpack_pairs(wd):
    a = pltpu.bitcast(wd & jnp.uint32(0xFFFF0000), F32).astype(BF16)
    b = pltpu.bitcast(wd << 16, F32).astype(BF16)
    return a, b


def _mod_kernel(cond_ref, w_ref, b_ref, o_ref):
    c = cond_ref[...]
    s = (c * _sigmoid(c)).astype(BF16)
    o_ref[...] = jnp.dot(s, w_ref[...].astype(BF16), preferred_element_type=F32) + b_ref[...]


def _modulation(cond, ada_w, ada_b):
    depth = ada_w.shape[0]
    nj = ada_w.shape[2] // D
    out = pl.pallas_call(
        _mod_kernel,
        grid=(depth, nj),
        in_specs=[
            pl.BlockSpec((N_COND, D), lambda l, j: (0, 0)),
            pl.BlockSpec((None, D, D), lambda l, j: (l, 0, j)),
            pl.BlockSpec((None, 1, D), lambda l, j: (l, 0, j)),
        ],
        out_specs=pl.BlockSpec((None, N_COND, D), lambda l, j: (l, 0, j)),
        out_shape=jax.ShapeDtypeStruct((depth, N_COND, nj * D), F32),
        compiler_params=_cparams(("arbitrary", "arbitrary")),
        name="ada_modulation",
    )(cond, ada_w, ada_b.reshape(depth, 1, nj * D))
    return out.reshape(depth, N_COND, nj, D)


def _load_table(tab_hbm, tab_smem, sem_tab, tile):
    cp = pltpu.make_async_copy(tab_hbm.at[pl.ds(tile * TAB, TAB)], tab_smem, sem_tab)
    cp.start()
    cp.wait()


def _move_chunks(tab_smem, copy_for):
    total = jnp.int32(0)
    for e in range(N_EXPERTS):
        nch = tab_smem[e]
        row = tab_smem[N_EXPERTS + e]
        slot = tab_smem[2 * N_EXPERTS + e]

        def body(j, carry, row=row, slot=slot):
            copy_for(pl.multiple_of(row + j * CH, CH), pl.multiple_of(slot + j * CH, CH)).start()
            return carry

        lax.fori_loop(0, nch, body, 0)
        total = total + nch

    def wait_body(j, carry):
        copy_for(0, 0).wait()
        return carry

    lax.fori_loop(0, total, wait_body, 0)


def _gather_expert_rows(tab_hbm, ys_hbm, tab_smem, yc, sem_tab, sem, tile):
    @pl.when(tile == 0)
    def _():
        yc[...] = jnp.zeros_like(yc)

    _load_table(tab_hbm, tab_smem, sem_tab, tile)
    _move_chunks(tab_smem, lambda row, slot: pltpu.make_async_copy(
        ys_hbm.at[pl.ds(row, CH), :], yc.at[pl.ds(slot, CH), :], sem))


def _combine(route, yc):
    lane = _lane((route.shape[0], SLOTS)).astype(F32)
    g = jnp.where(lane == route[:, 8:9], route[:, 4:5], 0.0)
    for k in range(1, TOP_K):
        g = g + jnp.where(lane == route[:, 8 + k:9 + k], route[:, 4 + k:5 + k], 0.0)
    gb = g.astype(BF16)
    ya, yb = _unpack_pairs(yc[...])
    return jnp.concatenate([jnp.dot(gb, ya, preferred_element_type=F32),
                            jnp.dot(gb, yb, preferred_element_type=F32)], axis=-1)


def _moe_route(x1, mod_ref, n2g, wr_ref, br_ref, xc_ref, route_ref, cnt_ref):
    h2 = _rms(x1, n2g) * (1.0 + mod_ref[4:5, :]) + mod_ref[3:4, :]
    h2b = h2.astype(BF16)
    logits = jnp.dot(h2b, wr_ref[...], preferred_element_type=F32) + br_ref[...]
    tm = logits.shape[0]
    lane = _lane((tm, LANES))
    lanef = lane.astype(F32)
    l = jnp.where(lane < N_EXPERTS, logits, NEG_INF)
    tops, idxs, sels = [], [], []
    for _ in range(TOP_K):
        mv = jnp.max(l, axis=-1, keepdims=True)
        idx = jnp.min(jnp.where(l == mv, lanef, float(LANES)), axis=-1, keepdims=True)
        sel = lanef == idx
        l = jnp.where(sel, NEG_INF, l)
        tops.append(mv)
        idxs.append(idx)
        sels.append(sel)
    ex = [jnp.exp(t - tops[0]) for t in tops]
    den = ex[0] + ex[1] + ex[2] + ex[3]
    gates = [e / den for e in ex]

    onehot = jnp.zeros((tm, LANES), F32)
    for sel in sels:
        onehot = onehot + jnp.where(sel, 1.0, 0.0)
    row = lax.broadcasted_iota(I32, (tm, tm), 0)
    col = lax.broadcasted_iota(I32, (tm, tm), 1)
    strict = jnp.where(col < row, 1.0, 0.0).astype(BF16)
    before = jnp.dot(strict, onehot.astype(BF16), preferred_element_type=F32)
    n_e = jnp.sum(onehot, axis=0, keepdims=True)
    cnt_ref[...] = jnp.broadcast_to(n_e, cnt_ref.shape)
    nch = jnp.right_shift(n_e.astype(I32) + (CH - 1), CH.bit_length() - 1).astype(F32)
    r128 = lax.broadcasted_iota(I32, (LANES, LANES), 0)
    c128 = lax.broadcasted_iota(I32, (LANES, LANES), 1)
    upper = jnp.where(r128 < c128, 1.0, 0.0).astype(BF16)
    seg0 = float(CH) * jnp.dot(jnp.broadcast_to(nch, (SUBLANES, LANES)).astype(BF16), upper,
                               preferred_element_type=F32)[0:1, :]
    slot_of = seg0 + before
    slots = [jnp.sum(jnp.where(sel, slot_of, 0.0), axis=-1, keepdims=True) for sel in sels]

    r = jnp.zeros((tm, LANES), F32)
    for j, colv in enumerate(idxs + gates + slots):
        r = jnp.where(lane == j, colv, r)
    route_ref[...] = r

    lane_s = _lane((tm, SLOTS)).astype(F32)
    pt = jnp.where(lane_s == slots[0], 1.0, 0.0)
    for k in range(1, TOP_K):
        pt = pt + jnp.where(lane_s == slots[k], 1.0, 0.0)
    xc = jnp.dot(pt.T.astype(BF16), h2b, preferred_element_type=F32)
    xc_ref[...] = _pack_pairs(xc)


def _values_with_ones(ckvb, wv_ref):
    vv = jnp.dot(ckvb, wv_ref[...], preferred_element_type=F32)
    return jnp.where((_lane(vv.shape) & (HEAD_PAD - 1)) == V_A, 1.0, vv).astype(BF16)


def _rope(x, tab_ref):
    return x * tab_ref[0] + pltpu.roll(x, LANES - 8, 1) * tab_ref[1] + pltpu.roll(x, 8, 1) * tab_ref[2]


def _ab_in_kernel(npt, xp_ref, xs_ref, mod_ref, n1g_ref, win_ref, qg_ref, wq_ref, kvg_ref, wk_ref, wv_ref, gb_ref,
                  tab_ref, q_ref, k_ref, v_ref, ckv_ref, misc_ref, mqkv_ref, mo_ref):
    x = jnp.where(pl.program_id(0) < npt, xp_ref[...], xs_ref[...])
    h = _rms(x, n1g_ref[...]) * (1.0 + mod_ref[1:2, :]) + mod_ref[0:1, :]
    z = jnp.dot(h.astype(BF16), win_ref[...], preferred_element_type=F32)

    qn = _rms(z[:, :Q_RANK], qg_ref[...]).astype(BF16)
    q = jnp.dot(qn, wq_ref[...], preferred_element_type=F32)
    ckv = _rms(z[:, Q_RANK:Q_RANK + KV_RANK], kvg_ref[...])
    ckv_ref[...] = ckv
    ckvb = ckv.astype(BF16)
    kn = jnp.dot(ckvb, wk_ref[...], preferred_element_type=F32)
    v_ref[...] = _values_with_ones(ckvb, wv_ref)

    misc = z[:, Q_RANK + KV_RANK:Q_RANK + KV_RANK + LANES]
    misc_ref[...] = misc + gb_ref[...]
    lane = _lane(misc.shape)
    kr = jnp.where((lane >= ROPE_LANE0) & (lane < ROPE_LANE0 + ROPE_DIM), _rope(misc, tab_ref), 0.0)
    scale = float((NOPE + ROPE_DIM) ** -0.5 * LOG2_E)
    for hh in range(H_A):
        sl = slice(hh * HEAD_PAD, (hh + 1) * HEAD_PAD)
        q_ref[:, sl] = (_rope(q[:, sl], tab_ref) * scale).astype(BF16)
        k_ref[:, sl] = (kn[:, sl] + kr).astype(BF16)

    m0 = Q_RANK + KV_RANK + LANES
    mqkv_ref[:, :MB] = z[:, m0:m0 + MB].astype(BF16)
    mqkv_ref[:, MB:2 * MB] = (z[:, m0 + MB:m0 + 2 * MB] * float(DH_B ** -0.5)).astype(BF16)
    mqkv_ref[:, 2 * MB:] = z[:, m0 + 2 * MB:m0 + 3 * MB].astype(BF16)
    mo_ref[...] = z[:, m0 + 3 * MB:m0 + 4 * MB]


def _ctx_kv_kernel(ckv_ref, kr_ref, wk_ref, wv_ref, k_ref, v_ref):
    ckvb = ckv_ref[...].astype(BF16)
    kn = jnp.dot(ckvb, wk_ref[...], preferred_element_type=F32)
    v_ref[...] = _values_with_ones(ckvb, wv_ref)
    kr = kr_ref[...]
    for hh in range(H_A):
        sl = slice(hh * HEAD_PAD, (hh + 1) * HEAD_PAD)
        k_ref[:, sl] = (kn[:, sl] + kr).astype(BF16)


def _attn_kernel(q_ref, k_ref, v_ref, prev_ref, o_ref):
    del prev_ref
    outs = []
    for j in range(2):
        sl = slice(j * HEAD_PAD, (j + 1) * HEAD_PAD)
        s = lax.dot_general(q_ref[:, sl], k_ref[:, sl], (((1,), (1,)), ((), ())), preferred_element_type=F32)
        m = jnp.max(s, axis=-1, keepdims=True)
        p = jnp.exp2(s - m).astype(BF16)
        o = jnp.dot(p, v_ref[:, sl], preferred_element_type=F32)
        outs.append(o / o[:, V_A:V_A + 1])
    lane = _lane(outs[0].shape)
    o_ref[...] = jnp.where(lane < V_A, outs[0], pltpu.roll(outs[1], V_A, 1)).astype(BF16)


def _attention(q, k, v, qrow0, nb, s, l, prev):
    t = q.shape[0]
    tq = min(TQ, s)
    nq = s // tq
    qrow = lambda bi, hp, qi: (qrow0 // tq + bi * nq + qi, hp)
    krow = lambda bi, hp, qi: (bi, hp)
    ins = [q, k, v, prev]
    in_specs = [pl.BlockSpec((tq, 2 * HEAD_PAD), qrow), pl.BlockSpec((l, 2 * HEAD_PAD), krow),
                pl.BlockSpec((l, 2 * HEAD_PAD), krow), pl.BlockSpec(memory_space=pl.ANY)]
    aliases = {3: 0}
    return pl.pallas_call(
        _attn_kernel,
        grid=(nb, H_A // 2, nq),
        in_specs=in_specs,
        out_specs=pl.BlockSpec((tq, 2 * V_A), qrow),
        out_shape=jax.ShapeDtypeStruct((t, H_A * V_A), BF16),
        input_output_aliases=aliases,
        compiler_params=_cparams(("arbitrary", "arbitrary", "arbitrary")),
        name="mla_attention",
    )(*ins)


def _log_sigmoid(x):
    return jnp.minimum(x, 0.0) - jnp.log(1.0 + jnp.exp(-jnp.abs(x)))


def _mlstm_direction(d, q_ref, k_ref, v_ref, g_ref, h_ref, c_s, n_s, m_s):
    gates = g_ref[...]
    lf_all = _log_sigmoid(gates)
    row = lax.broadcasted_iota(I32, (LC, LC), 0)
    col = lax.broadcasted_iota(I32, (LC, LC), 1)
    mask = (col <= row) if d == 0 else (col >= row)
    eye = row == col
    cum = jnp.dot(jnp.where(mask, 1.0, 0.0), lf_all, preferred_element_type=F32,
                  precision=lax.Precision.HIGHEST)

    for hh in range(H_B):
        sl = slice(hh * DH_B, (hh + 1) * DH_B)
        q = q_ref[:, sl]
        k = k_ref[:, sl]
        v = v_ref[:, sl]
        b_col = cum[:, 8 * d + 4 + hh:8 * d + 5 + hh]
        i_col = gates[:, 8 * d + hh:8 * d + hh + 1]
        m_prev = m_s[d, hh][:, 0:1]
        r_row = jnp.sum(jnp.where(eye, i_col - b_col, 0.0), axis=0, keepdims=True)
        dm = jnp.where(mask, b_col + r_row, NEG_INF)
        a_col = b_col + m_prev
        mt = jnp.maximum(a_col, jnp.max(dm, axis=-1, keepdims=True))
        w_inter = jnp.exp(a_col - mt)
        qk = lax.dot_general(q, k, (((1,), (1,)), ((), ())), preferred_element_type=F32)
        s = qk * jnp.exp(dm - mt)
        c_prev = c_s[d, hh]
        n_prev = n_s[d, hh]
        num = w_inter * jnp.dot(q, c_prev.astype(BF16), preferred_element_type=F32) \
            + jnp.dot(s.astype(BF16), v, preferred_element_type=F32)
        qf = q.astype(F32)
        den = w_inter * jnp.sum(qf * n_prev, axis=-1, keepdims=True) + jnp.sum(s, axis=-1, keepdims=True)
        h_ref[:, sl] = num / jnp.maximum(jnp.abs(den), jnp.exp(-mt))

        b_tot = b_col[LC - 1:LC, :] if d == 0 else b_col[0:1, :]
        g_col = b_tot - b_col + i_col
        m_new = jnp.maximum(b_tot + m_prev, jnp.max(g_col, axis=0, keepdims=True))
        w_c = jnp.exp(b_tot + m_prev - m_new)
        kw = k.astype(F32) * jnp.exp(g_col - m_new)
        c_s[d, hh] = w_c * c_prev + jnp.dot(kw.T.astype(BF16), v, preferred_element_type=F32)
        n_s[d, hh] = w_c * n_prev + jnp.sum(kw, axis=0, keepdims=True)
        m_s[d, hh] = jnp.broadcast_to(m_new, (1, LANES))


def _mlstm_kernel(qf_ref, kf_ref, vf_ref, gf_ref, qb_ref, kb_ref, vb_ref, gb_ref, c0_ref, n0_ref, m0_ref,
                  prevf_ref, prevb_ref, hf_ref, hb_ref, c_out, n_out, m_out, c_s, n_s, m_s):
    del prevf_ref, prevb_ref
    c = pl.program_id(1)

    @pl.when(c == 0)
    def _():
        c_s[...] = c0_ref[...]
        n_s[...] = n0_ref[...]
        m_s[...] = m0_ref[...]

    _mlstm_direction(0, qf_ref, kf_ref, vf_ref, gf_ref, hf_ref, c_s, n_s, m_s)
    _mlstm_direction(1, qb_ref, kb_ref, vb_ref, gb_ref, hb_ref, c_s, n_s, m_s)

    @pl.when(c == pl.num_programs(1) - 1)
    def _():
        c_out[...] = c_s[...]
        n_out[...] = n_s[...]
        m_out[...] = m_s[...]


def _mlstm(mqkv, misc, c0, n0, m0, row0, nb, s, prev):
    nc = s // LC
    blk0 = row0 // LC
    t = mqkv.shape[0]
    fwd = lambda bi, ci: blk0 + bi * nc + ci
    bwd = lambda bi, ci: blk0 + bi * nc + nc - 1 - ci
    chunk_specs = lambda rows: [
        pl.BlockSpec((LC, MB), lambda bi, ci: (rows(bi, ci), 0)),
        pl.BlockSpec((LC, MB), lambda bi, ci: (rows(bi, ci), 1)),
        pl.BlockSpec((LC, MB), lambda bi, ci: (rows(bi, ci), 2)),
        pl.BlockSpec((LC, LANES), lambda bi, ci: (rows(bi, ci), 0)),
    ]
    state_spec = lambda shape: pl.BlockSpec((None,) + shape, lambda bi, ci: (bi,) + (0,) * len(shape))
    state_specs = [state_spec((2, H_B, DH_B, DH_B)), state_spec((2, H_B, 1, DH_B)), state_spec((2, H_B, 1, LANES))]
    any_spec = pl.BlockSpec(memory_space=pl.ANY)
    return pl.pallas_call(
        _mlstm_kernel,
        grid=(nb, nc),
        in_specs=chunk_specs(fwd) + chunk_specs(bwd) + state_specs + [any_spec, any_spec],
        out_specs=[
            pl.BlockSpec((LC, MB), lambda bi, ci: (fwd(bi, ci), 0)),
            pl.BlockSpec((LC, MB), lambda bi, ci: (bwd(bi, ci), 0)),
        ] + state_specs,
        out_shape=[
            jax.ShapeDtypeStruct((t, MB), F32),
            jax.ShapeDtypeStruct((t, MB), F32),
            jax.ShapeDtypeStruct((nb, 2, H_B, DH_B, DH_B), F32),
            jax.ShapeDtypeStruct((nb, 2, H_B, 1, DH_B), F32),
            jax.ShapeDtypeStruct((nb, 2, H_B, 1, LANES), F32),
        ],
        scratch_shapes=[
            pltpu.VMEM((2, H_B, DH_B, DH_B), F32),
            pltpu.VMEM((2, H_B, 1, DH_B), F32),
            pltpu.VMEM((2, H_B, 1, LANES), F32),
        ],
        input_output_aliases={11: 0, 12: 1},
        compiler_params=_cparams(("arbitrary", "arbitrary")),
        name="mlstm_scan",
    )(mqkv, mqkv, mqkv, misc, mqkv, mqkv, mqkv, misc, c0, n0, m0, prev[0], prev[1])


def _ab_out_kernel(npt, oa_ref, hf_ref, hb_ref, mo_ref, mg_ref, wout_ref, xp_ref, xs_ref, mod_ref, n2g_ref, wr_ref,
                   br_ref, x1_ref, xc_ref, route_ref, cnt_ref):
    hsum = hf_ref[...] + hb_ref[...]
    parts = []
    for hh in range(H_B):
        hh_ = hsum[:, hh * DH_B:(hh + 1) * DH_B]
        mu = jnp.mean(hh_, axis=-1, keepdims=True)
        var = jnp.mean(jnp.square(hh_ - mu), axis=-1, keepdims=True)
        parts.append((hh_ - mu) * lax.rsqrt(var + 1e-5))
    hn = jnp.concatenate(parts, axis=-1) * mg_ref[...]
    ob = (hn * _sigmoid(mo_ref[...])).astype(BF16)
    o = jnp.dot(oa_ref[...], wout_ref[:H_A * V_A, :], preferred_element_type=F32) \
        + jnp.dot(ob, wout_ref[H_A * V_A:, :], preferred_element_type=F32)
    x = jnp.where(pl.program_id(0) < npt, xp_ref[...], xs_ref[...])
    x1 = x + mod_ref[2:3, :] * o
    x1_ref[...] = x1
    _moe_route(x1, mod_ref, n2g_ref[...], wr_ref, br_ref, xc_ref, route_ref, cnt_ref)


def _conv_in_kernel(tab_hbm, ys_hbm, x1_ref, route_ref, modp_ref, mod_ref, n1g_ref, w_ref,
                    x2_ref, u_ref, tab_smem, yc, sem_tab, sem):
    _gather_expert_rows(tab_hbm, ys_hbm, tab_smem, yc, sem_tab, sem, pl.program_id(0))
    x2 = x1_ref[...] + modp_ref[5:6, :] * _combine(route_ref[...], yc)
    x2_ref[...] = x2
    h = _rms(x2, n1g_ref[...]) * (1.0 + mod_ref[1:2, :]) + mod_ref[0:1, :]
    z = jnp.dot(h.astype(BF16), w_ref[...], preferred_element_type=F32)
    u_ref[...] = z[:, :D] * _sigmoid(z[:, D:])


HALO = 16


def _conv_out_kernel(npt, tiles_per_seq, u_ref, up_ref, un_ref, wdw_ref, bdw_ref, lg_ref, lb_ref, w2_ref,
                     x_ref, mod_ref, n2g_ref, wr_ref, br_ref, x1_ref, xc_ref, route_ref, cnt_ref, ext_ref, sh_ref):
    i = pl.program_id(0)
    j = i - npt
    in_prompt = i < npt
    first = in_prompt | (j % tiles_per_seq == 0)
    last = in_prompt | (j % tiles_per_seq == tiles_per_seq - 1)
    ext_ref[0:HALO, :] = jnp.where(first, 0.0, up_ref[...])
    ext_ref[HALO:HALO + TM, :] = u_ref[...]
    ext_ref[HALO + TM:, :] = jnp.where(last, 0.0, un_ref[...])

    for b in range(SUBLANES):
        sh_ref[b] = ext_ref[b + 1:b + 1 + TM + 24, :]
    acc = jnp.zeros((TM, D), F32) + bdw_ref[...]
    for tap in range(CONV_K):
        a, b = divmod(tap, SUBLANES)
        acc = acc + wdw_ref[tap:tap + 1, :] * sh_ref[b, SUBLANES * a:SUBLANES * a + TM, :]
    mu = jnp.mean(acc, axis=-1, keepdims=True)
    var = jnp.mean(jnp.square(acc - mu), axis=-1, keepdims=True)
    ln = (acc - mu) * lax.rsqrt(var + 1e-5) * lg_ref[...] + lb_ref[...]
    act = (ln * _sigmoid(ln)).astype(BF16)
    o = jnp.dot(act, w2_ref[...], preferred_element_type=F32)
    x1 = x_ref[...] + mod_ref[2:3, :] * o
    x1_ref[...] = x1
    _moe_route(x1, mod_ref, n2g_ref[...], wr_ref, br_ref, xc_ref, route_ref, cnt_ref)


def _dispatch_kernel(tab_hbm, xc_ref, xs_in, xs_hbm, tab_smem, sem_tab, sem):
    del xs_in
    _load_table(tab_hbm, tab_smem, sem_tab, pl.program_id(0))
    _move_chunks(tab_smem, lambda row, slot: pltpu.make_async_copy(
        xc_ref.at[pl.ds(slot, CH), :], xs_hbm.at[pl.ds(row, CH), :], sem))


def _dispatch(tab, xc, n_rows):
    nt = xc.shape[0] // SLOTS
    any_spec = pl.BlockSpec(memory_space=pl.ANY)
    return pl.pallas_call(
        _dispatch_kernel,
        grid=(nt,),
        in_specs=[any_spec, pl.BlockSpec((SLOTS, D // 2), lambda i: (i, 0)), any_spec],
        out_specs=any_spec,
        out_shape=jax.ShapeDtypeStruct((n_rows, D // 2), U32),
        scratch_shapes=[pltpu.SMEM((TAB,), I32), pltpu.SemaphoreType.DMA, pltpu.SemaphoreType.DMA],
        input_output_aliases={2: 0},
        compiler_params=_cparams(("arbitrary",)),
        name="moe_dispatch",
    )(tab, xc, jnp.zeros((n_rows, D // 2), U32))


def _ffn_kernel(be_ref, nused_ref, xs_ref, win_ref, bin_ref, wout_ref, bout_ref, y_ref, win_s, wout_s):
    i = pl.program_id(0)
    e = be_ref[i]
    e_prev = be_ref[jnp.maximum(i - 1, 0)]

    @pl.when((i == 0) | (e != e_prev))
    def _():
        win_s[...] = win_ref[...].astype(BF16)
        wout_s[...] = wout_ref[...].astype(BF16)

    @pl.when(i < nused_ref[0])
    def _():
        xa, xb = _unpack_pairs(xs_ref[...])
        hb = jnp.dot(xa, win_s[:D // 2, :], preferred_element_type=F32) \
            + jnp.dot(xb, win_s[D // 2:, :], preferred_element_type=F32) + bin_ref[...]
        g = jnp.minimum(hb[:, :D_FF], SWIGLU_LIMIT)
        u = jnp.clip(hb[:, D_FF:], -SWIGLU_LIMIT, SWIGLU_LIMIT)
        act = g * _sigmoid(SWIGLU_ALPHA * g) * (u + 1.0)
        y = jnp.dot(act.astype(BF16), wout_s[...], preferred_element_type=F32) + bout_ref[...]
        y_ref[...] = _pack_pairs(y.astype(BF16).astype(F32))

    @pl.when(i >= nused_ref[0])
    def _():
        y_ref[...] = jnp.zeros_like(y_ref)


def _expert_ffn(block_e, n_used, xs, layer, w_in, b_in, w_out, b_out):
    n_blocks = xs.shape[0] // FB
    depth = w_in.shape[0]
    return pl.pallas_call(
        _ffn_kernel,
        grid_spec=pltpu.PrefetchScalarGridSpec(
            num_scalar_prefetch=2,
            grid=(n_blocks,),
            in_specs=[
                pl.BlockSpec((FB, D // 2), lambda i, be, nu: (i, 0)),
                pl.BlockSpec((None, None, D, 2 * D_FF), lambda i, be, nu: (layer, be[i], 0, 0)),
                pl.BlockSpec((None, None, 1, 2 * D_FF), lambda i, be, nu: (layer, be[i], 0, 0)),
                pl.BlockSpec((None, None, D_FF, D), lambda i, be, nu: (layer, be[i], 0, 0)),
                pl.BlockSpec((None, None, 1, D), lambda i, be, nu: (layer, be[i], 0, 0)),
            ],
            out_specs=pl.BlockSpec((FB, D // 2), lambda i, be, nu: (i, 0)),
            scratch_shapes=[pltpu.VMEM((D, 2 * D_FF), BF16), pltpu.VMEM((D_FF, D), BF16)],
        ),
        out_shape=jax.ShapeDtypeStruct((n_blocks * FB, D // 2), U32),
        compiler_params=_cparams(("arbitrary",)),
        name="moe_expert_ffn",
    )(block_e, n_used, xs, w_in, b_in.reshape(depth, N_EXPERTS, 1, 2 * D_FF), w_out,
      b_out.reshape(depth, N_EXPERTS, 1, D))


def _moe_plan(cnt):
    nt = cnt.shape[0]
    n_blocks = (nt * TM * TOP_K + nt * N_EXPERTS * (CH - 1)) // FB + N_EXPERTS + 1
    n = cnt[:, 0, :N_EXPERTS].astype(I32)
    nch = (n + CH - 1) // CH
    padded = (CH * jnp.sum(nch, axis=0) + FB - 1) // FB * FB
    pad_ends = jnp.cumsum(padded)
    pad_starts = pad_ends - padded
    first_row = pad_starts[None, :] + CH * (jnp.cumsum(nch, axis=0) - nch)
    first_slot = CH * (jnp.cumsum(nch, axis=1) - nch)
    tab = jnp.concatenate([nch, first_row, first_slot, jnp.zeros((nt, TAB - 3 * N_EXPERTS), I32)], axis=1)
    blk = jnp.arange(n_blocks, dtype=I32) * FB
    block_e = jnp.minimum(jnp.sum((blk[:, None] >= pad_ends[None, :]).astype(I32), axis=-1), N_EXPERTS - 1)
    n_used = (pad_ends[-1] // FB).reshape(1)
    return tab.reshape(nt * TAB), block_e.astype(I32), n_used.astype(I32), n_blocks * FB


def _moe(cnt, xc, layer, w_in, b_in, w_out, b_out):
    tab, block_e, n_used, n_rows = _moe_plan(cnt)
    xs = _dispatch(tab, xc, n_rows)
    ys = _expert_ffn(block_e, n_used, xs, layer, w_in, b_in, w_out, b_out)
    return tab, ys


def _final_kernel(npt, tab_hbm, ys_hbm, x1_ref, route_ref, modp_ref, g_ref, op_ref, os_ref, tab_smem, yc, sem_tab, sem):
    i = pl.program_id(0)
    _gather_expert_rows(tab_hbm, ys_hbm, tab_smem, yc, sem_tab, sem, i)
    x2 = x1_ref[...] + modp_ref[5:6, :] * _combine(route_ref[...], yc)
    out = _rms(x2, g_ref[...])

    @pl.when(i < npt)
    def _():
        op_ref[...] = out

    @pl.when(i >= npt)
    def _():
        os_ref[...] = out


def _rope_tables(n_tokens):
    rows = n_tokens // GRID_W
    pos_row = jnp.repeat(jnp.arange(rows, dtype=F32), GRID_W)
    pos_col = jnp.tile(jnp.arange(GRID_W, dtype=F32), rows)
    n_freq = ROPE_DIM // 4
    inv_freq = ROPE_BASE ** (-jnp.arange(n_freq, dtype=F32) / n_freq)
    ang = jnp.stack([pos_row[:, None] * inv_freq, pos_col[:, None] * inv_freq], axis=1)
    cos, sin = jnp.cos(ang), jnp.sin(ang)
    zero = jnp.zeros_like(sin)
    c = jnp.stack([cos, cos], axis=2).reshape(n_tokens, ROPE_DIM)
    sm = jnp.stack([-sin, zero], axis=2).reshape(n_tokens, ROPE_DIM)
    sp = jnp.stack([zero, sin], axis=2).reshape(n_tokens, ROPE_DIM)

    def place(a, fill):
        left = jnp.full((n_tokens, ROPE_LANE0), fill, F32)
        right = jnp.full((n_tokens, LANES - ROPE_LANE0 - ROPE_DIM), fill, F32)
        body = jnp.concatenate([left, a, right], axis=1)
        ident = jnp.full((TM, LANES), fill, F32)
        return jnp.concatenate([body, ident], axis=0)

    return jnp.stack([place(c, 1.0), place(sm, 0.0), place(sp, 0.0)], axis=0)


def kernel(x_prompt, x_sample, c, cache_mla_ckv, cache_mla_krope, state_mlstm_C, state_mlstm_n, state_mlstm_m, c_ctx, ada_w, ada_b, norm1_g, norm2_g, ab_w_in, mla_q_norm_g, mla_w_uq, mla_kv_norm_g, mla_w_ukv, mlstm_gate_b, mlstm_norm_g, ab_w_out, conv_w_pw1, conv_w_dw, conv_b_dw, conv_ln_g, conv_ln_b, conv_w_pw2, router_w, router_b, moe_w_in, moe_b_in, moe_w_out, moe_b_out, final_norm_g):
    bp, sp_, _ = x_prompt.shape
    bs, ss, _ = x_sample.shape
    past = cache_mla_ckv.shape[2]
    tp, ts = bp * sp_, bs * ss
    t = tp + ts
    nt = t // TM
    npt = tp // TM
    tps = ss // TM
    assert sp_ == TM and ss % TM == 0 and bs <= CTX_ROW and tp % ss == 0

    xp2 = x_prompt.reshape(tp, D)
    xs2 = x_sample.reshape(ts, D)
    cond = jnp.zeros((N_COND, D), F32).at[:bs].set(c).at[CTX_ROW].set(c_ctx)
    mod = _modulation(cond, ada_w, ada_b)

    def mod_row(i):
        return jnp.where(i < npt, CTX_ROW, (i - npt) // tps)

    def mod_spec(layer):
        return pl.BlockSpec((None, None, 6, D), lambda i: (layer, mod_row(i), 0, 0))

    row_spec = lambda w: pl.BlockSpec((TM, w), lambda i: (i, 0))
    prompt_spec = pl.BlockSpec((TM, D), lambda i: (jnp.minimum(i, npt - 1), 0))
    sample_spec = pl.BlockSpec((TM, D), lambda i: (jnp.maximum(i - npt, 0), 0))
    full = lambda a: pl.BlockSpec(a.shape, lambda i: (0,) * a.ndim)
    vec = lambda a: a.reshape(1, -1)
    any_spec = pl.BlockSpec(memory_space=pl.ANY)

    def router_args(layer):
        wr = jnp.pad(router_w[layer], ((0, 0), (0, LANES - N_EXPERTS))).astype(BF16)
        br = jnp.pad(router_b[layer], (0, LANES - N_EXPERTS)).reshape(1, LANES)
        return wr, br

    moe_out_shapes = [
        jax.ShapeDtypeStruct((t, D), F32),
        jax.ShapeDtypeStruct((nt * SLOTS, D // 2), U32),
        jax.ShapeDtypeStruct((t, LANES), F32),
        jax.ShapeDtypeStruct((nt, SUBLANES, LANES), F32),
    ]
    moe_out_specs = [row_spec(D), pl.BlockSpec((SLOTS, D // 2), lambda i: (i, 0)), row_spec(LANES),
                     pl.BlockSpec((None, SUBLANES, LANES), lambda i: (i, 0, 0))]

    j = 0
    w = ab_w_in[j]
    zc = lambda n: jnp.zeros((D, n), F32)
    misc_w = jnp.concatenate([w[:, 2464:2480], zc(ROPE_LANE0 - 16), w[:, 384:416], zc(LANES - ROPE_LANE0 - ROPE_DIM)], axis=1)
    w_in = jnp.concatenate([w[:, :384], misc_w, w[:, 416:2464]], axis=1).astype(BF16)
    wq = jnp.pad(mla_w_uq[j].reshape(Q_RANK, H_A, NOPE + ROPE_DIM), ((0, 0), (0, 0), (0, HEAD_PAD - NOPE - ROPE_DIM)))
    wq = wq.reshape(Q_RANK, H_A * HEAD_PAD).astype(BF16)
    wkv = mla_w_ukv[j].reshape(KV_RANK, H_A, NOPE + V_A)
    wk = jnp.pad(wkv[:, :, :NOPE], ((0, 0), (0, 0), (0, HEAD_PAD - NOPE))).reshape(KV_RANK, H_A * HEAD_PAD).astype(BF16)
    wv = jnp.pad(wkv[:, :, NOPE:], ((0, 0), (0, 0), (0, HEAD_PAD - V_A))).reshape(KV_RANK, H_A * HEAD_PAD).astype(BF16)
    gate_b = jnp.pad(mlstm_gate_b[j], (0, LANES - 4 * H_B)).reshape(1, LANES)
    tabs = _rope_tables(ss)
    pos_blocks = ss // TM

    def tab_block(i):
        return jnp.where(i < npt, pos_blocks, (i - npt) % tps)

    ins = [xp2, xs2, mod, vec(norm1_g[0]), w_in, vec(mla_q_norm_g[j]), wq, vec(mla_kv_norm_g[j]), wk, wv, gate_b, tabs]
    in_specs = [prompt_spec, sample_spec, mod_spec(0)] + [full(a) for a in ins[3:11]] + [
        pl.BlockSpec((3, TM, LANES), lambda i: (0, tab_block(i), 0))]
    q, k, v, ckv, misc, mqkv, mo = pl.pallas_call(
        functools.partial(_ab_in_kernel, npt),
        grid=(nt,),
        in_specs=in_specs,
        out_specs=[row_spec(H_A * HEAD_PAD), row_spec(H_A * HEAD_PAD), row_spec(H_A * HEAD_PAD), row_spec(KV_RANK),
                   row_spec(LANES), row_spec(3 * MB), row_spec(MB)],
        out_shape=[
            jax.ShapeDtypeStruct((t, H_A * HEAD_PAD), BF16), jax.ShapeDtypeStruct((t, H_A * HEAD_PAD), BF16),
            jax.ShapeDtypeStruct((t, H_A * HEAD_PAD), BF16), jax.ShapeDtypeStruct((t, KV_RANK), F32),
            jax.ShapeDtypeStruct((t, LANES), F32), jax.ShapeDtypeStruct((t, 3 * MB), BF16),
            jax.ShapeDtypeStruct((t, MB), F32)],
        compiler_params=_cparams(("arbitrary",)),
        name="ab_in_proj",
    )(*ins)

    nctx = bs * past
    ctx_ckv = cache_mla_ckv[:, j].reshape(nctx, KV_RANK)
    ctx_kr = jnp.pad(cache_mla_krope[:, j].reshape(nctx, ROPE_DIM), ((0, 0), (ROPE_LANE0, LANES - ROPE_LANE0 - ROPE_DIM)))
    k_ctx, v_ctx = pl.pallas_call(
        _ctx_kv_kernel,
        grid=(nctx // TM,),
        in_specs=[row_spec(KV_RANK), row_spec(LANES), full(wk), full(wv)],
        out_specs=[row_spec(H_A * HEAD_PAD), row_spec(H_A * HEAD_PAD)],
        out_shape=[jax.ShapeDtypeStruct((nctx, H_A * HEAD_PAD), BF16)] * 2,
        compiler_params=_cparams(("arbitrary",)),
        name="ctx_kv",
    )(ctx_ckv, ctx_kr, wk, wv)

    o_a = _attention(q, k, v, 0, bp, sp_, sp_, jnp.zeros((t, H_A * V_A), BF16))
    seq_keys = lambda ctx, own: jnp.concatenate(
        [ctx.reshape(bs, past, -1), own[tp:].reshape(bs, ss, -1)], axis=1).reshape(bs * (past + ss), -1)
    o_a = _attention(q, seq_keys(k_ctx, k), seq_keys(v_ctx, v), tp, bs, ss, past + ss, o_a)

    zc0 = jnp.zeros((bp, 2, H_B, DH_B, DH_B), F32)
    zn0 = jnp.zeros((bp, 2, H_B, 1, DH_B), F32)
    zm0 = jnp.zeros((bp, 2, H_B, 1, LANES), F32)
    h_f, h_b, c_p, n_p, m_p = _mlstm(mqkv, misc, zc0, zn0, zm0, 0, bp, sp_,
                                     (jnp.zeros((t, MB), F32), jnp.zeros((t, MB), F32)))
    sm0 = jnp.broadcast_to(state_mlstm_m[:, j][..., None, None], (bs, 2, H_B, 1, LANES))
    h_f, h_b, _, _, _ = _mlstm(mqkv, misc, state_mlstm_C[:, j], state_mlstm_n[:, j][:, :, :, None, :], sm0, tp, bs, ss,
                               (h_f, h_b))

    wr, br = router_args(0)
    ins = [o_a, h_f, h_b, mo, vec(mlstm_norm_g[j]), ab_w_out[j].astype(BF16), xp2, xs2, mod, vec(norm2_g[0]), wr, br]
    in_specs = [row_spec(H_A * V_A), row_spec(MB), row_spec(MB), row_spec(MB), full(ins[4]), full(ins[5]),
                prompt_spec, sample_spec, mod_spec(0), full(ins[9]), full(wr), full(br)]
    x1, xc, route, cnt = pl.pallas_call(
        functools.partial(_ab_out_kernel, npt),
        grid=(nt,),
        in_specs=in_specs,
        out_specs=moe_out_specs,
        out_shape=moe_out_shapes,
        compiler_params=_cparams(("arbitrary",)),
        name="ab_out_proj",
    )(*ins)
    tab, ys = _moe(cnt, xc, 0, moe_w_in, moe_b_in, moe_w_out, moe_b_out)

    gather_scratch = [pltpu.SMEM((TAB,), I32), pltpu.VMEM((SLOTS, D // 2), U32),
                      pltpu.SemaphoreType.DMA, pltpu.SemaphoreType.DMA]
    ins = [tab, ys, x1, route, mod, mod, vec(norm1_g[1]), conv_w_pw1[0].astype(BF16)]
    x2, u = pl.pallas_call(
        _conv_in_kernel,
        grid=(nt,),
        in_specs=[any_spec, any_spec, row_spec(D), row_spec(LANES), mod_spec(0), mod_spec(1), full(ins[6]), full(ins[7])],
        out_specs=[row_spec(D), row_spec(D)],
        out_shape=[jax.ShapeDtypeStruct((t, D), F32), jax.ShapeDtypeStruct((t, D), F32)],
        scratch_shapes=gather_scratch,
        compiler_params=_cparams(("arbitrary",)),
        name="conv_in_proj",
    )(*ins)

    hpt = TM // HALO
    nhalo = t // HALO
    wr, br = router_args(1)
    ins = [u, u, u, conv_w_dw[0], vec(conv_b_dw[0]), vec(conv_ln_g[0]), vec(conv_ln_b[0]), conv_w_pw2[0].astype(BF16),
           x2, mod, vec(norm2_g[1]), wr, br]
    in_specs = [row_spec(D),
                pl.BlockSpec((HALO, D), lambda i: (jnp.maximum(i * hpt - 1, 0), 0)),
                pl.BlockSpec((HALO, D), lambda i: (jnp.minimum((i + 1) * hpt, nhalo - 1), 0)),
                full(ins[3]), full(ins[4]), full(ins[5]), full(ins[6]), full(ins[7]), row_spec(D), mod_spec(1),
                full(ins[10]), full(wr), full(br)]
    x1, xc, route, cnt = pl.pallas_call(
        functools.partial(_conv_out_kernel, npt, tps),
        grid=(nt,),
        in_specs=in_specs,
        out_specs=moe_out_specs,
        out_shape=moe_out_shapes,
        scratch_shapes=[pltpu.VMEM((TM + 2 * HALO, D), F32), pltpu.VMEM((SUBLANES, TM + 24, D), F32)],
        compiler_params=_cparams(("arbitrary",)),
        name="conv_out_proj",
    )(*ins)
    tab, ys = _moe(cnt, xc, 1, moe_w_in, moe_b_in, moe_w_out, moe_b_out)

    ins = [tab, ys, x1, route, mod, vec(final_norm_g)]
    y_p, y_s = pl.pallas_call(
        functools.partial(_final_kernel, npt),
        grid=(nt,),
        in_specs=[any_spec, any_spec, row_spec(D), row_spec(LANES), mod_spec(1), full(ins[5])],
        out_specs=[prompt_spec, sample_spec],
        out_shape=[jax.ShapeDtypeStruct((tp, D), F32), jax.ShapeDtypeStruct((ts, D), F32)],
        scratch_shapes=gather_scratch,
        compiler_params=_cparams(("arbitrary",)),
        name="final_norm",
    )(*ins)

    y_prompt = y_p.reshape(bp, sp_, D)
    y_sample = y_s.reshape(bs, ss, D)
    new_ckv = ckv[:tp].reshape(bp, 1, sp_, KV_RANK)
    new_krope = misc[:tp, ROPE_LANE0:ROPE_LANE0 + ROPE_DIM].reshape(bp, 1, sp_, ROPE_DIM)
    new_c = c_p[:, None]
    new_n = n_p[:, None, :, :, 0, :]
    new_m = m_p[:, None, :, :, 0, 0]
    return (y_prompt, y_sample, new_ckv, new_krope, new_c, new_n, new_m)
```

```python
import functools
import math

import jax
import jax.numpy as jnp
from jax import lax
from jax.experimental import pallas as pl
from jax.experimental.pallas import tpu as pltpu

F32 = jnp.float32
BF16 = jnp.bfloat16
I32 = jnp.int32
U32 = jnp.uint32

D = 1024
GRID_W = 64
H_A = 8
Q_RANK = 256
KV_RANK = 128
NOPE = 64
ROPE_DIM = 32
V_A = 64
ROPE_BASE = 10000.0
H_B = 4
DH_B = 128
MB = H_B * DH_B
CONV_K = 31
N_EXPERTS = 32
TOP_K = 4
D_FF = 1024
SWIGLU_LIMIT = 7.0
SWIGLU_ALPHA = 1.702

LANES = 128
SUBLANES = 8
HEAD_PAD = 128
ROPE_LANE0 = 64
N_COND = 16
CTX_ROW = 8

TM = 256
FB = 512
LC = 128
TQ = 256
CH = SUBLANES
SLOTS = TOP_K * TM + N_EXPERTS * CH
TABW = 128
VMEM_LIMIT = 56 * 1024 * 1024

NEG_INF = float("-inf")
LOG2_E = math.log2(math.e)


def _cparams(sem):
    return pltpu.CompilerParams(dimension_semantics=sem, vmem_limit_bytes=VMEM_LIMIT)


def _sigmoid(x):
    return 1.0 / (1.0 + jnp.exp(-x))


def _rms(x, g, eps=1e-6):
    return x * lax.rsqrt(jnp.mean(x * x, axis=-1, keepdims=True) + eps) * g


def _lane(shape):
    return lax.broadcasted_iota(I32, shape, len(shape) - 1)


def _pack_pairs(x):
    w = x.shape[1] // 2
    return pltpu.bitcast(x[:, :w], U32) | (pltpu.bitcast(x[:, w:], U32) >> 16)


def _unpack_pairs(wd):
    a = pltpu.bitcast(wd & jnp.uint32(0xFFFF0000), F32).astype(BF16)
    b = pltpu.bitcast(wd << 16, F32).astype(BF16)
    return a, b


def _mod_kernel(cond_ref, w_ref, b_ref, o_ref):
    c = cond_ref[...]
    s = (c * _sigmoid(c)).astype(BF16)
    o_ref[...] = jnp.dot(s, w_ref[...].astype(BF16), preferred_element_type=F32) + b_ref[...]


def _modulation(cond, ada_w, ada_b):
    depth = ada_w.shape[0]
    nj = ada_w.shape[2] // D
    out = pl.pallas_call(
        _mod_kernel,
        grid=(depth, nj),
        in_specs=[
            pl.BlockSpec((N_COND, D), lambda l, j: (0, 0)),
            pl.BlockSpec((None, D, D), lambda l, j: (l, 0, j)),
            pl.BlockSpec((None, 1, D), lambda l, j: (l, 0, j)),
        ],
        out_specs=pl.BlockSpec((None, N_COND, D), lambda l, j: (l, 0, j)),
        out_shape=jax.ShapeDtypeStruct((depth, N_COND, nj * D), F32),
        compiler_params=_cparams(("arbitrary", "arbitrary")),
        name="ada_modulation",
    )(cond, ada_w, ada_b.reshape(depth, 1, nj * D))
    return out.reshape(depth, N_COND, nj, D)


def _expert_rows_copy(ys_hbm, yc, sem, buf, row, slot):
    return pltpu.make_async_copy(ys_hbm.at[pl.ds(row, CH), :], yc.at[buf, pl.ds(slot, CH), :], sem.at[buf])


def _start_expert_rows(tab_ref, ys_hbm, yc, sem, tile):
    buf = tile % 2
    base = tile * TABW
    for e in range(N_EXPERTS):
        row = tab_ref[base + N_EXPERTS + e]
        slot = tab_ref[base + 2 * N_EXPERTS + e]

        def body(j, carry, row=row, slot=slot):
            _expert_rows_copy(ys_hbm, yc, sem, buf, pl.multiple_of(row + j * CH, CH),
                              pl.multiple_of(slot + j * CH, CH)).start()
            return carry

        lax.fori_loop(0, tab_ref[base + e], body, 0)


def _fetch_expert_rows(tab_ref, ys_hbm, yc, sem):
    i = pl.program_id(0)

    @pl.when(i == 0)
    def _():
        yc[...] = jnp.zeros_like(yc)
        _start_expert_rows(tab_ref, ys_hbm, yc, sem, i)

    @pl.when(i + 1 < pl.num_programs(0))
    def _():
        _start_expert_rows(tab_ref, ys_hbm, yc, sem, i + 1)

    def wait_body(j, carry):
        _expert_rows_copy(ys_hbm, yc, sem, i % 2, 0, 0).wait()
        return carry

    lax.fori_loop(0, tab_ref[i * TABW + 3 * N_EXPERTS], wait_body, 0)
    return yc[i % 2]


def _combine(route, rows):
    lane = _lane((route.shape[0], SLOTS)).astype(F32)
    g = jnp.where(lane == route[:, 8:9], route[:, 4:5], 0.0)
    for k in range(1, TOP_K):
        g = g + jnp.where(lane == route[:, 8 + k:9 + k], route[:, 4 + k:5 + k], 0.0)
    gb = g.astype(BF16)
    ya, yb = _unpack_pairs(rows)
    return jnp.concatenate([jnp.dot(gb, ya, preferred_element_type=F32),
                            jnp.dot(gb, yb, preferred_element_type=F32)], axis=-1)


def _moe_route(x1, mod_ref, n2g, wr_ref, br_ref, xc_ref, route_ref, cnt_ref):
    h2 = _rms(x1, n2g) * (1.0 + mod_ref[4:5, :]) + mod_ref[3:4, :]
    h2b = h2.astype(BF16)
    logits = jnp.dot(h2b, wr_ref[...], preferred_element_type=F32) + br_ref[...]
    tm = logits.shape[0]
    lane = _lane((tm, LANES))
    lanef = lane.astype(F32)
    l = jnp.where(lane < N_EXPERTS, logits, NEG_INF)
    tops, idxs, sels = [], [], []
    for _ in range(TOP_K):
        mv = jnp.max(l, axis=-1, keepdims=True)
        idx = jnp.min(jnp.where(l == mv, lanef, float(LANES)), axis=-1, keepdims=True)
        sel = lanef == idx
        l = jnp.where(sel, NEG_INF, l)
        tops.append(mv)
        idxs.append(idx)
        sels.append(sel)
    ex = [jnp.exp(t - tops[0]) for t in tops]
    den = ex[0] + ex[1] + ex[2] + ex[3]
    gates = [e / den for e in ex]

    onehot = jnp.zeros((tm, LANES), F32)
    for sel in sels:
        onehot = onehot + jnp.where(sel, 1.0, 0.0)
    row = lax.broadcasted_iota(I32, (tm, tm), 0)
    col = lax.broadcasted_iota(I32, (tm, tm), 1)
    strict = jnp.where(col < row, 1.0, 0.0).astype(BF16)
    before = jnp.dot(strict, onehot.astype(BF16), preferred_element_type=F32)
    n_e = jnp.sum(onehot, axis=0, keepdims=True)
    cnt_ref[...] = jnp.broadcast_to(n_e, cnt_ref.shape)
    nch = jnp.right_shift(n_e.astype(I32) + (CH - 1), CH.bit_length() - 1).astype(F32)
    r128 = lax.broadcasted_iota(I32, (LANES, LANES), 0)
    c128 = lax.broadcasted_iota(I32, (LANES, LANES), 1)
    upper = jnp.where(r128 < c128, 1.0, 0.0).astype(BF16)
    seg0 = float(CH) * jnp.dot(jnp.broadcast_to(nch, (SUBLANES, LANES)).astype(BF16), upper,
                               preferred_element_type=F32)[0:1, :]
    slot_of = seg0 + before
    slots = [jnp.sum(jnp.where(sel, slot_of, 0.0), axis=-1, keepdims=True) for sel in sels]

    r = jnp.zeros((tm, LANES), F32)
    for j, colv in enumerate(idxs + gates + slots):
        r = jnp.where(lane == j, colv, r)
    route_ref[...] = r

    lane_s = _lane((tm, SLOTS)).astype(F32)
    pt = jnp.where(lane_s == slots[0], 1.0, 0.0)
    for k in range(1, TOP_K):
        pt = pt + jnp.where(lane_s == slots[k], 1.0, 0.0)
    xc = jnp.dot(pt.T.astype(BF16), h2b, preferred_element_type=F32)
    xc_ref[...] = _pack_pairs(xc)


def _values_with_ones(ckvb, wv_ref):
    vv = jnp.dot(ckvb, wv_ref[...], preferred_element_type=F32)
    return jnp.where((_lane(vv.shape) & (HEAD_PAD - 1)) == V_A, 1.0, vv).astype(BF16)


def _rope(x, tab_ref):
    return x * tab_ref[0] + pltpu.roll(x, LANES - 8, 1) * tab_ref[1] + pltpu.roll(x, 8, 1) * tab_ref[2]


def _ab_in_kernel(npt, xp_ref, xs_ref, mod_ref, n1g_ref, win_ref, qg_ref, wq_ref, kvg_ref, wk_ref, wv_ref, gb_ref,
                  tab_ref, q_ref, k_ref, v_ref, ckv_ref, misc_ref, mqkv_ref, mo_ref):
    x = jnp.where(pl.program_id(0) < npt, xp_ref[...], xs_ref[...])
    h = _rms(x, n1g_ref[...]) * (1.0 + mod_ref[1:2, :]) + mod_ref[0:1, :]
    z = jnp.dot(h.astype(BF16), win_ref[...], preferred_element_type=F32)

    qn = _rms(z[:, :Q_RANK], qg_ref[...]).astype(BF16)
    q = jnp.dot(qn, wq_ref[...], preferred_element_type=F32)
    ckv = _rms(z[:, Q_RANK:Q_RANK + KV_RANK], kvg_ref[...])
    ckv_ref[...] = ckv
    ckvb = ckv.astype(BF16)
    kn = jnp.dot(ckvb, wk_ref[...], preferred_element_type=F32)
    v_ref[...] = _values_with_ones(ckvb, wv_ref)

    misc = z[:, Q_RANK + KV_RANK:Q_RANK + KV_RANK + LANES]
    misc_ref[...] = misc + gb_ref[...]
    lane = _lane(misc.shape)
    kr = jnp.where((lane >= ROPE_LANE0) & (lane < ROPE_LANE0 + ROPE_DIM), _rope(misc, tab_ref), 0.0)
    scale = float((NOPE + ROPE_DIM) ** -0.5 * LOG2_E)
    for hh in range(H_A):
        sl = slice(hh * HEAD_PAD, (hh + 1) * HEAD_PAD)
        q_ref[:, sl] = (_rope(q[:, sl], tab_ref) * scale).astype(BF16)
        k_ref[:, sl] = (kn[:, sl] + kr).astype(BF16)

    m0 = Q_RANK + KV_RANK + LANES
    mqkv_ref[:, :MB] = z[:, m0:m0 + MB].astype(BF16)
    mqkv_ref[:, MB:2 * MB] = (z[:, m0 + MB:m0 + 2 * MB] * float(DH_B ** -0.5)).astype(BF16)
    mqkv_ref[:, 2 * MB:] = z[:, m0 + 2 * MB:m0 + 3 * MB].astype(BF16)
    mo_ref[...] = z[:, m0 + 3 * MB:m0 + 4 * MB]


def _ctx_kv_kernel(ckv_ref, kr_ref, wk_ref, wv_ref, k_ref, v_ref):
    ckvb = ckv_ref[...].astype(BF16)
    kn = jnp.dot(ckvb, wk_ref[...], preferred_element_type=F32)
    v_ref[...] = _values_with_ones(ckvb, wv_ref)
    kr = kr_ref[...]
    for hh in range(H_A):
        sl = slice(hh * HEAD_PAD, (hh + 1) * HEAD_PAD)
        k_ref[:, sl] = (kn[:, sl] + kr).astype(BF16)


def _attn_kernel(q_ref, k_ref, v_ref, prev_ref, o_ref):
    del prev_ref
    outs = []
    for j in range(2):
        sl = slice(j * HEAD_PAD, (j + 1) * HEAD_PAD)
        s = lax.dot_general(q_ref[:, sl], k_ref[:, sl], (((1,), (1,)), ((), ())), preferred_element_type=F32)
        m = jnp.max(s, axis=-1, keepdims=True)
        p = jnp.exp2(s - m).astype(BF16)
        o = jnp.dot(p, v_ref[:, sl], preferred_element_type=F32)
        outs.append(o / o[:, V_A:V_A + 1])
    lane = _lane(outs[0].shape)
    o_ref[...] = jnp.where(lane < V_A, outs[0], pltpu.roll(outs[1], V_A, 1)).astype(BF16)


def _attention(q, k, v, qrow0, nb, s, l, prev):
    t = q.shape[0]
    tq = min(TQ, s)
    nq = s // tq
    qrow = lambda bi, hp, qi: (qrow0 // tq + bi * nq + qi, hp)
    krow = lambda bi, hp, qi: (bi, hp)
    ins = [q, k, v, prev]
    in_specs = [pl.BlockSpec((tq, 2 * HEAD_PAD), qrow), pl.BlockSpec((l, 2 * HEAD_PAD), krow),
                pl.BlockSpec((l, 2 * HEAD_PAD), krow), pl.BlockSpec(memory_space=pl.ANY)]
    aliases = {3: 0}
    return pl.pallas_call(
        _attn_kernel,
        grid=(nb, H_A // 2, nq),
        in_specs=in_specs,
        out_specs=pl.BlockSpec((tq, 2 * V_A), qrow),
        out_shape=jax.ShapeDtypeStruct((t, H_A * V_A), BF16),
        input_output_aliases=aliases,
        compiler_params=_cparams(("arbitrary", "arbitrary", "arbitrary")),
        name="mla_attention",
    )(*ins)


def _log_sigmoid(x):
    return jnp.minimum(x, 0.0) - jnp.log(1.0 + jnp.exp(-jnp.abs(x)))


def _mlstm_direction(d, q_ref, k_ref, v_ref, g_ref, h_ref, c_s, n_s, m_s):
    gates = g_ref[...]
    lf_all = _log_sigmoid(gates)
    row = lax.broadcasted_iota(I32, (LC, LC), 0)
    col = lax.broadcasted_iota(I32, (LC, LC), 1)
    mask = (col <= row) if d == 0 else (col >= row)
    eye = row == col
    cum = jnp.dot(jnp.where(mask, 1.0, 0.0), lf_all, preferred_element_type=F32,
                  precision=lax.Precision.HIGHEST)

    for hh in range(H_B):
        sl = slice(hh * DH_B, (hh + 1) * DH_B)
        q = q_ref[:, sl]
        k = k_ref[:, sl]
        v = v_ref[:, sl]
        b_col = cum[:, 8 * d + 4 + hh:8 * d + 5 + hh]
        i_col = gates[:, 8 * d + hh:8 * d + hh + 1]
        m_prev = m_s[d, hh][:, 0:1]
        r_row = jnp.sum(jnp.where(eye, i_col - b_col, 0.0), axis=0, keepdims=True)
        dm = jnp.where(mask, b_col + r_row, NEG_INF)
        a_col = b_col + m_prev
        mt = jnp.maximum(a_col, jnp.max(dm, axis=-1, keepdims=True))
        w_inter = jnp.exp(a_col - mt)
        qk = lax.dot_general(q, k, (((1,), (1,)), ((), ())), preferred_element_type=F32)
        s = qk * jnp.exp(dm - mt)
        c_prev = c_s[d, hh]
        n_prev = n_s[d, hh]
        num = w_inter * jnp.dot(q, c_prev.astype(BF16), preferred_element_type=F32) \
            + jnp.dot(s.astype(BF16), v, preferred_element_type=F32)
        qf = q.astype(F32)
        den = w_inter * jnp.sum(qf * n_prev, axis=-1, keepdims=True) + jnp.sum(s, axis=-1, keepdims=True)
        h_ref[:, sl] = num / jnp.maximum(jnp.abs(den), jnp.exp(-mt))

        b_tot = b_col[LC - 1:LC, :] if d == 0 else b_col[0:1, :]
        g_col = b_tot - b_col + i_col
        m_new = jnp.maximum(b_tot + m_prev, jnp.max(g_col, axis=0, keepdims=True))
        w_c = jnp.exp(b_tot + m_prev - m_new)
        kw = k.astype(F32) * jnp.exp(g_col - m_new)
        c_s[d, hh] = w_c * c_prev + jnp.dot(kw.T.astype(BF16), v, preferred_element_type=F32)
        n_s[d, hh] = w_c * n_prev + jnp.sum(kw, axis=0, keepdims=True)
        m_s[d, hh] = jnp.broadcast_to(m_new, (1, LANES))


def _mlstm_kernel(qf_ref, kf_ref, vf_ref, gf_ref, qb_ref, kb_ref, vb_ref, gb_ref, c0_ref, n0_ref, m0_ref,
                  prevf_ref, prevb_ref, hf_ref, hb_ref, c_out, n_out, m_out, c_s, n_s, m_s):
    del prevf_ref, prevb_ref
    c = pl.program_id(1)

    @pl.when(c == 0)
    def _():
        c_s[...] = c0_ref[...]
        n_s[...] = n0_ref[...]
        m_s[...] = m0_ref[...]

    _mlstm_direction(0, qf_ref, kf_ref, vf_ref, gf_ref, hf_ref, c_s, n_s, m_s)
    _mlstm_direction(1, qb_ref, kb_ref, vb_ref, gb_ref, hb_ref, c_s, n_s, m_s)

    @pl.when(c == pl.num_programs(1) - 1)
    def _():
        c_out[...] = c_s[...]
        n_out[...] = n_s[...]
        m_out[...] = m_s[...]


def _mlstm(mqkv, misc, c0, n0, m0, row0, nb, s, prev):
    nc = s // LC
    blk0 = row0 // LC
    t = mqkv.shape[0]
    fwd = lambda bi, ci: blk0 + bi * nc + ci
    bwd = lambda bi, ci: blk0 + bi * nc + nc - 1 - ci
    chunk_specs = lambda rows: [
        pl.BlockSpec((LC, MB), lambda bi, ci: (rows(bi, ci), 0)),
        pl.BlockSpec((LC, MB), lambda bi, ci: (rows(bi, ci), 1)),
        pl.BlockSpec((LC, MB), lambda bi, ci: (rows(bi, ci), 2)),
        pl.BlockSpec((LC, LANES), lambda bi, ci: (rows(bi, ci), 0)),
    ]
    state_spec = lambda shape: pl.BlockSpec((None,) + shape, lambda bi, ci: (bi,) + (0,) * len(shape))
    state_specs = [state_spec((2, H_B, DH_B, DH_B)), state_spec((2, H_B, 1, DH_B)), state_spec((2, H_B, 1, LANES))]
    any_spec = pl.BlockSpec(memory_space=pl.ANY)
    return pl.pallas_call(
        _mlstm_kernel,
        grid=(nb, nc),
        in_specs=chunk_specs(fwd) + chunk_specs(bwd) + state_specs + [any_spec, any_spec],
        out_specs=[
            pl.BlockSpec((LC, MB), lambda bi, ci: (fwd(bi, ci), 0)),
            pl.BlockSpec((LC, MB), lambda bi, ci: (bwd(bi, ci), 0)),
        ] + state_specs,
        out_shape=[
            jax.ShapeDtypeStruct((t, MB), F32),
            jax.ShapeDtypeStruct((t, MB), F32),
            jax.ShapeDtypeStruct((nb, 2, H_B, DH_B, DH_B), F32),
            jax.ShapeDtypeStruct((nb, 2, H_B, 1, DH_B), F32),
            jax.ShapeDtypeStruct((nb, 2, H_B, 1, LANES), F32),
        ],
        scratch_shapes=[
            pltpu.VMEM((2, H_B, DH_B, DH_B), F32),
            pltpu.VMEM((2, H_B, 1, DH_B), F32),
            pltpu.VMEM((2, H_B, 1, LANES), F32),
        ],
        input_output_aliases={11: 0, 12: 1},
        compiler_params=_cparams(("arbitrary", "arbitrary")),
        name="mlstm_scan",
    )(mqkv, mqkv, mqkv, misc, mqkv, mqkv, mqkv, misc, c0, n0, m0, prev[0], prev[1])


def _ab_out_kernel(npt, oa_ref, hf_ref, hb_ref, mo_ref, mg_ref, wout_ref, xp_ref, xs_ref, mod_ref, n2g_ref, wr_ref,
                   br_ref, x1_ref, xc_ref, route_ref, cnt_ref):
    hsum = hf_ref[...] + hb_ref[...]
    parts = []
    for hh in range(H_B):
        hh_ = hsum[:, hh * DH_B:(hh + 1) * DH_B]
        mu = jnp.mean(hh_, axis=-1, keepdims=True)
        var = jnp.mean(jnp.square(hh_ - mu), axis=-1, keepdims=True)
        parts.append((hh_ - mu) * lax.rsqrt(var + 1e-5))
    hn = jnp.concatenate(parts, axis=-1) * mg_ref[...]
    ob = (hn * _sigmoid(mo_ref[...])).astype(BF16)
    o = jnp.dot(oa_ref[...], wout_ref[:H_A * V_A, :], preferred_element_type=F32) \
        + jnp.dot(ob, wout_ref[H_A * V_A:, :], preferred_element_type=F32)
    x = jnp.where(pl.program_id(0) < npt, xp_ref[...], xs_ref[...])
    x1 = x + mod_ref[2:3, :] * o
    x1_ref[...] = x1
    _moe_route(x1, mod_ref, n2g_ref[...], wr_ref, br_ref, xc_ref, route_ref, cnt_ref)


def _conv_in_kernel(tab_ref, ys_hbm, x1_ref, route_ref, modp_ref, mod_ref, n1g_ref, w_ref,
                    x2_ref, u_ref, yc, sem):
    rows = _fetch_expert_rows(tab_ref, ys_hbm, yc, sem)
    x2 = x1_ref[...] + modp_ref[5:6, :] * _combine(route_ref[...], rows)
    x2_ref[...] = x2
    h = _rms(x2, n1g_ref[...]) * (1.0 + mod_ref[1:2, :]) + mod_ref[0:1, :]
    z = jnp.dot(h.astype(BF16), w_ref[...], preferred_element_type=F32)
    u_ref[...] = z[:, :D] * _sigmoid(z[:, D:])


HALO = 16


def _conv_out_kernel(npt, tiles_per_seq, u_ref, up_ref, un_ref, wdw_ref, bdw_ref, lg_ref, lb_ref, w2_ref,
                     x_ref, mod_ref, n2g_ref, wr_ref, br_ref, x1_ref, xc_ref, route_ref, cnt_ref, ext_ref, sh_ref):
    i = pl.program_id(0)
    j = i - npt
    in_prompt = i < npt
    first = in_prompt | (j % tiles_per_seq == 0)
    last = in_prompt | (j % tiles_per_seq == tiles_per_seq - 1)
    ext_ref[0:HALO, :] = jnp.where(first, 0.0, up_ref[...])
    ext_ref[HALO:HALO + TM, :] = u_ref[...]
    ext_ref[HALO + TM:, :] = jnp.where(last, 0.0, un_ref[...])

    for b in range(SUBLANES):
        sh_ref[b] = ext_ref[b + 1:b + 1 + TM + 24, :]
    acc = jnp.zeros((TM, D), F32) + bdw_ref[...]
    for tap in range(CONV_K):
        a, b = divmod(tap, SUBLANES)
        acc = acc + wdw_ref[tap:tap + 1, :] * sh_ref[b, SUBLANES * a:SUBLANES * a + TM, :]
    mu = jnp.mean(acc, axis=-1, keepdims=True)
    var = jnp.mean(jnp.square(acc - mu), axis=-1, keepdims=True)
    ln = (acc - mu) * lax.rsqrt(var + 1e-5) * lg_ref[...] + lb_ref[...]
    act = (ln * _sigmoid(ln)).astype(BF16)
    o = jnp.dot(act, w2_ref[...], preferred_element_type=F32)
    x1 = x_ref[...] + mod_ref[2:3, :] * o
    x1_ref[...] = x1
    _moe_route(x1, mod_ref, n2g_ref[...], wr_ref, br_ref, xc_ref, route_ref, cnt_ref)


CPB = FB // CH


def _ffn_kernel(be_ref, nused_ref, src_ref, xc_hbm, win_ref, bin_ref, wout_ref, bout_ref, y_ref,
                win_s, wout_s, xbuf, sem):
    i = pl.program_id(0)
    n_used = nused_ref[0]
    e = be_ref[i]
    e_prev = be_ref[jnp.maximum(i - 1, 0)]

    def fetch(blk):
        buf = blk % 2
        for c in range(CPB):
            row = pl.multiple_of(src_ref[blk * CPB + c], CH)
            pltpu.make_async_copy(xc_hbm.at[pl.ds(row, CH), :], xbuf.at[buf, pl.ds(c * CH, CH), :], sem.at[buf]).start()

    @pl.when((i == 0) & (n_used > 0))
    def _():
        fetch(i)

    @pl.when(i + 1 < n_used)
    def _():
        fetch(i + 1)

    @pl.when((i == 0) | (e != e_prev))
    def _():
        win_s[...] = win_ref[...].astype(BF16)
        wout_s[...] = wout_ref[...].astype(BF16)

    @pl.when(i < n_used)
    def _():
        pltpu.make_async_copy(xc_hbm.at[pl.ds(0, FB), :], xbuf.at[i % 2], sem.at[i % 2]).wait()
        xa, xb = _unpack_pairs(xbuf[i % 2])
        hb = jnp.dot(xa, win_s[:D // 2, :], preferred_element_type=F32) \
            + jnp.dot(xb, win_s[D // 2:, :], preferred_element_type=F32) + bin_ref[...]
        g = jnp.minimum(hb[:, :D_FF], SWIGLU_LIMIT)
        u = jnp.clip(hb[:, D_FF:], -SWIGLU_LIMIT, SWIGLU_LIMIT)
        act = g * _sigmoid(SWIGLU_ALPHA * g) * (u + 1.0)
        y = jnp.dot(act.astype(BF16), wout_s[...], preferred_element_type=F32) + bout_ref[...]
        y_ref[...] = _pack_pairs(y.astype(BF16).astype(F32))

    @pl.when(i >= n_used)
    def _():
        y_ref[...] = jnp.zeros_like(y_ref)


def _expert_ffn(block_e, n_used, src, xc, layer, w_in, b_in, w_out, b_out):
    n_blocks = block_e.shape[0]
    depth = w_in.shape[0]
    return pl.pallas_call(
        _ffn_kernel,
        grid_spec=pltpu.PrefetchScalarGridSpec(
            num_scalar_prefetch=3,
            grid=(n_blocks,),
            in_specs=[
                pl.BlockSpec(memory_space=pl.ANY),
                pl.BlockSpec((None, None, D, 2 * D_FF), lambda i, be, nu, sr: (layer, be[i], 0, 0)),
                pl.BlockSpec((None, None, 1, 2 * D_FF), lambda i, be, nu, sr: (layer, be[i], 0, 0)),
                pl.BlockSpec((None, None, D_FF, D), lambda i, be, nu, sr: (layer, be[i], 0, 0)),
                pl.BlockSpec((None, None, 1, D), lambda i, be, nu, sr: (layer, be[i], 0, 0)),
            ],
            out_specs=pl.BlockSpec((FB, D // 2), lambda i, be, nu, sr: (i, 0)),
            scratch_shapes=[pltpu.VMEM((D, 2 * D_FF), BF16), pltpu.VMEM((D_FF, D), BF16),
                            pltpu.VMEM((2, FB, D // 2), U32), pltpu.SemaphoreType.DMA((2,))],
        ),
        out_shape=jax.ShapeDtypeStruct((n_blocks * FB, D // 2), U32),
        compiler_params=_cparams(("arbitrary",)),
        name="moe_expert_ffn",
    )(block_e, n_used, src, xc, w_in, b_in.reshape(depth, N_EXPERTS, 1, 2 * D_FF), w_out,
      b_out.reshape(depth, N_EXPERTS, 1, D))


def _moe_plan(cnt):
    nt = cnt.shape[0]
    n_blocks = (nt * TM * TOP_K + nt * N_EXPERTS * (CH - 1)) // FB + N_EXPERTS + 1
    n = cnt[:, 0, :N_EXPERTS].astype(I32)
    nch = (n + CH - 1) // CH
    cend = jnp.cumsum(nch, axis=0)
    cstart = cend - nch
    ce = cend[-1]
    pce = (ce + CPB - 1) // CPB * CPB
    e_end = jnp.cumsum(pce)
    e_start = e_end - pce
    first_slot = CH * (jnp.cumsum(nch, axis=1) - nch)

    first_row = CH * (e_start[None, :] + cstart)
    tab = jnp.concatenate([nch, first_row, first_slot, jnp.sum(nch, axis=1, keepdims=True),
                           jnp.zeros((nt, TABW - 3 * N_EXPERTS - 1), I32)], axis=1).reshape(nt * TABW)

    d = jnp.arange(n_blocks * CPB, dtype=I32)
    e_of = jnp.minimum(jnp.sum((d[:, None] >= e_end[None, :]).astype(I32), axis=-1), N_EXPERTS - 1)
    local = d - jnp.take(e_start, e_of)
    tile_of = jnp.minimum(jnp.sum((local[:, None] >= jnp.take(cend.T, e_of, axis=0)).astype(I32), axis=-1), nt - 1)
    flat = tile_of * N_EXPERTS + e_of
    src = tile_of * SLOTS + jnp.take(first_slot.reshape(-1), flat) + CH * (local - jnp.take(cstart.reshape(-1), flat))
    src = jnp.where(local < jnp.take(ce, e_of), src, 0)
    block_e = e_of[::CPB]
    n_used = (e_end[-1] // CPB).reshape(1)
    return tab, src.astype(I32), block_e.astype(I32), n_used.astype(I32)


def _moe(cnt, xc, layer, w_in, b_in, w_out, b_out):
    tab, src, block_e, n_used = _moe_plan(cnt)
    ys = _expert_ffn(block_e, n_used, src, xc, layer, w_in, b_in, w_out, b_out)
    return tab, ys


def _final_kernel(npt, tab_ref, ys_hbm, x1_ref, route_ref, modp_ref, g_ref, op_ref, os_ref, yc, sem):
    i = pl.program_id(0)
    rows = _fetch_expert_rows(tab_ref, ys_hbm, yc, sem)
    x2 = x1_ref[...] + modp_ref[5:6, :] * _combine(route_ref[...], rows)
    out = _rms(x2, g_ref[...])

    @pl.when(i < npt)
    def _():
        op_ref[...] = out

    @pl.when(i >= npt)
    def _():
        os_ref[...] = out


def _rope_tables(n_tokens):
    rows = n_tokens // GRID_W
    pos_row = jnp.repeat(jnp.arange(rows, dtype=F32), GRID_W)
    pos_col = jnp.tile(jnp.arange(GRID_W, dtype=F32), rows)
    n_freq = ROPE_DIM // 4
    inv_freq = ROPE_BASE ** (-jnp.arange(n_freq, dtype=F32) / n_freq)
    ang = jnp.stack([pos_row[:, None] * inv_freq, pos_col[:, None] * inv_freq], axis=1)
    cos, sin = jnp.cos(ang), jnp.sin(ang)
    zero = jnp.zeros_like(sin)
    c = jnp.stack([cos, cos], axis=2).reshape(n_tokens, ROPE_DIM)
    sm = jnp.stack([-sin, zero], axis=2).reshape(n_tokens, ROPE_DIM)
    sp = jnp.stack([zero, sin], axis=2).reshape(n_tokens, ROPE_DIM)

    def place(a, fill):
        left = jnp.full((n_tokens, ROPE_LANE0), fill, F32)
        right = jnp.full((n_tokens, LANES - ROPE_LANE0 - ROPE_DIM), fill, F32)
        body = jnp.concatenate([left, a, right], axis=1)
        ident = jnp.full((TM, LANES), fill, F32)
        return jnp.concatenate([body, ident], axis=0)

    return jnp.stack([place(c, 1.0), place(sm, 0.0), place(sp, 0.0)], axis=0)


def kernel(x_prompt, x_sample, c, cache_mla_ckv, cache_mla_krope, state_mlstm_C, state_mlstm_n, state_mlstm_m, c_ctx, ada_w, ada_b, norm1_g, norm2_g, ab_w_in, mla_q_norm_g, mla_w_uq, mla_kv_norm_g, mla_w_ukv, mlstm_gate_b, mlstm_norm_g, ab_w_out, conv_w_pw1, conv_w_dw, conv_b_dw, conv_ln_g, conv_ln_b, conv_w_pw2, router_w, router_b, moe_w_in, moe_b_in, moe_w_out, moe_b_out, final_norm_g):
    bp, sp_, _ = x_prompt.shape
    bs, ss, _ = x_sample.shape
    past = cache_mla_ckv.shape[2]
    tp, ts = bp * sp_, bs * ss
    t = tp + ts
    nt = t // TM
    npt = tp // TM
    tps = ss // TM
    assert sp_ == TM and ss % TM == 0 and bs <= CTX_ROW and tp % ss == 0

    xp2 = x_prompt.reshape(tp, D)
    xs2 = x_sample.reshape(ts, D)
    cond = jnp.zeros((N_COND, D), F32).at[:bs].set(c).at[CTX_ROW].set(c_ctx)
    mod = _modulation(cond, ada_w, ada_b)

    def mod_row(i):
        return jnp.where(i < npt, CTX_ROW, (i - npt) // tps)

    def mod_spec(layer):
        return pl.BlockSpec((None, None, 6, D), lambda i, *_: (layer, mod_row(i), 0, 0))

    row_spec = lambda w: pl.BlockSpec((TM, w), lambda i, *_: (i, 0))
    prompt_spec = pl.BlockSpec((TM, D), lambda i, *_: (jnp.minimum(i, npt - 1), 0))
    sample_spec = pl.BlockSpec((TM, D), lambda i, *_: (jnp.maximum(i - npt, 0), 0))
    full = lambda a: pl.BlockSpec(a.shape, lambda i, *_: (0,) * a.ndim)
    vec = lambda a: a.reshape(1, -1)
    any_spec = pl.BlockSpec(memory_space=pl.ANY)

    def router_args(layer):
        wr = jnp.pad(router_w[layer], ((0, 0), (0, LANES - N_EXPERTS))).astype(BF16)
        br = jnp.pad(router_b[layer], (0, LANES - N_EXPERTS)).reshape(1, LANES)
        return wr, br

    moe_out_shapes = [
        jax.ShapeDtypeStruct((t, D), F32),
        jax.ShapeDtypeStruct((nt * SLOTS, D // 2), U32),
        jax.ShapeDtypeStruct((t, LANES), F32),
        jax.ShapeDtypeStruct((nt, SUBLANES, LANES), F32),
    ]
    moe_out_specs = [row_spec(D), pl.BlockSpec((SLOTS, D // 2), lambda i: (i, 0)), row_spec(LANES),
                     pl.BlockSpec((None, SUBLANES, LANES), lambda i: (i, 0, 0))]

    j = 0
    w = ab_w_in[j]
    zc = lambda n: jnp.zeros((D, n), F32)
    misc_w = jnp.concatenate([w[:, 2464:2480], zc(ROPE_LANE0 - 16), w[:, 384:416], zc(LANES - ROPE_LANE0 - ROPE_DIM)], axis=1)
    w_in = jnp.concatenate([w[:, :384], misc_w, w[:, 416:2464]], axis=1).astype(BF16)
    wq = jnp.pad(mla_w_uq[j].reshape(Q_RANK, H_A, NOPE + ROPE_DIM), ((0, 0), (0, 0), (0, HEAD_PAD - NOPE - ROPE_DIM)))
    wq = wq.reshape(Q_RANK, H_A * HEAD_PAD).astype(BF16)
    wkv = mla_w_ukv[j].reshape(KV_RANK, H_A, NOPE + V_A)
    wk = jnp.pad(wkv[:, :, :NOPE], ((0, 0), (0, 0), (0, HEAD_PAD - NOPE))).reshape(KV_RANK, H_A * HEAD_PAD).astype(BF16)
    wv = jnp.pad(wkv[:, :, NOPE:], ((0, 0), (0, 0), (0, HEAD_PAD - V_A))).reshape(KV_RANK, H_A * HEAD_PAD).astype(BF16)
    gate_b = jnp.pad(mlstm_gate_b[j], (0, LANES - 4 * H_B)).reshape(1, LANES)
    tabs = _rope_tables(ss)
    pos_blocks = ss // TM

    def tab_block(i):
        return jnp.where(i < npt, pos_blocks, (i - npt) % tps)

    ins = [xp2, xs2, mod, vec(norm1_g[0]), w_in, vec(mla_q_norm_g[j]), wq, vec(mla_kv_norm_g[j]), wk, wv, gate_b, tabs]
    in_specs = [prompt_spec, sample_spec, mod_spec(0)] + [full(a) for a in ins[3:11]] + [
        pl.BlockSpec((3, TM, LANES), lambda i: (0, tab_block(i), 0))]
    q, k, v, ckv, misc, mqkv, mo = pl.pallas_call(
        functools.partial(_ab_in_kernel, npt),
        grid=(nt,),
        in_specs=in_specs,
        out_specs=[row_spec(H_A * HEAD_PAD), row_spec(H_A * HEAD_PAD), row_spec(H_A * HEAD_PAD), row_spec(KV_RANK),
                   row_spec(LANES), row_spec(3 * MB), row_spec(MB)],
        out_shape=[
            jax.ShapeDtypeStruct((t, H_A * HEAD_PAD), BF16), jax.ShapeDtypeStruct((t, H_A * HEAD_PAD), BF16),
            jax.ShapeDtypeStruct((t, H_A * HEAD_PAD), BF16), jax.ShapeDtypeStruct((t, KV_RANK), F32),
            jax.ShapeDtypeStruct((t, LANES), F32), jax.ShapeDtypeStruct((t, 3 * MB), BF16),
            jax.ShapeDtypeStruct((t, MB), F32)],
        compiler_params=_cparams(("arbitrary",)),
        name="ab_in_proj",
    )(*ins)

    nctx = bs * past
    ctx_ckv = cache_mla_ckv[:, j].reshape(nctx, KV_RANK)
    ctx_kr = jnp.pad(cache_mla_krope[:, j].reshape(nctx, ROPE_DIM), ((0, 0), (ROPE_LANE0, LANES - ROPE_LANE0 - ROPE_DIM)))
    k_ctx, v_ctx = pl.pallas_call(
        _ctx_kv_kernel,
        grid=(nctx // TM,),
        in_specs=[row_spec(KV_RANK), row_spec(LANES), full(wk), full(wv)],
        out_specs=[row_spec(H_A * HEAD_PAD), row_spec(H_A * HEAD_PAD)],
        out_shape=[jax.ShapeDtypeStruct((nctx, H_A * HEAD_PAD), BF16)] * 2,
        compiler_params=_cparams(("arbitrary",)),
        name="ctx_kv",
    )(ctx_ckv, ctx_kr, wk, wv)

    o_a = _attention(q, k, v, 0, bp, sp_, sp_, jnp.zeros((t, H_A * V_A), BF16))
    seq_keys = lambda ctx, own: jnp.concatenate(
        [ctx.reshape(bs, past, -1), own[tp:].reshape(bs, ss, -1)], axis=1).reshape(bs * (past + ss), -1)
    o_a = _attention(q, seq_keys(k_ctx, k), seq_keys(v_ctx, v), tp, bs, ss, past + ss, o_a)

    zc0 = jnp.zeros((bp, 2, H_B, DH_B, DH_B), F32)
    zn0 = jnp.zeros((bp, 2, H_B, 1, DH_B), F32)
    zm0 = jnp.zeros((bp, 2, H_B, 1, LANES), F32)
    h_f, h_b, c_p, n_p, m_p = _mlstm(mqkv, misc, zc0, zn0, zm0, 0, bp, sp_,
                                     (jnp.zeros((t, MB), F32), jnp.zeros((t, MB), F32)))
    sm0 = jnp.broadcast_to(state_mlstm_m[:, j][..., None, None], (bs, 2, H_B, 1, LANES))
    h_f, h_b, _, _, _ = _mlstm(mqkv, misc, state_mlstm_C[:, j], state_mlstm_n[:, j][:, :, :, None, :], sm0, tp, bs, ss,
                               (h_f, h_b))

    wr, br = router_args(0)
    ins = [o_a, h_f, h_b, mo, vec(mlstm_norm_g[j]), ab_w_out[j].astype(BF16), xp2, xs2, mod, vec(norm2_g[0]), wr, br]
    in_specs = [row_spec(H_A * V_A), row_spec(MB), row_spec(MB), row_spec(MB), full(ins[4]), full(ins[5]),
                prompt_spec, sample_spec, mod_spec(0), full(ins[9]), full(wr), full(br)]
    x1, xc, route, cnt = pl.pallas_call(
        functools.partial(_ab_out_kernel, npt),
        grid=(nt,),
        in_specs=in_specs,
        out_specs=moe_out_specs,
        out_shape=moe_out_shapes,
        compiler_params=_cparams(("arbitrary",)),
        name="ab_out_proj",
    )(*ins)
    tab, ys = _moe(cnt, xc, 0, moe_w_in, moe_b_in, moe_w_out, moe_b_out)

    gather_scratch = [pltpu.VMEM((2, SLOTS, D // 2), U32), pltpu.SemaphoreType.DMA((2,))]
    ins = [tab, ys, x1, route, mod, mod, vec(norm1_g[1]), conv_w_pw1[0].astype(BF16)]
    x2, u = pl.pallas_call(
        _conv_in_kernel,
        grid_spec=pltpu.PrefetchScalarGridSpec(
            num_scalar_prefetch=1,
            grid=(nt,),
            in_specs=[any_spec, row_spec(D), row_spec(LANES), mod_spec(0), mod_spec(1), full(ins[6]), full(ins[7])],
            out_specs=[row_spec(D), row_spec(D)],
            scratch_shapes=gather_scratch,
        ),
        out_shape=[jax.ShapeDtypeStruct((t, D), F32), jax.ShapeDtypeStruct((t, D), F32)],
        compiler_params=_cparams(("arbitrary",)),
        name="conv_in_proj",
    )(*ins)

    hpt = TM // HALO
    nhalo = t // HALO
    wr, br = router_args(1)
    ins = [u, u, u, conv_w_dw[0], vec(conv_b_dw[0]), vec(conv_ln_g[0]), vec(conv_ln_b[0]), conv_w_pw2[0].astype(BF16),
           x2, mod, vec(norm2_g[1]), wr, br]
    in_specs = [row_spec(D),
                pl.BlockSpec((HALO, D), lambda i: (jnp.maximum(i * hpt - 1, 0), 0)),
                pl.BlockSpec((HALO, D), lambda i: (jnp.minimum((i + 1) * hpt, nhalo - 1), 0)),
                full(ins[3]), full(ins[4]), full(ins[5]), full(ins[6]), full(ins[7]), row_spec(D), mod_spec(1),
                full(ins[10]), full(wr), full(br)]
    x1, xc, route, cnt = pl.pallas_call(
        functools.partial(_conv_out_kernel, npt, tps),
        grid=(nt,),
        in_specs=in_specs,
        out_specs=moe_out_specs,
        out_shape=moe_out_shapes,
        scratch_shapes=[pltpu.VMEM((TM + 2 * HALO, D), F32), pltpu.VMEM((SUBLANES, TM + 24, D), F32)],
        compiler_params=_cparams(("arbitrary",)),
        name="conv_out_proj",
    )(*ins)
    tab, ys = _moe(cnt, xc, 1, moe_w_in, moe_b_in, moe_w_out, moe_b_out)

    ins = [tab, ys, x1, route, mod, vec(final_norm_g)]
    y_p, y_s = pl.pallas_call(
        functools.partial(_final_kernel, npt),
        grid_spec=pltpu.PrefetchScalarGridSpec(
            num_scalar_prefetch=1,
            grid=(nt,),
            in_specs=[any_spec, row_spec(D), row_spec(LANES), mod_spec(1), full(ins[5])],
            out_specs=[prompt_spec, sample_spec],
            scratch_shapes=gather_scratch,
        ),
        out_shape=[jax.ShapeDtypeStruct((tp, D), F32), jax.ShapeDtypeStruct((ts, D), F32)],
        compiler_params=_cparams(("arbitrary",)),
        name="final_norm",
    )(*ins)

    y_prompt = y_p.reshape(bp, sp_, D)
    y_sample = y_s.reshape(bs, ss, D)
    new_ckv = ckv[:tp].reshape(bp, 1, sp_, KV_RANK)
    new_krope = misc[:tp, ROPE_LANE0:ROPE_LANE0 + ROPE_DIM].reshape(bp, 1, sp_, ROPE_DIM)
    new_c = c_p[:, None]
    new_n = n_p[:, None, :, :, 0, :]
    new_m = m_p[:, None, :, :, 0, 0]
    return (y_prompt, y_sample, new_ckv, new_krope, new_c, new_n, new_m)
```

```python
import functools
import math

import jax
import jax.numpy as jnp
from jax import lax
from jax.experimental import pallas as pl
from jax.experimental.pallas import tpu as pltpu

F32 = jnp.float32
BF16 = jnp.bfloat16
I32 = jnp.int32
U32 = jnp.uint32

D = 1024
GRID_W = 64
H_A = 8
Q_RANK = 256
KV_RANK = 128
NOPE = 64
ROPE_DIM = 32
V_A = 64
ROPE_BASE = 10000.0
H_B = 4
DH_B = 128
MB = H_B * DH_B
CONV_K = 31
N_EXPERTS = 32
TOP_K = 4
D_FF = 1024
SWIGLU_LIMIT = 7.0
SWIGLU_ALPHA = 1.702

LANES = 128
SUBLANES = 8
HEAD_PAD = 128
ROPE_LANE0 = 64
N_COND = 16
CTX_ROW = 8

TM = 256
FB = 512
LC = 128
TQ = 256
CH = SUBLANES
SLOTS = TOP_K * TM + N_EXPERTS * CH
VMEM_LIMIT = 56 * 1024 * 1024

NEG_INF = float("-inf")
LOG2_E = math.log2(math.e)


def _cparams(sem):
    return pltpu.CompilerParams(dimension_semantics=sem, vmem_limit_bytes=VMEM_LIMIT)


def _sigmoid(x):
    return 1.0 / (1.0 + jnp.exp(-x))


def _rms(x, g, eps=1e-6):
    return x * lax.rsqrt(jnp.mean(x * x, axis=-1, keepdims=True) + eps) * g


def _lane(shape):
    return lax.broadcasted_iota(I32, shape, len(shape) - 1)


def _pack_pairs(x):
    w = x.shape[1] // 2
    return pltpu.bitcast(x[:, :w], U32) | (pltpu.bitcast(x[:, w:], U32) >> 16)


def _unpack_pairs(wd):
    a = pltpu.bitcast(wd & jnp.uint32(0xFFFF0000), F32).astype(BF16)
    b = pltpu.bitcast(wd << 16, F32).astype(BF16)
    return a, b


def _mod_kernel(cond_ref, w_ref, b_ref, o_ref):
    c = cond_ref[...]
    s = (c * _sigmoid(c)).astype(BF16)
    o_ref[...] = jnp.dot(s, w_ref[...].astype(BF16), preferred_element_type=F32) + b_ref[...]


def _modulation(cond, ada_w, ada_b):
    depth = ada_w.shape[0]
    nj = ada_w.shape[2] // D
    out = pl.pallas_call(
        _mod_kernel,
        grid=(depth, nj),
        in_specs=[
            pl.BlockSpec((N_COND, D), lambda l, j: (0, 0)),
            pl.BlockSpec((None, D, D), lambda l, j: (l, 0, j)),
            pl.BlockSpec((None, 1, D), lambda l, j: (l, 0, j)),
        ],
        out_specs=pl.BlockSpec((None, N_COND, D), lambda l, j: (l, 0, j)),
        out_shape=jax.ShapeDtypeStruct((depth, N_COND, nj * D), F32),
        compiler_params=_cparams(("arbitrary", "arbitrary")),
        name="ada_modulation",
    )(cond, ada_w, ada_b.reshape(depth, 1, nj * D))
    return out.reshape(depth, N_COND, nj, D)


def _combine(route, rows):
    lane = _lane((route.shape[0], SLOTS)).astype(F32)
    g = jnp.where(lane == route[:, 8:9], route[:, 4:5], 0.0)
    for k in range(1, TOP_K):
        g = g + jnp.where(lane == route[:, 8 + k:9 + k], route[:, 4 + k:5 + k], 0.0)
    gb = g.astype(BF16)
    ya, yb = _unpack_pairs(rows)
    return jnp.concatenate([jnp.dot(gb, ya, preferred_element_type=F32),
                            jnp.dot(gb, yb, preferred_element_type=F32)], axis=-1)


def _moe_route(x1, mod_ref, n2g, wr_ref, br_ref, xc_ref, route_ref, cnt_ref):
    h2 = _rms(x1, n2g) * (1.0 + mod_ref[4:5, :]) + mod_ref[3:4, :]
    h2b = h2.astype(BF16)
    logits = jnp.dot(h2b, wr_ref[...], preferred_element_type=F32) + br_ref[...]
    tm = logits.shape[0]
    lane = _lane((tm, LANES))
    lanef = lane.astype(F32)
    l = jnp.where(lane < N_EXPERTS, logits, NEG_INF)
    tops, idxs, sels = [], [], []
    for _ in range(TOP_K):
        mv = jnp.max(l, axis=-1, keepdims=True)
        idx = jnp.min(jnp.where(l == mv, lanef, float(LANES)), axis=-1, keepdims=True)
        sel = lanef == idx
        l = jnp.where(sel, NEG_INF, l)
        tops.append(mv)
        idxs.append(idx)
        sels.append(sel)
    ex = [jnp.exp(t - tops[0]) for t in tops]
    den = ex[0] + ex[1] + ex[2] + ex[3]
    gates = [e / den for e in ex]

    onehot = jnp.zeros((tm, LANES), F32)
    for sel in sels:
        onehot = onehot + jnp.where(sel, 1.0, 0.0)
    row = lax.broadcasted_iota(I32, (tm, tm), 0)
    col = lax.broadcasted_iota(I32, (tm, tm), 1)
    strict = jnp.where(col < row, 1.0, 0.0).astype(BF16)
    before = jnp.dot(strict, onehot.astype(BF16), preferred_element_type=F32)
    n_e = jnp.sum(onehot, axis=0, keepdims=True)
    cnt_ref[...] = jnp.broadcast_to(n_e, cnt_ref.shape)
    nch = jnp.right_shift(n_e.astype(I32) + (CH - 1), CH.bit_length() - 1).astype(F32)
    r128 = lax.broadcasted_iota(I32, (LANES, LANES), 0)
    c128 = lax.broadcasted_iota(I32, (LANES, LANES), 1)
    upper = jnp.where(r128 < c128, 1.0, 0.0).astype(BF16)
    seg0 = float(CH) * jnp.dot(jnp.broadcast_to(nch, (SUBLANES, LANES)).astype(BF16), upper,
                               preferred_element_type=F32)[0:1, :]
    slot_of = seg0 + before
    slots = [jnp.sum(jnp.where(sel, slot_of, 0.0), axis=-1, keepdims=True) for sel in sels]

    r = jnp.zeros((tm, LANES), F32)
    for j, colv in enumerate(idxs + gates + slots):
        r = jnp.where(lane == j, colv, r)
    route_ref[...] = r

    lane_s = _lane((tm, SLOTS)).astype(F32)
    pt = jnp.where(lane_s == slots[0], 1.0, 0.0)
    for k in range(1, TOP_K):
        pt = pt + jnp.where(lane_s == slots[k], 1.0, 0.0)
    xc = jnp.dot(pt.T.astype(BF16), h2b, preferred_element_type=F32)
    xc_ref[...] = _pack_pairs(xc)


def _values_with_ones(ckvb, wv_ref):
    vv = jnp.dot(ckvb, wv_ref[...], preferred_element_type=F32)
    return jnp.where((_lane(vv.shape) & (HEAD_PAD - 1)) == V_A, 1.0, vv).astype(BF16)


def _rope(x, tab_ref):
    return x * tab_ref[0] + pltpu.roll(x, LANES - 8, 1) * tab_ref[1] + pltpu.roll(x, 8, 1) * tab_ref[2]


def _ab_in_kernel(npt, xp_ref, xs_ref, mod_ref, n1g_ref, win_ref, qg_ref, wq_ref, kvg_ref, wk_ref, wv_ref, gb_ref,
                  tab_ref, q_ref, k_ref, v_ref, ckv_ref, misc_ref, mqkv_ref, mo_ref):
    x = jnp.where(pl.program_id(0) < npt, xp_ref[...], xs_ref[...])
    h = _rms(x, n1g_ref[...]) * (1.0 + mod_ref[1:2, :]) + mod_ref[0:1, :]
    z = jnp.dot(h.astype(BF16), win_ref[...], preferred_element_type=F32)

    qn = _rms(z[:, :Q_RANK], qg_ref[...]).astype(BF16)
    q = jnp.dot(qn, wq_ref[...], preferred_element_type=F32)
    ckv = _rms(z[:, Q_RANK:Q_RANK + KV_RANK], kvg_ref[...])
    ckv_ref[...] = ckv
    ckvb = ckv.astype(BF16)
    kn = jnp.dot(ckvb, wk_ref[...], preferred_element_type=F32)
    v_ref[...] = _values_with_ones(ckvb, wv_ref)

    misc = z[:, Q_RANK + KV_RANK:Q_RANK + KV_RANK + LANES]
    misc_ref[...] = misc + gb_ref[...]
    lane = _lane(misc.shape)
    kr = jnp.where((lane >= ROPE_LANE0) & (lane < ROPE_LANE0 + ROPE_DIM), _rope(misc, tab_ref), 0.0)
    scale = float((NOPE + ROPE_DIM) ** -0.5 * LOG2_E)
    for hh in range(H_A):
        sl = slice(hh * HEAD_PAD, (hh + 1) * HEAD_PAD)
        q_ref[:, sl] = (_rope(q[:, sl], tab_ref) * scale).astype(BF16)
        k_ref[:, sl] = (kn[:, sl] + kr).astype(BF16)

    m0 = Q_RANK + KV_RANK + LANES
    mqkv_ref[:, :MB] = z[:, m0:m0 + MB].astype(BF16)
    mqkv_ref[:, MB:2 * MB] = (z[:, m0 + MB:m0 + 2 * MB] * float(DH_B ** -0.5)).astype(BF16)
    mqkv_ref[:, 2 * MB:] = z[:, m0 + 2 * MB:m0 + 3 * MB].astype(BF16)
    mo_ref[...] = z[:, m0 + 3 * MB:m0 + 4 * MB]


def _ctx_kv_kernel(ckv_ref, kr_ref, wk_ref, wv_ref, k_ref, v_ref):
    ckvb = ckv_ref[...].astype(BF16)
    kn = jnp.dot(ckvb, wk_ref[...], preferred_element_type=F32)
    v_ref[...] = _values_with_ones(ckvb, wv_ref)
    kr = kr_ref[...]
    for hh in range(H_A):
        sl = slice(hh * HEAD_PAD, (hh + 1) * HEAD_PAD)
        k_ref[:, sl] = (kn[:, sl] + kr).astype(BF16)


def _attn_kernel(q_ref, k_ref, v_ref, prev_ref, o_ref):
    del prev_ref
    outs = []
    for j in range(2):
        sl = slice(j * HEAD_PAD, (j + 1) * HEAD_PAD)
        s = lax.dot_general(q_ref[:, sl], k_ref[:, sl], (((1,), (1,)), ((), ())), preferred_element_type=F32)
        m = jnp.max(s, axis=-1, keepdims=True)
        p = jnp.exp2(s - m).astype(BF16)
        o = jnp.dot(p, v_ref[:, sl], preferred_element_type=F32)
        outs.append(o / o[:, V_A:V_A + 1])
    lane = _lane(outs[0].shape)
    o_ref[...] = jnp.where(lane < V_A, outs[0], pltpu.roll(outs[1], V_A, 1)).astype(BF16)


def _attention(q, k, v, qrow0, nb, s, l, prev):
    t = q.shape[0]
    tq = min(TQ, s)
    nq = s // tq
    qrow = lambda bi, hp, qi: (qrow0 // tq + bi * nq + qi, hp)
    krow = lambda bi, hp, qi: (bi, hp)
    ins = [q, k, v, prev]
    in_specs = [pl.BlockSpec((tq, 2 * HEAD_PAD), qrow), pl.BlockSpec((l, 2 * HEAD_PAD), krow),
                pl.BlockSpec((l, 2 * HEAD_PAD), krow), pl.BlockSpec(memory_space=pl.ANY)]
    aliases = {3: 0}
    return pl.pallas_call(
        _attn_kernel,
        grid=(nb, H_A // 2, nq),
        in_specs=in_specs,
        out_specs=pl.BlockSpec((tq, 2 * V_A), qrow),
        out_shape=jax.ShapeDtypeStruct((t, H_A * V_A), BF16),
        input_output_aliases=aliases,
        compiler_params=_cparams(("arbitrary", "arbitrary", "arbitrary")),
        name="mla_attention",
    )(*ins)


def _log_sigmoid(x):
    return jnp.minimum(x, 0.0) - jnp.log(1.0 + jnp.exp(-jnp.abs(x)))


def _mlstm_direction(d, q_ref, k_ref, v_ref, g_ref, h_ref, c_s, n_s, m_s):
    gates = g_ref[...]
    lf_all = _log_sigmoid(gates)
    row = lax.broadcasted_iota(I32, (LC, LC), 0)
    col = lax.broadcasted_iota(I32, (LC, LC), 1)
    mask = (col <= row) if d == 0 else (col >= row)
    eye = row == col
    cum = jnp.dot(jnp.where(mask, 1.0, 0.0), lf_all, preferred_element_type=F32,
                  precision=lax.Precision.HIGHEST)

    for hh in range(H_B):
        sl = slice(hh * DH_B, (hh + 1) * DH_B)
        q = q_ref[:, sl]
        k = k_ref[:, sl]
        v = v_ref[:, sl]
        b_col = cum[:, 8 * d + 4 + hh:8 * d + 5 + hh]
        i_col = gates[:, 8 * d + hh:8 * d + hh + 1]
        m_prev = m_s[d, hh][:, 0:1]
        r_row = jnp.sum(jnp.where(eye, i_col - b_col, 0.0), axis=0, keepdims=True)
        dm = jnp.where(mask, b_col + r_row, NEG_INF)
        a_col = b_col + m_prev
        mt = jnp.maximum(a_col, jnp.max(dm, axis=-1, keepdims=True))
        w_inter = jnp.exp(a_col - mt)
        qk = lax.dot_general(q, k, (((1,), (1,)), ((), ())), preferred_element_type=F32)
        s = qk * jnp.exp(dm - mt)
        c_prev = c_s[d, hh]
        n_prev = n_s[d, hh]
        num = w_inter * jnp.dot(q, c_prev.astype(BF16), preferred_element_type=F32) \
            + jnp.dot(s.astype(BF16), v, preferred_element_type=F32)
        qf = q.astype(F32)
        den = w_inter * jnp.sum(qf * n_prev, axis=-1, keepdims=True) + jnp.sum(s, axis=-1, keepdims=True)
        h_ref[:, sl] = num / jnp.maximum(jnp.abs(den), jnp.exp(-mt))

        b_tot = b_col[LC - 1:LC, :] if d == 0 else b_col[0:1, :]
        g_col = b_tot - b_col + i_col
        m_new = jnp.maximum(b_tot + m_prev, jnp.max(g_col, axis=0, keepdims=True))
        w_c = jnp.exp(b_tot + m_prev - m_new)
        kw = k.astype(F32) * jnp.exp(g_col - m_new)
        c_s[d, hh] = w_c * c_prev + jnp.dot(kw.T.astype(BF16), v, preferred_element_type=F32)
        n_s[d, hh] = w_c * n_prev + jnp.sum(kw, axis=0, keepdims=True)
        m_s[d, hh] = jnp.broadcast_to(m_new, (1, LANES))


def _mlstm_kernel(qf_ref, kf_ref, vf_ref, gf_ref, qb_ref, kb_ref, vb_ref, gb_ref, c0_ref, n0_ref, m0_ref,
                  prevf_ref, prevb_ref, hf_ref, hb_ref, c_out, n_out, m_out, c_s, n_s, m_s):
    del prevf_ref, prevb_ref
    c = pl.program_id(1)

    @pl.when(c == 0)
    def _():
        c_s[...] = c0_ref[...]
        n_s[...] = n0_ref[...]
        m_s[...] = m0_ref[...]

    _mlstm_direction(0, qf_ref, kf_ref, vf_ref, gf_ref, hf_ref, c_s, n_s, m_s)
    _mlstm_direction(1, qb_ref, kb_ref, vb_ref, gb_ref, hb_ref, c_s, n_s, m_s)

    @pl.when(c == pl.num_programs(1) - 1)
    def _():
        c_out[...] = c_s[...]
        n_out[...] = n_s[...]
        m_out[...] = m_s[...]


def _mlstm(mqkv, misc, c0, n0, m0, row0, nb, s, prev):
    nc = s // LC
    blk0 = row0 // LC
    t = mqkv.shape[0]
    fwd = lambda bi, ci: blk0 + bi * nc + ci
    bwd = lambda bi, ci: blk0 + bi * nc + nc - 1 - ci
    chunk_specs = lambda rows: [
        pl.BlockSpec((LC, MB), lambda bi, ci: (rows(bi, ci), 0)),
        pl.BlockSpec((LC, MB), lambda bi, ci: (rows(bi, ci), 1)),
        pl.BlockSpec((LC, MB), lambda bi, ci: (rows(bi, ci), 2)),
        pl.BlockSpec((LC, LANES), lambda bi, ci: (rows(bi, ci), 0)),
    ]
    state_spec = lambda shape: pl.BlockSpec((None,) + shape, lambda bi, ci: (bi,) + (0,) * len(shape))
    state_specs = [state_spec((2, H_B, DH_B, DH_B)), state_spec((2, H_B, 1, DH_B)), state_spec((2, H_B, 1, LANES))]
    any_spec = pl.BlockSpec(memory_space=pl.ANY)
    return pl.pallas_call(
        _mlstm_kernel,
        grid=(nb, nc),
        in_specs=chunk_specs(fwd) + chunk_specs(bwd) + state_specs + [any_spec, any_spec],
        out_specs=[
            pl.BlockSpec((LC, MB), lambda bi, ci: (fwd(bi, ci), 0)),
            pl.BlockSpec((LC, MB), lambda bi, ci: (bwd(bi, ci), 0)),
        ] + state_specs,
        out_shape=[
            jax.ShapeDtypeStruct((t, MB), F32),
            jax.ShapeDtypeStruct((t, MB), F32),
            jax.ShapeDtypeStruct((nb, 2, H_B, DH_B, DH_B), F32),
            jax.ShapeDtypeStruct((nb, 2, H_B, 1, DH_B), F32),
            jax.ShapeDtypeStruct((nb, 2, H_B, 1, LANES), F32),
        ],
        scratch_shapes=[
            pltpu.VMEM((2, H_B, DH_B, DH_B), F32),
            pltpu.VMEM((2, H_B, 1, DH_B), F32),
            pltpu.VMEM((2, H_B, 1, LANES), F32),
        ],
        input_output_aliases={11: 0, 12: 1},
        compiler_params=_cparams(("arbitrary", "arbitrary")),
        name="mlstm_scan",
    )(mqkv, mqkv, mqkv, misc, mqkv, mqkv, mqkv, misc, c0, n0, m0, prev[0], prev[1])


def _ab_out_kernel(npt, oa_ref, hf_ref, hb_ref, mo_ref, mg_ref, wout_ref, xp_ref, xs_ref, mod_ref, n2g_ref, wr_ref,
                   br_ref, x1_ref, xc_ref, route_ref, cnt_ref):
    hsum = hf_ref[...] + hb_ref[...]
    parts = []
    for hh in range(H_B):
        hh_ = hsum[:, hh * DH_B:(hh + 1) * DH_B]
        mu = jnp.mean(hh_, axis=-1, keepdims=True)
        var = jnp.mean(jnp.square(hh_ - mu), axis=-1, keepdims=True)
        parts.append((hh_ - mu) * lax.rsqrt(var + 1e-5))
    hn = jnp.concatenate(parts, axis=-1) * mg_ref[...]
    ob = (hn * _sigmoid(mo_ref[...])).astype(BF16)
    o = jnp.dot(oa_ref[...], wout_ref[:H_A * V_A, :], preferred_element_type=F32) \
        + jnp.dot(ob, wout_ref[H_A * V_A:, :], preferred_element_type=F32)
    x = jnp.where(pl.program_id(0) < npt, xp_ref[...], xs_ref[...])
    x1 = x + mod_ref[2:3, :] * o
    x1_ref[...] = x1
    _moe_route(x1, mod_ref, n2g_ref[...], wr_ref, br_ref, xc_ref, route_ref, cnt_ref)


def _conv_in_kernel(rows_ref, x1_ref, route_ref, modp_ref, mod_ref, n1g_ref, w_ref, x2_ref, u_ref):
    x2 = x1_ref[...] + modp_ref[5:6, :] * _combine(route_ref[...], rows_ref[...])
    x2_ref[...] = x2
    h = _rms(x2, n1g_ref[...]) * (1.0 + mod_ref[1:2, :]) + mod_ref[0:1, :]
    z = jnp.dot(h.astype(BF16), w_ref[...], preferred_element_type=F32)
    u_ref[...] = z[:, :D] * _sigmoid(z[:, D:])


HALO = 16


def _conv_out_kernel(npt, tiles_per_seq, u_ref, up_ref, un_ref, wdw_ref, bdw_ref, lg_ref, lb_ref, w2_ref,
                     x_ref, mod_ref, n2g_ref, wr_ref, br_ref, x1_ref, xc_ref, route_ref, cnt_ref, ext_ref, sh_ref):
    i = pl.program_id(0)
    j = i - npt
    in_prompt = i < npt
    first = in_prompt | (j % tiles_per_seq == 0)
    last = in_prompt | (j % tiles_per_seq == tiles_per_seq - 1)
    ext_ref[0:HALO, :] = jnp.where(first, 0.0, up_ref[...])
    ext_ref[HALO:HALO + TM, :] = u_ref[...]
    ext_ref[HALO + TM:, :] = jnp.where(last, 0.0, un_ref[...])

    for b in range(SUBLANES):
        sh_ref[b] = ext_ref[b + 1:b + 1 + TM + 24, :]
    acc = jnp.zeros((TM, D), F32) + bdw_ref[...]
    for tap in range(CONV_K):
        a, b = divmod(tap, SUBLANES)
        acc = acc + wdw_ref[tap:tap + 1, :] * sh_ref[b, SUBLANES * a:SUBLANES * a + TM, :]
    mu = jnp.mean(acc, axis=-1, keepdims=True)
    var = jnp.mean(jnp.square(acc - mu), axis=-1, keepdims=True)
    ln = (acc - mu) * lax.rsqrt(var + 1e-5) * lg_ref[...] + lb_ref[...]
    act = (ln * _sigmoid(ln)).astype(BF16)
    o = jnp.dot(act, w2_ref[...], preferred_element_type=F32)
    x1 = x_ref[...] + mod_ref[2:3, :] * o
    x1_ref[...] = x1
    _moe_route(x1, mod_ref, n2g_ref[...], wr_ref, br_ref, xc_ref, route_ref, cnt_ref)


CPB = FB // CH


def _ffn_kernel(be_ref, nused_ref, nval_ref, src_ref, xc_in, win_ref, bin_ref, wout_ref, bout_ref, rows_hbm,
                win_s, wout_s, xbuf, ybuf, rsem, wsem):
    del xc_in
    i = pl.program_id(0)
    last = pl.num_programs(0) - 1
    e = be_ref[i]
    e_prev = be_ref[jnp.maximum(i - 1, 0)]

    def copy_in(blk, c):
        row = pl.multiple_of(src_ref[blk * CPB + c], CH)
        return pltpu.make_async_copy(rows_hbm.at[pl.ds(row, CH), :],
                                     xbuf.at[blk % 2, pl.ds(pl.multiple_of(c * CH, CH), CH), :], rsem.at[blk % 2])

    def copy_out(blk, c):
        row = pl.multiple_of(src_ref[blk * CPB + c], CH)
        return pltpu.make_async_copy(ybuf.at[blk % 2, pl.ds(pl.multiple_of(c * CH, CH), CH), :],
                                     rows_hbm.at[pl.ds(row, CH), :], wsem.at[blk % 2])

    def for_chunks(blk, fn):
        nv = nval_ref[blk]

        @pl.when(nv == CPB)
        def _():
            for c in range(CPB):
                fn(c)

        @pl.when(nv < CPB)
        def _():
            def body(c, carry):
                fn(c)
                return carry

            lax.fori_loop(0, nv, body, 0)

    @pl.when(i == 0)
    def _():
        xbuf[...] = jnp.zeros_like(xbuf)
        for_chunks(i, lambda c: copy_in(i, c).start())

    @pl.when(i < last)
    def _():
        for_chunks(i + 1, lambda c: copy_in(i + 1, c).start())

    @pl.when((i == 0) | (e != e_prev))
    def _():
        win_s[...] = win_ref[...].astype(BF16)
        wout_s[...] = wout_ref[...].astype(BF16)

    @pl.when(i >= 2)
    def _():
        for_chunks(i - 2, lambda c: copy_out(i - 2, c).wait())

    @pl.when(i < nused_ref[0])
    def _():
        for_chunks(i, lambda c: copy_in(i, c).wait())
        xa, xb = _unpack_pairs(xbuf[i % 2])
        hb = jnp.dot(xa, win_s[:D // 2, :], preferred_element_type=F32) \
            + jnp.dot(xb, win_s[D // 2:, :], preferred_element_type=F32) + bin_ref[...]
        g = jnp.minimum(hb[:, :D_FF], SWIGLU_LIMIT)
        u = jnp.clip(hb[:, D_FF:], -SWIGLU_LIMIT, SWIGLU_LIMIT)
        act = g * _sigmoid(SWIGLU_ALPHA * g) * (u + 1.0)
        y = jnp.dot(act.astype(BF16), wout_s[...], preferred_element_type=F32) + bout_ref[...]
        ybuf[i % 2] = _pack_pairs(y.astype(BF16).astype(F32))
        for_chunks(i, lambda c: copy_out(i, c).start())

    @pl.when(i == last)
    def _():
        @pl.when(i >= 1)
        def _():
            for_chunks(i - 1, lambda c: copy_out(i - 1, c).wait())

        for_chunks(i, lambda c: copy_out(i, c).wait())


def _expert_ffn(block_e, n_used, n_valid, src, xc, layer, w_in, b_in, w_out, b_out):
    n_blocks = block_e.shape[0]
    depth = w_in.shape[0]
    wspec = lambda shape: pl.BlockSpec((None, None) + shape, lambda i, be, *_: (layer, be[i], 0, 0))
    return pl.pallas_call(
        _ffn_kernel,
        grid_spec=pltpu.PrefetchScalarGridSpec(
            num_scalar_prefetch=4,
            grid=(n_blocks,),
            in_specs=[pl.BlockSpec(memory_space=pl.ANY), wspec((D, 2 * D_FF)), wspec((1, 2 * D_FF)),
                      wspec((D_FF, D)), wspec((1, D))],
            out_specs=pl.BlockSpec(memory_space=pl.ANY),
            scratch_shapes=[pltpu.VMEM((D, 2 * D_FF), BF16), pltpu.VMEM((D_FF, D), BF16),
                            pltpu.VMEM((2, FB, D // 2), U32), pltpu.VMEM((2, FB, D // 2), U32),
                            pltpu.SemaphoreType.DMA((2,)), pltpu.SemaphoreType.DMA((2,))],
        ),
        out_shape=jax.ShapeDtypeStruct(xc.shape, U32),
        input_output_aliases={4: 0},
        compiler_params=_cparams(("arbitrary",)),
        name="moe_expert_ffn",
    )(block_e, n_used, n_valid, src, xc, w_in, b_in.reshape(depth, N_EXPERTS, 1, 2 * D_FF), w_out,
      b_out.reshape(depth, N_EXPERTS, 1, D))


def _moe_plan(cnt):
    nt = cnt.shape[0]
    n_blocks = (nt * TM * TOP_K + nt * N_EXPERTS * (CH - 1)) // FB + N_EXPERTS + 1
    n = cnt[:, 0, :N_EXPERTS].astype(I32)
    nch = (n + CH - 1) // CH
    cend = jnp.cumsum(nch, axis=0)
    ce = cend[-1]
    pce = (ce + CPB - 1) // CPB * CPB
    e_end = jnp.cumsum(pce)
    e_start = e_end - pce
    first_slot = CH * (jnp.cumsum(nch, axis=1) - nch)

    blk0 = jnp.arange(n_blocks, dtype=I32) * CPB
    block_e = jnp.minimum(jnp.sum((blk0[:, None] >= e_end[None, :]).astype(I32), axis=-1), N_EXPERTS - 1)
    pick = lambda table, idx: jnp.sum(jnp.where(idx[:, None] == jnp.arange(table.shape[0], dtype=I32), table, 0), axis=-1)
    n_valid = jnp.clip(pick(e_start + ce, block_e) - blk0, 0, CPB)
    n_used = (e_end[-1] // CPB).reshape(1)

    d = jnp.arange(n_blocks * CPB, dtype=I32)
    e_of = jnp.repeat(block_e, CPB)
    local = d - pick(e_start, e_of)
    onehot_e = (e_of[:, None] == jnp.arange(N_EXPERTS, dtype=I32)).astype(F32)
    by_expert = lambda m: jnp.dot(onehot_e, m.T.astype(F32), precision=lax.Precision.HIGHEST).astype(I32)
    cend_e = by_expert(cend)
    in_tile = (local[:, None] >= cend_e - by_expert(nch)) & (local[:, None] < cend_e)
    tile_iota = jnp.arange(nt, dtype=I32)
    at_tile = lambda m: jnp.sum(jnp.where(in_tile, m, 0), axis=-1)
    src = at_tile(tile_iota * SLOTS + by_expert(first_slot) + CH * (local[:, None] - (cend_e - by_expert(nch))))
    return src.astype(I32), block_e.astype(I32), n_valid.astype(I32), n_used.astype(I32)


def _moe(cnt, xc, layer, w_in, b_in, w_out, b_out):
    src, block_e, n_valid, n_used = _moe_plan(cnt)
    return _expert_ffn(block_e, n_used, n_valid, src, xc, layer, w_in, b_in, w_out, b_out)


def _final_kernel(npt, rows_ref, x1_ref, route_ref, modp_ref, g_ref, op_ref, os_ref):
    i = pl.program_id(0)
    x2 = x1_ref[...] + modp_ref[5:6, :] * _combine(route_ref[...], rows_ref[...])
    out = _rms(x2, g_ref[...])

    @pl.when(i < npt)
    def _():
        op_ref[...] = out

    @pl.when(i >= npt)
    def _():
        os_ref[...] = out


def _rope_tables(n_tokens):
    rows = n_tokens // GRID_W
    pos_row = jnp.repeat(jnp.arange(rows, dtype=F32), GRID_W)
    pos_col = jnp.tile(jnp.arange(GRID_W, dtype=F32), rows)
    n_freq = ROPE_DIM // 4
    inv_freq = ROPE_BASE ** (-jnp.arange(n_freq, dtype=F32) / n_freq)
    ang = jnp.stack([pos_row[:, None] * inv_freq, pos_col[:, None] * inv_freq], axis=1)
    cos, sin = jnp.cos(ang), jnp.sin(ang)
    zero = jnp.zeros_like(sin)
    c = jnp.stack([cos, cos], axis=2).reshape(n_tokens, ROPE_DIM)
    sm = jnp.stack([-sin, zero], axis=2).reshape(n_tokens, ROPE_DIM)
    sp = jnp.stack([zero, sin], axis=2).reshape(n_tokens, ROPE_DIM)

    def place(a, fill):
        left = jnp.full((n_tokens, ROPE_LANE0), fill, F32)
        right = jnp.full((n_tokens, LANES - ROPE_LANE0 - ROPE_DIM), fill, F32)
        body = jnp.concatenate([left, a, right], axis=1)
        ident = jnp.full((TM, LANES), fill, F32)
        return jnp.concatenate([body, ident], axis=0)

    return jnp.stack([place(c, 1.0), place(sm, 0.0), place(sp, 0.0)], axis=0)


def kernel(x_prompt, x_sample, c, cache_mla_ckv, cache_mla_krope, state_mlstm_C, state_mlstm_n, state_mlstm_m, c_ctx, ada_w, ada_b, norm1_g, norm2_g, ab_w_in, mla_q_norm_g, mla_w_uq, mla_kv_norm_g, mla_w_ukv, mlstm_gate_b, mlstm_norm_g, ab_w_out, conv_w_pw1, conv_w_dw, conv_b_dw, conv_ln_g, conv_ln_b, conv_w_pw2, router_w, router_b, moe_w_in, moe_b_in, moe_w_out, moe_b_out, final_norm_g):
    bp, sp_, _ = x_prompt.shape
    bs, ss, _ = x_sample.shape
    past = cache_mla_ckv.shape[2]
    tp, ts = bp * sp_, bs * ss
    t = tp + ts
    nt = t // TM
    npt = tp // TM
    tps = ss // TM
    assert sp_ == TM and ss % TM == 0 and bs <= CTX_ROW and tp % ss == 0

    xp2 = x_prompt.reshape(tp, D)
    xs2 = x_sample.reshape(ts, D)
    cond = jnp.zeros((N_COND, D), F32).at[:bs].set(c).at[CTX_ROW].set(c_ctx)
    mod = _modulation(cond, ada_w, ada_b)

    def mod_row(i):
        return jnp.where(i < npt, CTX_ROW, (i - npt) // tps)

    def mod_spec(layer):
        return pl.BlockSpec((None, None, 6, D), lambda i, *_: (layer, mod_row(i), 0, 0))

    row_spec = lambda w: pl.BlockSpec((TM, w), lambda i, *_: (i, 0))
    prompt_spec = pl.BlockSpec((TM, D), lambda i, *_: (jnp.minimum(i, npt - 1), 0))
    sample_spec = pl.BlockSpec((TM, D), lambda i, *_: (jnp.maximum(i - npt, 0), 0))
    full = lambda a: pl.BlockSpec(a.shape, lambda i, *_: (0,) * a.ndim)
    vec = lambda a: a.reshape(1, -1)
    any_spec = pl.BlockSpec(memory_space=pl.ANY)

    def router_args(layer):
        wr = jnp.pad(router_w[layer], ((0, 0), (0, LANES - N_EXPERTS))).astype(BF16)
        br = jnp.pad(router_b[layer], (0, LANES - N_EXPERTS)).reshape(1, LANES)
        return wr, br

    moe_out_shapes = [
        jax.ShapeDtypeStruct((t, D), F32),
        jax.ShapeDtypeStruct((nt * SLOTS, D // 2), U32),
        jax.ShapeDtypeStruct((t, LANES), F32),
        jax.ShapeDtypeStruct((nt, SUBLANES, LANES), F32),
    ]
    moe_out_specs = [row_spec(D), pl.BlockSpec((SLOTS, D // 2), lambda i: (i, 0)), row_spec(LANES),
                     pl.BlockSpec((None, SUBLANES, LANES), lambda i: (i, 0, 0))]

    j = 0
    w = ab_w_in[j]
    zc = lambda n: jnp.zeros((D, n), F32)
    misc_w = jnp.concatenate([w[:, 2464:2480], zc(ROPE_LANE0 - 16), w[:, 384:416], zc(LANES - ROPE_LANE0 - ROPE_DIM)], axis=1)
    w_in = jnp.concatenate([w[:, :384], misc_w, w[:, 416:2464]], axis=1).astype(BF16)
    wq = jnp.pad(mla_w_uq[j].reshape(Q_RANK, H_A, NOPE + ROPE_DIM), ((0, 0), (0, 0), (0, HEAD_PAD - NOPE - ROPE_DIM)))
    wq = wq.reshape(Q_RANK, H_A * HEAD_PAD).astype(BF16)
    wkv = mla_w_ukv[j].reshape(KV_RANK, H_A, NOPE + V_A)
    wk = jnp.pad(wkv[:, :, :NOPE], ((0, 0), (0, 0), (0, HEAD_PAD - NOPE))).reshape(KV_RANK, H_A * HEAD_PAD).astype(BF16)
    wv = jnp.pad(wkv[:, :, NOPE:], ((0, 0), (0, 0), (0, HEAD_PAD - V_A))).reshape(KV_RANK, H_A * HEAD_PAD).astype(BF16)
    gate_b = jnp.pad(mlstm_gate_b[j], (0, LANES - 4 * H_B)).reshape(1, LANES)
    tabs = _rope_tables(ss)
    pos_blocks = ss // TM

    def tab_block(i):
        return jnp.where(i < npt, pos_blocks, (i - npt) % tps)

    ins = [xp2, xs2, mod, vec(norm1_g[0]), w_in, vec(mla_q_norm_g[j]), wq, vec(mla_kv_norm_g[j]), wk, wv, gate_b, tabs]
    in_specs = [prompt_spec, sample_spec, mod_spec(0)] + [full(a) for a in ins[3:11]] + [
        pl.BlockSpec((3, TM, LANES), lambda i: (0, tab_block(i), 0))]
    q, k, v, ckv, misc, mqkv, mo = pl.pallas_call(
        functools.partial(_ab_in_kernel, npt),
        grid=(nt,),
        in_specs=in_specs,
        out_specs=[row_spec(H_A * HEAD_PAD), row_spec(H_A * HEAD_PAD), row_spec(H_A * HEAD_PAD), row_spec(KV_RANK),
                   row_spec(LANES), row_spec(3 * MB), row_spec(MB)],
        out_shape=[
            jax.ShapeDtypeStruct((t, H_A * HEAD_PAD), BF16), jax.ShapeDtypeStruct((t, H_A * HEAD_PAD), BF16),
            jax.ShapeDtypeStruct((t, H_A * HEAD_PAD), BF16), jax.ShapeDtypeStruct((t, KV_RANK), F32),
            jax.ShapeDtypeStruct((t, LANES), F32), jax.ShapeDtypeStruct((t, 3 * MB), BF16),
            jax.ShapeDtypeStruct((t, MB), F32)],
        compiler_params=_cparams(("arbitrary",)),
        name="ab_in_proj",
    )(*ins)

    nctx = bs * past
    ctx_ckv = cache_mla_ckv[:, j].reshape(nctx, KV_RANK)
    ctx_kr = jnp.pad(cache_mla_krope[:, j].reshape(nctx, ROPE_DIM), ((0, 0), (ROPE_LANE0, LANES - ROPE_LANE0 - ROPE_DIM)))
    k_ctx, v_ctx = pl.pallas_call(
        _ctx_kv_kernel,
        grid=(nctx // TM,),
        in_specs=[row_spec(KV_RANK), row_spec(LANES), full(wk), full(wv)],
        out_specs=[row_spec(H_A * HEAD_PAD), row_spec(H_A * HEAD_PAD)],
        out_shape=[jax.ShapeDtypeStruct((nctx, H_A * HEAD_PAD), BF16)] * 2,
        compiler_params=_cparams(("arbitrary",)),
        name="ctx_kv",
    )(ctx_ckv, ctx_kr, wk, wv)

    o_a = _attention(q, k, v, 0, bp, sp_, sp_, jnp.zeros((t, H_A * V_A), BF16))
    seq_keys = lambda ctx, own: jnp.concatenate(
        [ctx.reshape(bs, past, -1), own[tp:].reshape(bs, ss, -1)], axis=1).reshape(bs * (past + ss), -1)
    o_a = _attention(q, seq_keys(k_ctx, k), seq_keys(v_ctx, v), tp, bs, ss, past + ss, o_a)

    zc0 = jnp.zeros((bp, 2, H_B, DH_B, DH_B), F32)
    zn0 = jnp.zeros((bp, 2, H_B, 1, DH_B), F32)
    zm0 = jnp.zeros((bp, 2, H_B, 1, LANES), F32)
    h_f, h_b, c_p, n_p, m_p = _mlstm(mqkv, misc, zc0, zn0, zm0, 0, bp, sp_,
                                     (jnp.zeros((t, MB), F32), jnp.zeros((t, MB), F32)))
    sm0 = jnp.broadcast_to(state_mlstm_m[:, j][..., None, None], (bs, 2, H_B, 1, LANES))
    h_f, h_b, _, _, _ = _mlstm(mqkv, misc, state_mlstm_C[:, j], state_mlstm_n[:, j][:, :, :, None, :], sm0, tp, bs, ss,
                               (h_f, h_b))

    wr, br = router_args(0)
    ins = [o_a, h_f, h_b, mo, vec(mlstm_norm_g[j]), ab_w_out[j].astype(BF16), xp2, xs2, mod, vec(norm2_g[0]), wr, br]
    in_specs = [row_spec(H_A * V_A), row_spec(MB), row_spec(MB), row_spec(MB), full(ins[4]), full(ins[5]),
                prompt_spec, sample_spec, mod_spec(0), full(ins[9]), full(wr), full(br)]
    x1, xc, route, cnt = pl.pallas_call(
        functools.partial(_ab_out_kernel, npt),
        grid=(nt,),
        in_specs=in_specs,
        out_specs=moe_out_specs,
        out_shape=moe_out_shapes,
        compiler_params=_cparams(("arbitrary",)),
        name="ab_out_proj",
    )(*ins)
    ys = _moe(cnt, xc, 0, moe_w_in, moe_b_in, moe_w_out, moe_b_out)

    slots_spec = pl.BlockSpec((SLOTS, D // 2), lambda i, *_: (i, 0))
    ins = [ys, x1, route, mod, mod, vec(norm1_g[1]), conv_w_pw1[0].astype(BF16)]
    x2, u = pl.pallas_call(
        _conv_in_kernel,
        grid=(nt,),
        in_specs=[slots_spec, row_spec(D), row_spec(LANES), mod_spec(0), mod_spec(1), full(ins[5]), full(ins[6])],
        out_specs=[row_spec(D), row_spec(D)],
        out_shape=[jax.ShapeDtypeStruct((t, D), F32), jax.ShapeDtypeStruct((t, D), F32)],
        compiler_params=_cparams(("arbitrary",)),
        name="conv_in_proj",
    )(*ins)

    hpt = TM // HALO
    nhalo = t // HALO
    wr, br = router_args(1)
    ins = [u, u, u, conv_w_dw[0], vec(conv_b_dw[0]), vec(conv_ln_g[0]), vec(conv_ln_b[0]), conv_w_pw2[0].astype(BF16),
           x2, mod, vec(norm2_g[1]), wr, br]
    in_specs = [row_spec(D),
                pl.BlockSpec((HALO, D), lambda i: (jnp.maximum(i * hpt - 1, 0), 0)),
                pl.BlockSpec((HALO, D), lambda i: (jnp.minimum((i + 1) * hpt, nhalo - 1), 0)),
                full(ins[3]), full(ins[4]), full(ins[5]), full(ins[6]), full(ins[7]), row_spec(D), mod_spec(1),
                full(ins[10]), full(wr), full(br)]
    x1, xc, route, cnt = pl.pallas_call(
        functools.partial(_conv_out_kernel, npt, tps),
        grid=(nt,),
        in_specs=in_specs,
        out_specs=moe_out_specs,
        out_shape=moe_out_shapes,
        scratch_shapes=[pltpu.VMEM((TM + 2 * HALO, D), F32), pltpu.VMEM((SUBLANES, TM + 24, D), F32)],
        compiler_params=_cparams(("arbitrary",)),
        name="conv_out_proj",
    )(*ins)
    ys = _moe(cnt, xc, 1, moe_w_in, moe_b_in, moe_w_out, moe_b_out)

    ins = [ys, x1, route, mod, vec(final_norm_g)]
    y_p, y_s = pl.pallas_call(
        functools.partial(_final_kernel, npt),
        grid=(nt,),
        in_specs=[slots_spec, row_spec(D), row_spec(LANES), mod_spec(1), full(ins[4])],
        out_specs=[prompt_spec, sample_spec],
        out_shape=[jax.ShapeDtypeStruct((tp, D), F32), jax.ShapeDtypeStruct((ts, D), F32)],
        compiler_params=_cparams(("arbitrary",)),
        name="final_norm",
    )(*ins)

    y_prompt = y_p.reshape(bp, sp_, D)
    y_sample = y_s.reshape(bs, ss, D)
    new_ckv = ckv[:tp].reshape(bp, 1, sp_, KV_RANK)
    new_krope = misc[:tp, ROPE_LANE0:ROPE_LANE0 + ROPE_DIM].reshape(bp, 1, sp_, ROPE_DIM)
    new_c = c_p[:, None]
    new_n = n_p[:, None, :, :, 0, :]
    new_m = m_p[:, None, :, :, 0, 0]
    return (y_prompt, y_sample, new_ckv, new_krope, new_c, new_n, new_m)
```

```python
import functools
import math

import jax
import jax.numpy as jnp
from jax import lax
from jax.experimental import pallas as pl
from jax.experimental.pallas import tpu as pltpu

F32 = jnp.float32
BF16 = jnp.bfloat16
I32 = jnp.int32
U32 = jnp.uint32

D = 1024
GRID_W = 64
H_A = 8
Q_RANK = 256
KV_RANK = 128
NOPE = 64
ROPE_DIM = 32
V_A = 64
ROPE_BASE = 10000.0
H_B = 4
DH_B = 128
MB = H_B * DH_B
CONV_K = 31
N_EXPERTS = 32
TOP_K = 4
D_FF = 1024
SWIGLU_LIMIT = 7.0
SWIGLU_ALPHA = 1.702

LANES = 128
SUBLANES = 8
HEAD_PAD = 128
ROPE_LANE0 = 64
N_COND = 16
CTX_ROW = 8

TM = 256
FB = 512
LC = 128
TQ = 256
HPS = 4
CH = SUBLANES
SLOTS = TOP_K * TM + N_EXPERTS * CH
VMEM_LIMIT = 56 * 1024 * 1024

NEG_INF = float("-inf")
LOG2_E = math.log2(math.e)


def _cparams(sem):
    return pltpu.CompilerParams(dimension_semantics=sem, vmem_limit_bytes=VMEM_LIMIT)


def _sigmoid(x):
    return 1.0 / (1.0 + jnp.exp(-x))


def _rms(x, g, eps=1e-6):
    return x * lax.rsqrt(jnp.mean(x * x, axis=-1, keepdims=True) + eps) * g


def _lane(shape):
    return lax.broadcasted_iota(I32, shape, len(shape) - 1)


def _pack_pairs(x):
    w = x.shape[1] // 2
    return pltpu.bitcast(x[:, :w], U32) | (pltpu.bitcast(x[:, w:], U32) >> 16)


def _unpack_pairs(wd):
    a = pltpu.bitcast(wd & jnp.uint32(0xFFFF0000), F32).astype(BF16)
    b = pltpu.bitcast(wd << 16, F32).astype(BF16)
    return a, b


def _mod_kernel(cond_ref, w_ref, b_ref, o_ref):
    c = cond_ref[...]
    s = (c * _sigmoid(c)).astype(BF16)
    o_ref[...] = jnp.dot(s, w_ref[...].astype(BF16), preferred_element_type=F32) + b_ref[...]


def _modulation(cond, ada_w, ada_b):
    depth = ada_w.shape[0]
    nj = ada_w.shape[2] // D
    out = pl.pallas_call(
        _mod_kernel,
        grid=(depth, nj),
        in_specs=[
            pl.BlockSpec((N_COND, D), lambda l, j: (0, 0)),
            pl.BlockSpec((None, D, D), lambda l, j: (l, 0, j)),
            pl.BlockSpec((None, 1, D), lambda l, j: (l, 0, j)),
        ],
        out_specs=pl.BlockSpec((None, N_COND, D), lambda l, j: (l, 0, j)),
        out_shape=jax.ShapeDtypeStruct((depth, N_COND, nj * D), F32),
        compiler_params=_cparams(("arbitrary", "arbitrary")),
        name="ada_modulation",
    )(cond, ada_w, ada_b.reshape(depth, 1, nj * D))
    return out.reshape(depth, N_COND, nj, D)


def _combine(route, rows):
    lane = _lane((route.shape[0], SLOTS)).astype(F32)
    g = jnp.where(lane == route[:, 8:9], route[:, 4:5], 0.0)
    for k in range(1, TOP_K):
        g = g + jnp.where(lane == route[:, 8 + k:9 + k], route[:, 4 + k:5 + k], 0.0)
    gb = g.astype(BF16)
    ya, yb = _unpack_pairs(rows)
    return jnp.concatenate([jnp.dot(gb, ya, preferred_element_type=F32),
                            jnp.dot(gb, yb, preferred_element_type=F32)], axis=-1)


def _moe_route(x1, mod_ref, n2g, wr_ref, br_ref, xc_ref, route_ref, cnt_ref):
    h2 = _rms(x1, n2g) * (1.0 + mod_ref[4:5, :]) + mod_ref[3:4, :]
    h2b = h2.astype(BF16)
    logits = jnp.dot(h2b, wr_ref[...], preferred_element_type=F32) + br_ref[...]
    tm = logits.shape[0]
    lane = _lane((tm, LANES))
    lanef = lane.astype(F32)
    l = jnp.where(lane < N_EXPERTS, logits, NEG_INF)
    tops, idxs, sels = [], [], []
    for _ in range(TOP_K):
        mv = jnp.max(l, axis=-1, keepdims=True)
        idx = jnp.min(jnp.where(l == mv, lanef, float(LANES)), axis=-1, keepdims=True)
        sel = lanef == idx
        l = jnp.where(sel, NEG_INF, l)
        tops.append(mv)
        idxs.append(idx)
        sels.append(sel)
    ex = [jnp.exp(t - tops[0]) for t in tops]
    den = ex[0] + ex[1] + ex[2] + ex[3]
    gates = [e / den for e in ex]

    onehot = jnp.zeros((tm, LANES), F32)
    for sel in sels:
        onehot = onehot + jnp.where(sel, 1.0, 0.0)
    row = lax.broadcasted_iota(I32, (tm, tm), 0)
    col = lax.broadcasted_iota(I32, (tm, tm), 1)
    strict = jnp.where(col < row, 1.0, 0.0).astype(BF16)
    before = jnp.dot(strict, onehot.astype(BF16), preferred_element_type=F32)
    n_e = jnp.sum(onehot, axis=0, keepdims=True)
    cnt_ref[...] = jnp.broadcast_to(n_e, cnt_ref.shape)
    nch = jnp.right_shift(n_e.astype(I32) + (CH - 1), CH.bit_length() - 1).astype(F32)
    r128 = lax.broadcasted_iota(I32, (LANES, LANES), 0)
    c128 = lax.broadcasted_iota(I32, (LANES, LANES), 1)
    upper = jnp.where(r128 < c128, 1.0, 0.0).astype(BF16)
    seg0 = float(CH) * jnp.dot(jnp.broadcast_to(nch, (SUBLANES, LANES)).astype(BF16), upper,
                               preferred_element_type=F32)[0:1, :]
    slot_of = seg0 + before
    slots = [jnp.sum(jnp.where(sel, slot_of, 0.0), axis=-1, keepdims=True) for sel in sels]

    r = jnp.zeros((tm, LANES), F32)
    for j, colv in enumerate(idxs + gates + slots):
        r = jnp.where(lane == j, colv, r)
    route_ref[...] = r

    lane_s = _lane((tm, SLOTS)).astype(F32)
    pt = jnp.where(lane_s == slots[0], 1.0, 0.0)
    for k in range(1, TOP_K):
        pt = pt + jnp.where(lane_s == slots[k], 1.0, 0.0)
    xc = jnp.dot(pt.T.astype(BF16), h2b, preferred_element_type=F32)
    xc_ref[...] = _pack_pairs(xc)


def _values_with_ones(ckvb, wv_ref):
    vv = jnp.dot(ckvb, wv_ref[...], preferred_element_type=F32)
    return jnp.where((_lane(vv.shape) & (HEAD_PAD - 1)) == V_A, 1.0, vv).astype(BF16)


def _rope(x, tab_ref):
    return x * tab_ref[0] + pltpu.roll(x, LANES - 8, 1) * tab_ref[1] + pltpu.roll(x, 8, 1) * tab_ref[2]


def _ab_in_kernel(npt, xp_ref, xs_ref, mod_ref, n1g_ref, win_ref, qg_ref, wq_ref, kvg_ref, wk_ref, wv_ref, gb_ref,
                  tab_ref, q_ref, k_ref, v_ref, ckv_ref, misc_ref, mqkv_ref, mo_ref):
    x = jnp.where(pl.program_id(0) < npt, xp_ref[...], xs_ref[...])
    h = _rms(x, n1g_ref[...]) * (1.0 + mod_ref[1:2, :]) + mod_ref[0:1, :]
    z = jnp.dot(h.astype(BF16), win_ref[...], preferred_element_type=F32)

    qn = _rms(z[:, :Q_RANK], qg_ref[...]).astype(BF16)
    q = jnp.dot(qn, wq_ref[...], preferred_element_type=F32)
    ckv = _rms(z[:, Q_RANK:Q_RANK + KV_RANK], kvg_ref[...])
    ckv_ref[...] = ckv
    ckvb = ckv.astype(BF16)
    kn = jnp.dot(ckvb, wk_ref[...], preferred_element_type=F32)
    v_ref[...] = _values_with_ones(ckvb, wv_ref)

    misc = z[:, Q_RANK + KV_RANK:Q_RANK + KV_RANK + LANES]
    misc_ref[...] = misc + gb_ref[...]
    lane = _lane(misc.shape)
    kr = jnp.where((lane >= ROPE_LANE0) & (lane < ROPE_LANE0 + ROPE_DIM), _rope(misc, tab_ref), 0.0)
    scale = float((NOPE + ROPE_DIM) ** -0.5 * LOG2_E)
    for hh in range(H_A):
        sl = slice(hh * HEAD_PAD, (hh + 1) * HEAD_PAD)
        q_ref[:, sl] = (_rope(q[:, sl], tab_ref) * scale).astype(BF16)
        k_ref[:, sl] = (kn[:, sl] + kr).astype(BF16)

    m0 = Q_RANK + KV_RANK + LANES
    mqkv_ref[:, :MB] = z[:, m0:m0 + MB].astype(BF16)
    mqkv_ref[:, MB:2 * MB] = (z[:, m0 + MB:m0 + 2 * MB] * float(DH_B ** -0.5)).astype(BF16)
    mqkv_ref[:, 2 * MB:] = z[:, m0 + 2 * MB:m0 + 3 * MB].astype(BF16)
    mo_ref[...] = z[:, m0 + 3 * MB:m0 + 4 * MB]


def _ctx_kv_kernel(ckv_ref, kr_ref, wk_ref, wv_ref, k_ref, v_ref):
    ckvb = ckv_ref[...].astype(BF16)
    kn = jnp.dot(ckvb, wk_ref[...], preferred_element_type=F32)
    v_ref[...] = _values_with_ones(ckvb, wv_ref)
    kr = kr_ref[...]
    for hh in range(H_A):
        sl = slice(hh * HEAD_PAD, (hh + 1) * HEAD_PAD)
        k_ref[:, sl] = (kn[:, sl] + kr).astype(BF16)


def _attn_kernel(q_ref, k_ref, v_ref, prev_ref, o_ref):
    del prev_ref
    outs = []
    for j in range(HPS):
        sl = slice(j * HEAD_PAD, (j + 1) * HEAD_PAD)
        s = lax.dot_general(q_ref[:, sl], k_ref[:, sl], (((1,), (1,)), ((), ())), preferred_element_type=F32)
        m = jnp.max(s, axis=-1, keepdims=True)
        p = jnp.exp2(s - m).astype(BF16)
        o = jnp.dot(p, v_ref[:, sl], preferred_element_type=F32)
        outs.append(o / o[:, V_A:V_A + 1])
    lane = _lane(outs[0].shape)
    for j in range(0, HPS, 2):
        o_ref[:, j * V_A:(j + 2) * V_A] = jnp.where(lane < V_A, outs[j], pltpu.roll(outs[j + 1], V_A, 1)).astype(BF16)


def _attention(q, k, v, qrow0, nb, s, l, prev):
    t = q.shape[0]
    tq = min(TQ, s)
    nq = s // tq
    qrow = lambda bi, hp, qi: (qrow0 // tq + bi * nq + qi, hp)
    krow = lambda bi, hp, qi: (bi, hp)
    ins = [q, k, v, prev]
    in_specs = [pl.BlockSpec((tq, HPS * HEAD_PAD), qrow), pl.BlockSpec((l, HPS * HEAD_PAD), krow),
                pl.BlockSpec((l, HPS * HEAD_PAD), krow), pl.BlockSpec(memory_space=pl.ANY)]
    aliases = {3: 0}
    return pl.pallas_call(
        _attn_kernel,
        grid=(nb, H_A // HPS, nq),
        in_specs=in_specs,
        out_specs=pl.BlockSpec((tq, HPS * V_A), qrow),
        out_shape=jax.ShapeDtypeStruct((t, H_A * V_A), BF16),
        input_output_aliases=aliases,
        compiler_params=_cparams(("arbitrary", "arbitrary", "arbitrary")),
        name="mla_attention",
    )(*ins)


def _log_sigmoid(x):
    return jnp.minimum(x, 0.0) - jnp.log(1.0 + jnp.exp(-jnp.abs(x)))


def _mlstm_direction(d, q_ref, k_ref, v_ref, g_ref, h_ref, c_s, n_s, m_s):
    gates = g_ref[...]
    lf_all = _log_sigmoid(gates)
    row = lax.broadcasted_iota(I32, (LC, LC), 0)
    col = lax.broadcasted_iota(I32, (LC, LC), 1)
    mask = (col <= row) if d == 0 else (col >= row)
    eye = row == col
    cum = jnp.dot(jnp.where(mask, 1.0, 0.0), lf_all, preferred_element_type=F32,
                  precision=lax.Precision.HIGHEST)

    for hh in range(H_B):
        sl = slice(hh * DH_B, (hh + 1) * DH_B)
        q = q_ref[:, sl]
        k = k_ref[:, sl]
        v = v_ref[:, sl]
        b_col = cum[:, 8 * d + 4 + hh:8 * d + 5 + hh]
        i_col = gates[:, 8 * d + hh:8 * d + hh + 1]
        m_prev = m_s[d, hh][:, 0:1]
        r_row = jnp.sum(jnp.where(eye, i_col - b_col, 0.0), axis=0, keepdims=True)
        dm = jnp.where(mask, b_col + r_row, NEG_INF)
        a_col = b_col + m_prev
        mt = jnp.maximum(a_col, jnp.max(dm, axis=-1, keepdims=True))
        w_inter = jnp.exp(a_col - mt)
        qk = lax.dot_general(q, k, (((1,), (1,)), ((), ())), preferred_element_type=F32)
        s = qk * jnp.exp(dm - mt)
        c_prev = c_s[d, hh]
        n_prev = n_s[d, hh]
        num = w_inter * jnp.dot(q, c_prev.astype(BF16), preferred_element_type=F32) \
            + jnp.dot(s.astype(BF16), v, preferred_element_type=F32)
        qf = q.astype(F32)
        den = w_inter * jnp.sum(qf * n_prev, axis=-1, keepdims=True) + jnp.sum(s, axis=-1, keepdims=True)
        h_ref[:, sl] = num / jnp.maximum(jnp.abs(den), jnp.exp(-mt))

        b_tot = b_col[LC - 1:LC, :] if d == 0 else b_col[0:1, :]
        g_col = b_tot - b_col + i_col
        m_new = jnp.maximum(b_tot + m_prev, jnp.max(g_col, axis=0, keepdims=True))
        w_c = jnp.exp(b_tot + m_prev - m_new)
        kw = k.astype(F32) * jnp.exp(g_col - m_new)
        c_s[d, hh] = w_c * c_prev + jnp.dot(kw.T.astype(BF16), v, preferred_element_type=F32)
        n_s[d, hh] = w_c * n_prev + jnp.sum(kw, axis=0, keepdims=True)
        m_s[d, hh] = jnp.broadcast_to(m_new, (1, LANES))


def _mlstm_kernel(qf_ref, kf_ref, vf_ref, gf_ref, qb_ref, kb_ref, vb_ref, gb_ref, c0_ref, n0_ref, m0_ref,
                  prevf_ref, prevb_ref, hf_ref, hb_ref, c_out, n_out, m_out, c_s, n_s, m_s):
    del prevf_ref, prevb_ref
    c = pl.program_id(1)

    @pl.when(c == 0)
    def _():
        c_s[...] = c0_ref[...]
        n_s[...] = n0_ref[...]
        m_s[...] = m0_ref[...]

    _mlstm_direction(0, qf_ref, kf_ref, vf_ref, gf_ref, hf_ref, c_s, n_s, m_s)
    _mlstm_direction(1, qb_ref, kb_ref, vb_ref, gb_ref, hb_ref, c_s, n_s, m_s)

    @pl.when(c == pl.num_programs(1) - 1)
    def _():
        c_out[...] = c_s[...]
        n_out[...] = n_s[...]
        m_out[...] = m_s[...]


def _mlstm(mqkv, misc, c0, n0, m0, row0, nb, s, prev):
    nc = s // LC
    blk0 = row0 // LC
    t = mqkv.shape[0]
    fwd = lambda bi, ci: blk0 + bi * nc + ci
    bwd = lambda bi, ci: blk0 + bi * nc + nc - 1 - ci
    chunk_specs = lambda rows: [
        pl.BlockSpec((LC, MB), lambda bi, ci: (rows(bi, ci), 0)),
        pl.BlockSpec((LC, MB), lambda bi, ci: (rows(bi, ci), 1)),
        pl.BlockSpec((LC, MB), lambda bi, ci: (rows(bi, ci), 2)),
        pl.BlockSpec((LC, LANES), lambda bi, ci: (rows(bi, ci), 0)),
    ]
    state_spec = lambda shape: pl.BlockSpec((None,) + shape, lambda bi, ci: (bi,) + (0,) * len(shape))
    state_specs = [state_spec((2, H_B, DH_B, DH_B)), state_spec((2, H_B, 1, DH_B)), state_spec((2, H_B, 1, LANES))]
    any_spec = pl.BlockSpec(memory_space=pl.ANY)
    return pl.pallas_call(
        _mlstm_kernel,
        grid=(nb, nc),
        in_specs=chunk_specs(fwd) + chunk_specs(bwd) + state_specs + [any_spec, any_spec],
        out_specs=[
            pl.BlockSpec((LC, MB), lambda bi, ci: (fwd(bi, ci), 0)),
            pl.BlockSpec((LC, MB), lambda bi, ci: (bwd(bi, ci), 0)),
        ] + state_specs,
        out_shape=[
            jax.ShapeDtypeStruct((t, MB), F32),
            jax.ShapeDtypeStruct((t, MB), F32),
            jax.ShapeDtypeStruct((nb, 2, H_B, DH_B, DH_B), F32),
            jax.ShapeDtypeStruct((nb, 2, H_B, 1, DH_B), F32),
            jax.ShapeDtypeStruct((nb, 2, H_B, 1, LANES), F32),
        ],
        scratch_shapes=[
            pltpu.VMEM((2, H_B, DH_B, DH_B), F32),
            pltpu.VMEM((2, H_B, 1, DH_B), F32),
            pltpu.VMEM((2, H_B, 1, LANES), F32),
        ],
        input_output_aliases={11: 0, 12: 1},
        compiler_params=_cparams(("arbitrary", "arbitrary")),
        name="mlstm_scan",
    )(mqkv, mqkv, mqkv, misc, mqkv, mqkv, mqkv, misc, c0, n0, m0, prev[0], prev[1])


def _ab_out_kernel(npt, oa_ref, hf_ref, hb_ref, mo_ref, mg_ref, wout_ref, xp_ref, xs_ref, mod_ref, n2g_ref, wr_ref,
                   br_ref, x1_ref, xc_ref, route_ref, cnt_ref):
    hsum = hf_ref[...] + hb_ref[...]
    parts = []
    for hh in range(H_B):
        hh_ = hsum[:, hh * DH_B:(hh + 1) * DH_B]
        mu = jnp.mean(hh_, axis=-1, keepdims=True)
        var = jnp.mean(jnp.square(hh_ - mu), axis=-1, keepdims=True)
        parts.append((hh_ - mu) * lax.rsqrt(var + 1e-5))
    hn = jnp.concatenate(parts, axis=-1) * mg_ref[...]
    ob = (hn * _sigmoid(mo_ref[...])).astype(BF16)
    o = jnp.dot(oa_ref[...], wout_ref[:H_A * V_A, :], preferred_element_type=F32) \
        + jnp.dot(ob, wout_ref[H_A * V_A:, :], preferred_element_type=F32)
    x = jnp.where(pl.program_id(0) < npt, xp_ref[...], xs_ref[...])
    x1 = x + mod_ref[2:3, :] * o
    x1_ref[...] = x1
    _moe_route(x1, mod_ref, n2g_ref[...], wr_ref, br_ref, xc_ref, route_ref, cnt_ref)


def _conv_in_kernel(rows_ref, x1_ref, route_ref, modp_ref, mod_ref, n1g_ref, w_ref, x2_ref, u_ref):
    x2 = x1_ref[...] + modp_ref[5:6, :] * _combine(route_ref[...], rows_ref[...])
    x2_ref[...] = x2
    h = _rms(x2, n1g_ref[...]) * (1.0 + mod_ref[1:2, :]) + mod_ref[0:1, :]
    z = jnp.dot(h.astype(BF16), w_ref[...], preferred_element_type=F32)
    u_ref[...] = z[:, :D] * _sigmoid(z[:, D:])


HALO = 16


def _conv_out_kernel(npt, tiles_per_seq, u_ref, up_ref, un_ref, wdw_ref, bdw_ref, lg_ref, lb_ref, w2_ref,
                     x_ref, mod_ref, n2g_ref, wr_ref, br_ref, x1_ref, xc_ref, route_ref, cnt_ref, ext_ref, sh_ref):
    i = pl.program_id(0)
    j = i - npt
    in_prompt = i < npt
    first = in_prompt | (j % tiles_per_seq == 0)
    last = in_prompt | (j % tiles_per_seq == tiles_per_seq - 1)
    ext_ref[0:HALO, :] = jnp.where(first, 0.0, up_ref[...])
    ext_ref[HALO:HALO + TM, :] = u_ref[...]
    ext_ref[HALO + TM:, :] = jnp.where(last, 0.0, un_ref[...])

    for b in range(SUBLANES):
        sh_ref[b] = ext_ref[b + 1:b + 1 + TM + 24, :]
    acc = jnp.zeros((TM, D), F32) + bdw_ref[...]
    for tap in range(CONV_K):
        a, b = divmod(tap, SUBLANES)
        acc = acc + wdw_ref[tap:tap + 1, :] * sh_ref[b, SUBLANES * a:SUBLANES * a + TM, :]
    mu = jnp.mean(acc, axis=-1, keepdims=True)
    var = jnp.mean(jnp.square(acc - mu), axis=-1, keepdims=True)
    ln = (acc - mu) * lax.rsqrt(var + 1e-5) * lg_ref[...] + lb_ref[...]
    act = (ln * _sigmoid(ln)).astype(BF16)
    o = jnp.dot(act, w2_ref[...], preferred_element_type=F32)
    x1 = x_ref[...] + mod_ref[2:3, :] * o
    x1_ref[...] = x1
    _moe_route(x1, mod_ref, n2g_ref[...], wr_ref, br_ref, xc_ref, route_ref, cnt_ref)


CPB = FB // CH
FFN_SPLIT = 2


def _ffn_kernel(be_ref, nused_ref, nval_ref, src_ref, xc_in, win_ref, bin_ref, wout_ref, bout_ref, rows_hbm,
                win_s, wout_s, xbuf, ybuf, rsem, wsem):
    del xc_in
    i = pl.program_id(0)
    last = pl.num_programs(0) - 1
    e = be_ref[i]
    e_prev = be_ref[jnp.maximum(i - 1, 0)]

    def copy_in(blk, c):
        row = pl.multiple_of(src_ref[blk * CPB + c], CH)
        return pltpu.make_async_copy(rows_hbm.at[pl.ds(row, CH), :],
                                     xbuf.at[blk % 2, pl.ds(pl.multiple_of(c * CH, CH), CH), :], rsem.at[blk % 2])

    def copy_out(blk, c):
        row = pl.multiple_of(src_ref[blk * CPB + c], CH)
        return pltpu.make_async_copy(ybuf.at[blk % 2, pl.ds(pl.multiple_of(c * CH, CH), CH), :],
                                     rows_hbm.at[pl.ds(row, CH), :], wsem.at[blk % 2])

    def for_chunks(blk, fn):
        nv = nval_ref[blk]

        @pl.when(nv == CPB)
        def _():
            for c in range(CPB):
                fn(c)

        @pl.when(nv < CPB)
        def _():
            def body(c, carry):
                fn(c)
                return carry

            lax.fori_loop(0, nv, body, 0)

    def start_reads(blk):
        for c in range(CPB):
            copy_in(blk, c).start()

    def wait_reads(blk):
        pltpu.make_async_copy(rows_hbm.at[pl.ds(0, FB), :], xbuf.at[blk % 2], rsem.at[blk % 2]).wait()

    @pl.when(i == 0)
    def _():
        start_reads(i)

    @pl.when((i == 0) | (e != e_prev))
    def _():
        win_s[...] = win_ref[...].astype(BF16)
        wout_s[...] = wout_ref[...].astype(BF16)

    @pl.when(i >= 2)
    def _():
        for_chunks(i - 2, lambda c: copy_out(i - 2, c).wait())

    @pl.when(i == nused_ref[0])
    def _():
        wait_reads(i)

    @pl.when(i < nused_ref[0])
    def _():
        wait_reads(i)
        start_reads(i + 1)
        for r0 in range(0, FB, FB // FFN_SPLIT):
            rs = pl.ds(r0, FB // FFN_SPLIT)
            xa, xb = _unpack_pairs(xbuf[i % 2, rs, :])
            hb = jnp.dot(xa, win_s[:D // 2, :], preferred_element_type=F32) \
                + jnp.dot(xb, win_s[D // 2:, :], preferred_element_type=F32) + bin_ref[...]
            g = jnp.minimum(hb[:, :D_FF], SWIGLU_LIMIT)
            u = jnp.clip(hb[:, D_FF:], -SWIGLU_LIMIT, SWIGLU_LIMIT)
            act = g * _sigmoid(SWIGLU_ALPHA * g) * (u + 1.0)
            y = jnp.dot(act.astype(BF16), wout_s[...], preferred_element_type=F32) + bout_ref[...]
            ybuf[i % 2, rs, :] = _pack_pairs(y.astype(BF16).astype(F32))
        for_chunks(i, lambda c: copy_out(i, c).start())

    @pl.when(i == last)
    def _():
        @pl.when(i >= 1)
        def _():
            for_chunks(i - 1, lambda c: copy_out(i - 1, c).wait())

        for_chunks(i, lambda c: copy_out(i, c).wait())


def _expert_ffn(block_e, n_used, n_valid, src, xc, layer, w_in, b_in, w_out, b_out):
    n_blocks = block_e.shape[0]
    depth = w_in.shape[0]
    wspec = lambda shape: pl.BlockSpec((None, None) + shape, lambda i, be, *_: (layer, be[i], 0, 0))
    return pl.pallas_call(
        _ffn_kernel,
        grid_spec=pltpu.PrefetchScalarGridSpec(
            num_scalar_prefetch=4,
            grid=(n_blocks,),
            in_specs=[pl.BlockSpec(memory_space=pl.ANY), wspec((D, 2 * D_FF)), wspec((1, 2 * D_FF)),
                      wspec((D_FF, D)), wspec((1, D))],
            out_specs=pl.BlockSpec(memory_space=pl.ANY),
            scratch_shapes=[pltpu.VMEM((D, 2 * D_FF), BF16), pltpu.VMEM((D_FF, D), BF16),
                            pltpu.VMEM((2, FB, D // 2), U32), pltpu.VMEM((2, FB, D // 2), U32),
                            pltpu.SemaphoreType.DMA((2,)), pltpu.SemaphoreType.DMA((2,))],
        ),
        out_shape=jax.ShapeDtypeStruct(xc.shape, U32),
        input_output_aliases={4: 0},
        compiler_params=_cparams(("arbitrary",)),
        name="moe_expert_ffn",
    )(block_e, n_used, n_valid, src, xc, w_in, b_in.reshape(depth, N_EXPERTS, 1, 2 * D_FF), w_out,
      b_out.reshape(depth, N_EXPERTS, 1, D))


def _moe_plan(cnt):
    nt = cnt.shape[0]
    n_blocks = (nt * TM * TOP_K + nt * N_EXPERTS * (CH - 1)) // FB + N_EXPERTS + 2
    n = cnt[:, 0, :N_EXPERTS].astype(I32)
    nch = (n + CH - 1) // CH
    cend = jnp.cumsum(nch, axis=0)
    ce = cend[-1]
    pce = (ce + CPB - 1) // CPB * CPB
    e_end = jnp.cumsum(pce)
    e_start = e_end - pce
    first_slot = CH * (jnp.cumsum(nch, axis=1) - nch)

    blk0 = jnp.arange(n_blocks, dtype=I32) * CPB
    block_e = jnp.minimum(jnp.sum((blk0[:, None] >= e_end[None, :]).astype(I32), axis=-1), N_EXPERTS - 1)
    pick = lambda table, idx: jnp.sum(jnp.where(idx[:, None] == jnp.arange(table.shape[0], dtype=I32), table, 0), axis=-1)
    n_valid = jnp.clip(pick(e_start + ce, block_e) - blk0, 0, CPB)
    n_used = (e_end[-1] // CPB).reshape(1)

    d = jnp.arange(n_blocks * CPB, dtype=I32)
    e_of = jnp.repeat(block_e, CPB)
    local = d - pick(e_start, e_of)
    onehot_e = (e_of[:, None] == jnp.arange(N_EXPERTS, dtype=I32)).astype(F32)
    by_expert = lambda m: jnp.dot(onehot_e, m.T.astype(F32), precision=lax.Precision.HIGHEST).astype(I32)
    cend_e = by_expert(cend)
    in_tile = (local[:, None] >= cend_e - by_expert(nch)) & (local[:, None] < cend_e)
    tile_iota = jnp.arange(nt, dtype=I32)
    at_tile = lambda m: jnp.sum(jnp.where(in_tile, m, 0), axis=-1)
    src = at_tile(tile_iota * SLOTS + by_expert(first_slot) + CH * (local[:, None] - (cend_e - by_expert(nch))))
    src = jnp.where(jnp.any(in_tile, axis=-1), src, SLOTS - CH)
    return src.astype(I32), block_e.astype(I32), n_valid.astype(I32), n_used.astype(I32)


def _moe(cnt, xc, layer, w_in, b_in, w_out, b_out):
    src, block_e, n_valid, n_used = _moe_plan(cnt)
    return _expert_ffn(block_e, n_used, n_valid, src, xc, layer, w_in, b_in, w_out, b_out)


def _final_kernel(npt, rows_ref, x1_ref, route_ref, modp_ref, g_ref, op_ref, os_ref):
    i = pl.program_id(0)
    x2 = x1_ref[...] + modp_ref[5:6, :] * _combine(route_ref[...], rows_ref[...])
    out = _rms(x2, g_ref[...])

    @pl.when(i < npt)
    def _():
        op_ref[...] = out

    @pl.when(i >= npt)
    def _():
        os_ref[...] = out


def _rope_tables(n_tokens):
    rows = n_tokens // GRID_W
    pos_row = jnp.repeat(jnp.arange(rows, dtype=F32), GRID_W)
    pos_col = jnp.tile(jnp.arange(GRID_W, dtype=F32), rows)
    n_freq = ROPE_DIM // 4
    inv_freq = ROPE_BASE ** (-jnp.arange(n_freq, dtype=F32) / n_freq)
    ang = jnp.stack([pos_row[:, None] * inv_freq, pos_col[:, None] * inv_freq], axis=1)
    cos, sin = jnp.cos(ang), jnp.sin(ang)
    zero = jnp.zeros_like(sin)
    c = jnp.stack([cos, cos], axis=2).reshape(n_tokens, ROPE_DIM)
    sm = jnp.stack([-sin, zero], axis=2).reshape(n_tokens, ROPE_DIM)
    sp = jnp.stack([zero, sin], axis=2).reshape(n_tokens, ROPE_DIM)

    def place(a, fill):
        left = jnp.full((n_tokens, ROPE_LANE0), fill, F32)
        right = jnp.full((n_tokens, LANES - ROPE_LANE0 - ROPE_DIM), fill, F32)
        body = jnp.concatenate([left, a, right], axis=1)
        ident = jnp.full((TM, LANES), fill, F32)
        return jnp.concatenate([body, ident], axis=0)

    return jnp.stack([place(c, 1.0), place(sm, 0.0), place(sp, 0.0)], axis=0)


def kernel(x_prompt, x_sample, c, cache_mla_ckv, cache_mla_krope, state_mlstm_C, state_mlstm_n, state_mlstm_m, c_ctx, ada_w, ada_b, norm1_g, norm2_g, ab_w_in, mla_q_norm_g, mla_w_uq, mla_kv_norm_g, mla_w_ukv, mlstm_gate_b, mlstm_norm_g, ab_w_out, conv_w_pw1, conv_w_dw, conv_b_dw, conv_ln_g, conv_ln_b, conv_w_pw2, router_w, router_b, moe_w_in, moe_b_in, moe_w_out, moe_b_out, final_norm_g):
    bp, sp_, _ = x_prompt.shape
    bs, ss, _ = x_sample.shape
    past = cache_mla_ckv.shape[2]
    tp, ts = bp * sp_, bs * ss
    t = tp + ts
    nt = t // TM
    npt = tp // TM
    tps = ss // TM
    assert sp_ == TM and ss % TM == 0 and bs <= CTX_ROW and tp % ss == 0

    xp2 = x_prompt.reshape(tp, D)
    xs2 = x_sample.reshape(ts, D)
    cond = jnp.zeros((N_COND, D), F32).at[:bs].set(c).at[CTX_ROW].set(c_ctx)
    mod = _modulation(cond, ada_w, ada_b)

    def mod_row(i):
        return jnp.where(i < npt, CTX_ROW, (i - npt) // tps)

    def mod_spec(layer):
        return pl.BlockSpec((None, None, 6, D), lambda i, *_: (layer, mod_row(i), 0, 0))

    row_spec = lambda w: pl.BlockSpec((TM, w), lambda i, *_: (i, 0))
    prompt_spec = pl.BlockSpec((TM, D), lambda i, *_: (jnp.minimum(i, npt - 1), 0))
    sample_spec = pl.BlockSpec((TM, D), lambda i, *_: (jnp.maximum(i - npt, 0), 0))
    full = lambda a: pl.BlockSpec(a.shape, lambda i, *_: (0,) * a.ndim)
    vec = lambda a: a.reshape(1, -1)
    any_spec = pl.BlockSpec(memory_space=pl.ANY)

    def router_args(layer):
        wr = jnp.pad(router_w[layer], ((0, 0), (0, LANES - N_EXPERTS))).astype(BF16)
        br = jnp.pad(router_b[layer], (0, LANES - N_EXPERTS)).reshape(1, LANES)
        return wr, br

    moe_out_shapes = [
        jax.ShapeDtypeStruct((t, D), F32),
        jax.ShapeDtypeStruct((nt * SLOTS, D // 2), U32),
        jax.ShapeDtypeStruct((t, LANES), F32),
        jax.ShapeDtypeStruct((nt, SUBLANES, LANES), F32),
    ]
    moe_out_specs = [row_spec(D), pl.BlockSpec((SLOTS, D // 2), lambda i: (i, 0)), row_spec(LANES),
                     pl.BlockSpec((None, SUBLANES, LANES), lambda i: (i, 0, 0))]

    j = 0
    w = ab_w_in[j]
    zc = lambda n: jnp.zeros((D, n), F32)
    misc_w = jnp.concatenate([w[:, 2464:2480], zc(ROPE_LANE0 - 16), w[:, 384:416], zc(LANES - ROPE_LANE0 - ROPE_DIM)], axis=1)
    w_in = jnp.concatenate([w[:, :384], misc_w, w[:, 416:2464]], axis=1).astype(BF16)
    wq = jnp.pad(mla_w_uq[j].reshape(Q_RANK, H_A, NOPE + ROPE_DIM), ((0, 0), (0, 0), (0, HEAD_PAD - NOPE - ROPE_DIM)))
    wq = wq.reshape(Q_RANK, H_A * HEAD_PAD).astype(BF16)
    wkv = mla_w_ukv[j].reshape(KV_RANK, H_A, NOPE + V_A)
    wk = jnp.pad(wkv[:, :, :NOPE], ((0, 0), (0, 0), (0, HEAD_PAD - NOPE))).reshape(KV_RANK, H_A * HEAD_PAD).astype(BF16)
    wv = jnp.pad(wkv[:, :, NOPE:], ((0, 0), (0, 0), (0, HEAD_PAD - V_A))).reshape(KV_RANK, H_A * HEAD_PAD).astype(BF16)
    gate_b = jnp.pad(mlstm_gate_b[j], (0, LANES - 4 * H_B)).reshape(1, LANES)
    tabs = _rope_tables(ss)
    pos_blocks = ss // TM

    def tab_block(i):
        return jnp.where(i < npt, pos_blocks, (i - npt) % tps)

    ins = [xp2, xs2, mod, vec(norm1_g[0]), w_in, vec(mla_q_norm_g[j]), wq, vec(mla_kv_norm_g[j]), wk, wv, gate_b, tabs]
    in_specs = [prompt_spec, sample_spec, mod_spec(0)] + [full(a) for a in ins[3:11]] + [
        pl.BlockSpec((3, TM, LANES), lambda i: (0, tab_block(i), 0))]
    q, k, v, ckv, misc, mqkv, mo = pl.pallas_call(
        functools.partial(_ab_in_kernel, npt),
        grid=(nt,),
        in_specs=in_specs,
        out_specs=[row_spec(H_A * HEAD_PAD), row_spec(H_A * HEAD_PAD), row_spec(H_A * HEAD_PAD), row_spec(KV_RANK),
                   row_spec(LANES), row_spec(3 * MB), row_spec(MB)],
        out_shape=[
            jax.ShapeDtypeStruct((t, H_A * HEAD_PAD), BF16), jax.ShapeDtypeStruct((t, H_A * HEAD_PAD), BF16),
            jax.ShapeDtypeStruct((t, H_A * HEAD_PAD), BF16), jax.ShapeDtypeStruct((t, KV_RANK), F32),
            jax.ShapeDtypeStruct((t, LANES), F32), jax.ShapeDtypeStruct((t, 3 * MB), BF16),
            jax.ShapeDtypeStruct((t, MB), F32)],
        compiler_params=_cparams(("arbitrary",)),
        name="ab_in_proj",
    )(*ins)

    nctx = bs * past
    ctx_ckv = cache_mla_ckv[:, j].reshape(nctx, KV_RANK)
    ctx_kr = jnp.pad(cache_mla_krope[:, j].reshape(nctx, ROPE_DIM), ((0, 0), (ROPE_LANE0, LANES - ROPE_LANE0 - ROPE_DIM)))
    k_ctx, v_ctx = pl.pallas_call(
        _ctx_kv_kernel,
        grid=(nctx // TM,),
        in_specs=[row_spec(KV_RANK), row_spec(LANES), full(wk), full(wv)],
        out_specs=[row_spec(H_A * HEAD_PAD), row_spec(H_A * HEAD_PAD)],
        out_shape=[jax.ShapeDtypeStruct((nctx, H_A * HEAD_PAD), BF16)] * 2,
        compiler_params=_cparams(("arbitrary",)),
        name="ctx_kv",
    )(ctx_ckv, ctx_kr, wk, wv)

    o_a = _attention(q, k, v, 0, bp, sp_, sp_, jnp.zeros((t, H_A * V_A), BF16))
    seq_keys = lambda ctx, own: jnp.concatenate(
        [ctx.reshape(bs, past, -1), own[tp:].reshape(bs, ss, -1)], axis=1).reshape(bs * (past + ss), -1)
    o_a = _attention(q, seq_keys(k_ctx, k), seq_keys(v_ctx, v), tp, bs, ss, past + ss, o_a)

    zc0 = jnp.zeros((bp, 2, H_B, DH_B, DH_B), F32)
    zn0 = jnp.zeros((bp, 2, H_B, 1, DH_B), F32)
    zm0 = jnp.zeros((bp, 2, H_B, 1, LANES), F32)
    h_f, h_b, c_p, n_p, m_p = _mlstm(mqkv, misc, zc0, zn0, zm0, 0, bp, sp_,
                                     (jnp.zeros((t, MB), F32), jnp.zeros((t, MB), F32)))
    sm0 = jnp.broadcast_to(state_mlstm_m[:, j][..., None, None], (bs, 2, H_B, 1, LANES))
    h_f, h_b, _, _, _ = _mlstm(mqkv, misc, state_mlstm_C[:, j], state_mlstm_n[:, j][:, :, :, None, :], sm0, tp, bs, ss,
                               (h_f, h_b))

    wr, br = router_args(0)
    ins = [o_a, h_f, h_b, mo, vec(mlstm_norm_g[j]), ab_w_out[j].astype(BF16), xp2, xs2, mod, vec(norm2_g[0]), wr, br]
    in_specs = [row_spec(H_A * V_A), row_spec(MB), row_spec(MB), row_spec(MB), full(ins[4]), full(ins[5]),
                prompt_spec, sample_spec, mod_spec(0), full(ins[9]), full(wr), full(br)]
    x1, xc, route, cnt = pl.pallas_call(
        functools.partial(_ab_out_kernel, npt),
        grid=(nt,),
        in_specs=in_specs,
        out_specs=moe_out_specs,
        out_shape=moe_out_shapes,
        compiler_params=_cparams(("arbitrary",)),
        name="ab_out_proj",
    )(*ins)
    ys = _moe(cnt, xc, 0, moe_w_in, moe_b_in, moe_w_out, moe_b_out)

    slots_spec = pl.BlockSpec((SLOTS, D // 2), lambda i, *_: (i, 0))
    ins = [ys, x1, route, mod, mod, vec(norm1_g[1]), conv_w_pw1[0].astype(BF16)]
    x2, u = pl.pallas_call(
        _conv_in_kernel,
        grid=(nt,),
        in_specs=[slots_spec, row_spec(D), row_spec(LANES), mod_spec(0), mod_spec(1), full(ins[5]), full(ins[6])],
        out_specs=[row_spec(D), row_spec(D)],
        out_shape=[jax.ShapeDtypeStruct((t, D), F32), jax.ShapeDtypeStruct((t, D), F32)],
        compiler_params=_cparams(("arbitrary",)),
        name="conv_in_proj",
    )(*ins)

    hpt = TM // HALO
    nhalo = t // HALO
    wr, br = router_args(1)
    ins = [u, u, u, conv_w_dw[0], vec(conv_b_dw[0]), vec(conv_ln_g[0]), vec(conv_ln_b[0]), conv_w_pw2[0].astype(BF16),
           x2, mod, vec(norm2_g[1]), wr, br]
    in_specs = [row_spec(D),
                pl.BlockSpec((HALO, D), lambda i: (jnp.maximum(i * hpt - 1, 0), 0)),
                pl.BlockSpec((HALO, D), lambda i: (jnp.minimum((i + 1) * hpt, nhalo - 1), 0)),
                full(ins[3]), full(ins[4]), full(ins[5]), full(ins[6]), full(ins[7]), row_spec(D), mod_spec(1),
                full(ins[10]), full(wr), full(br)]
    x1, xc, route, cnt = pl.pallas_call(
        functools.partial(_conv_out_kernel, npt, tps),
        grid=(nt,),
        in_specs=in_specs,
        out_specs=moe_out_specs,
        out_shape=moe_out_shapes,
        scratch_shapes=[pltpu.VMEM((TM + 2 * HALO, D), F32), pltpu.VMEM((SUBLANES, TM + 24, D), F32)],
        compiler_params=_cparams(("arbitrary",)),
        name="conv_out_proj",
    )(*ins)
    ys = _moe(cnt, xc, 1, moe_w_in, moe_b_in, moe_w_out, moe_b_out)

    ins = [ys, x1, route, mod, vec(final_norm_g)]
    y_p, y_s = pl.pallas_call(
        functools.partial(_final_kernel, npt),
        grid=(nt,),
        in_specs=[slots_spec, row_spec(D), row_spec(LANES), mod_spec(1), full(ins[4])],
        out_specs=[prompt_spec, sample_spec],
        out_shape=[jax.ShapeDtypeStruct((tp, D), F32), jax.ShapeDtypeStruct((ts, D), F32)],
        compiler_params=_cparams(("arbitrary",)),
        name="final_norm",
    )(*ins)

    y_prompt = y_p.reshape(bp, sp_, D)
    y_sample = y_s.reshape(bs, ss, D)
    new_ckv = ckv[:tp].reshape(bp, 1, sp_, KV_RANK)
    new_krope = misc[:tp, ROPE_LANE0:ROPE_LANE0 + ROPE_DIM].reshape(bp, 1, sp_, ROPE_DIM)
    new_c = c_p[:, None]
    new_n = n_p[:, None, :, :, 0, :]
    new_m = m_p[:, None, :, :, 0, 0]
    return (y_prompt, y_sample, new_ckv, new_krope, new_c, new_n, new_m)
```

```python
import functools
import math

import jax
import jax.numpy as jnp
from jax import lax
from jax.experimental import pallas as pl
from jax.experimental.pallas import tpu as pltpu

F32 = jnp.float32
BF16 = jnp.bfloat16
I32 = jnp.int32
U32 = jnp.uint32

D = 1024
GRID_W = 64
H_A = 8
Q_RANK = 256
KV_RANK = 128
NOPE = 64
ROPE_DIM = 32
V_A = 64
ROPE_BASE = 10000.0
H_B = 4
DH_B = 128
MB = H_B * DH_B
CONV_K = 31
N_EXPERTS = 32
TOP_K = 4
D_FF = 1024
SWIGLU_LIMIT = 7.0
SWIGLU_ALPHA = 1.702

LANES = 128
SUBLANES = 8
HEAD_PAD = 128
ROPE_LANE0 = 64
N_COND = 16
CTX_ROW = 8

TM = 256
FB = 512
LC = 256
TQ = 256
HPS = 4
CH = SUBLANES
SLOTS = TOP_K * TM + N_EXPERTS * CH
VMEM_LIMIT = 56 * 1024 * 1024

NEG_INF = float("-inf")
LOG2_E = math.log2(math.e)


def _cparams(sem):
    return pltpu.CompilerParams(dimension_semantics=sem, vmem_limit_bytes=VMEM_LIMIT)


def _sigmoid(x):
    return 1.0 / (1.0 + jnp.exp(-x))


def _rms(x, g, eps=1e-6):
    return x * lax.rsqrt(jnp.mean(x * x, axis=-1, keepdims=True) + eps) * g


def _lane(shape):
    return lax.broadcasted_iota(I32, shape, len(shape) - 1)


def _pack_pairs(x):
    w = x.shape[1] // 2
    return pltpu.bitcast(x[:, :w], U32) | (pltpu.bitcast(x[:, w:], U32) >> 16)


def _unpack_pairs(wd):
    a = pltpu.bitcast(wd & jnp.uint32(0xFFFF0000), F32).astype(BF16)
    b = pltpu.bitcast(wd << 16, F32).astype(BF16)
    return a, b


def _mod_kernel(cond_ref, w_ref, b_ref, o_ref):
    c = cond_ref[...]
    s = (c * _sigmoid(c)).astype(BF16)
    o_ref[...] = jnp.dot(s, w_ref[...].astype(BF16), preferred_element_type=F32) + b_ref[...]


def _modulation(cond, ada_w, ada_b):
    depth = ada_w.shape[0]
    nj = ada_w.shape[2] // D
    out = pl.pallas_call(
        _mod_kernel,
        grid=(depth, nj),
        in_specs=[
            pl.BlockSpec((N_COND, D), lambda l, j: (0, 0)),
            pl.BlockSpec((None, D, D), lambda l, j: (l, 0, j)),
            pl.BlockSpec((None, 1, D), lambda l, j: (l, 0, j)),
        ],
        out_specs=pl.BlockSpec((None, N_COND, D), lambda l, j: (l, 0, j)),
        out_shape=jax.ShapeDtypeStruct((depth, N_COND, nj * D), F32),
        compiler_params=_cparams(("arbitrary", "arbitrary")),
        name="ada_modulation",
    )(cond, ada_w, ada_b.reshape(depth, 1, nj * D))
    return out.reshape(depth, N_COND, nj, D)


def _combine(route, rows):
    lane = _lane((route.shape[0], SLOTS)).astype(F32)
    g = jnp.where(lane == route[:, 8:9], route[:, 4:5], 0.0)
    for k in range(1, TOP_K):
        g = g + jnp.where(lane == route[:, 8 + k:9 + k], route[:, 4 + k:5 + k], 0.0)
    gb = g.astype(BF16)
    ya, yb = _unpack_pairs(rows)
    return jnp.concatenate([jnp.dot(gb, ya, preferred_element_type=F32),
                            jnp.dot(gb, yb, preferred_element_type=F32)], axis=-1)


def _moe_route(x1, mod_ref, n2g, wr_ref, br_ref, xc_ref, route_ref, cnt_ref):
    h2 = _rms(x1, n2g) * (1.0 + mod_ref[4:5, :]) + mod_ref[3:4, :]
    h2b = h2.astype(BF16)
    logits = jnp.dot(h2b, wr_ref[...], preferred_element_type=F32) + br_ref[...]
    tm = logits.shape[0]
    lane = _lane((tm, LANES))
    lanef = lane.astype(F32)
    l = jnp.where(lane < N_EXPERTS, logits, NEG_INF)
    tops, idxs, sels = [], [], []
    for _ in range(TOP_K):
        mv = jnp.max(l, axis=-1, keepdims=True)
        idx = jnp.min(jnp.where(l == mv, lanef, float(LANES)), axis=-1, keepdims=True)
        sel = lanef == idx
        l = jnp.where(sel, NEG_INF, l)
        tops.append(mv)
        idxs.append(idx)
        sels.append(sel)
    ex = [jnp.exp(t - tops[0]) for t in tops]
    den = ex[0] + ex[1] + ex[2] + ex[3]
    gates = [e / den for e in ex]

    onehot = jnp.zeros((tm, LANES), F32)
    for sel in sels:
        onehot = onehot + jnp.where(sel, 1.0, 0.0)
    row = lax.broadcasted_iota(I32, (tm, tm), 0)
    col = lax.broadcasted_iota(I32, (tm, tm), 1)
    strict = jnp.where(col < row, 1.0, 0.0).astype(BF16)
    before = jnp.dot(strict, onehot.astype(BF16), preferred_element_type=F32)
    n_e = jnp.sum(onehot, axis=0, keepdims=True)
    cnt_ref[...] = jnp.broadcast_to(n_e, cnt_ref.shape)
    nch = jnp.right_shift(n_e.astype(I32) + (CH - 1), CH.bit_length() - 1).astype(F32)
    r128 = lax.broadcasted_iota(I32, (LANES, LANES), 0)
    c128 = lax.broadcasted_iota(I32, (LANES, LANES), 1)
    upper = jnp.where(r128 < c128, 1.0, 0.0).astype(BF16)
    seg0 = float(CH) * jnp.dot(jnp.broadcast_to(nch, (SUBLANES, LANES)).astype(BF16), upper,
                               preferred_element_type=F32)[0:1, :]
    slot_of = seg0 + before
    slots = [jnp.sum(jnp.where(sel, slot_of, 0.0), axis=-1, keepdims=True) for sel in sels]

    r = jnp.zeros((tm, LANES), F32)
    for j, colv in enumerate(idxs + gates + slots):
        r = jnp.where(lane == j, colv, r)
    route_ref[...] = r

    lane_s = _lane((tm, SLOTS)).astype(F32)
    pt = jnp.where(lane_s == slots[0], 1.0, 0.0)
    for k in range(1, TOP_K):
        pt = pt + jnp.where(lane_s == slots[k], 1.0, 0.0)
    xc = jnp.dot(pt.T.astype(BF16), h2b, preferred_element_type=F32)
    xc_ref[...] = _pack_pairs(xc)


def _values_with_ones(ckvb, wv_ref):
    vv = jnp.dot(ckvb, wv_ref[...], preferred_element_type=F32)
    return jnp.where((_lane(vv.shape) & (HEAD_PAD - 1)) == V_A, 1.0, vv).astype(BF16)


def _rope(x, tab_ref):
    return x * tab_ref[0] + pltpu.roll(x, LANES - 8, 1) * tab_ref[1] + pltpu.roll(x, 8, 1) * tab_ref[2]


def _ab_in_kernel(npt, nt, *refs):
    @pl.when(pl.program_id(0) < nt)
    def _():
        _ab_in_tokens(npt, *refs)

    @pl.when(pl.program_id(0) >= nt)
    def _():
        _ab_in_context(*refs)


def _ab_in_context(xp_ref, xs_ref, cckv_ref, ckr_ref, mod_ref, n1g_ref, win_ref, qg_ref, wq_ref, kvg_ref, wk_ref, wv_ref,
                   gb_ref, tab_ref, q_ref, k_ref, v_ref, ckv_ref, misc_ref, mqkv_ref, mo_ref):
    ckvb = cckv_ref[...].astype(BF16)
    kn = jnp.dot(ckvb, wk_ref[...], preferred_element_type=F32)
    v_ref[...] = _values_with_ones(ckvb, wv_ref)
    kr = ckr_ref[...]
    for hh in range(H_A):
        sl = slice(hh * HEAD_PAD, (hh + 1) * HEAD_PAD)
        k_ref[:, sl] = (kn[:, sl] + kr).astype(BF16)


def _ab_in_tokens(npt, xp_ref, xs_ref, cckv_ref, ckr_ref, mod_ref, n1g_ref, win_ref, qg_ref, wq_ref, kvg_ref, wk_ref,
                  wv_ref, gb_ref, tab_ref, q_ref, k_ref, v_ref, ckv_ref, misc_ref, mqkv_ref, mo_ref):
    x = jnp.where(pl.program_id(0) < npt, xp_ref[...], xs_ref[...])
    h = _rms(x, n1g_ref[...]) * (1.0 + mod_ref[1:2, :]) + mod_ref[0:1, :]
    z = jnp.dot(h.astype(BF16), win_ref[...], preferred_element_type=F32)

    qn = _rms(z[:, :Q_RANK], qg_ref[...]).astype(BF16)
    q = jnp.dot(qn, wq_ref[...], preferred_element_type=F32)
    ckv = _rms(z[:, Q_RANK:Q_RANK + KV_RANK], kvg_ref[...])
    ckv_ref[...] = ckv
    ckvb = ckv.astype(BF16)
    kn = jnp.dot(ckvb, wk_ref[...], preferred_element_type=F32)
    v_ref[...] = _values_with_ones(ckvb, wv_ref)

    misc = z[:, Q_RANK + KV_RANK:Q_RANK + KV_RANK + LANES]
    misc_ref[...] = misc + gb_ref[...]
    lane = _lane(misc.shape)
    kr = jnp.where((lane >= ROPE_LANE0) & (lane < ROPE_LANE0 + ROPE_DIM), _rope(misc, tab_ref), 0.0)
    scale = float((NOPE + ROPE_DIM) ** -0.5 * LOG2_E)
    for hh in range(H_A):
        sl = slice(hh * HEAD_PAD, (hh + 1) * HEAD_PAD)
        q_ref[:, sl] = (_rope(q[:, sl], tab_ref) * scale).astype(BF16)
        k_ref[:, sl] = (kn[:, sl] + kr).astype(BF16)

    m0 = Q_RANK + KV_RANK + LANES
    mqkv_ref[:, :MB] = z[:, m0:m0 + MB].astype(BF16)
    mqkv_ref[:, MB:2 * MB] = (z[:, m0 + MB:m0 + 2 * MB] * float(DH_B ** -0.5)).astype(BF16)
    mqkv_ref[:, 2 * MB:] = z[:, m0 + 2 * MB:m0 + 3 * MB].astype(BF16)
    mo_ref[...] = z[:, m0 + 3 * MB:m0 + 4 * MB]


def _attn_kernel(q_ref, k_ref, v_ref, prev_ref, o_ref):
    del prev_ref
    outs = []
    for j in range(HPS):
        sl = slice(j * HEAD_PAD, (j + 1) * HEAD_PAD)
        s = lax.dot_general(q_ref[:, sl], k_ref[:, sl], (((1,), (1,)), ((), ())), preferred_element_type=F32)
        m = jnp.max(s, axis=-1, keepdims=True)
        p = jnp.exp2(s - m).astype(BF16)
        o = jnp.dot(p, v_ref[:, sl], preferred_element_type=F32)
        outs.append(o / o[:, V_A:V_A + 1])
    lane = _lane(outs[0].shape)
    for j in range(0, HPS, 2):
        o_ref[:, j * V_A:(j + 2) * V_A] = jnp.where(lane < V_A, outs[j], pltpu.roll(outs[j + 1], V_A, 1)).astype(BF16)


def _attention(q, k, v, qrow0, nb, s, krow0, l, prev):
    t = q.shape[0]
    tq = min(TQ, s)
    nq = s // tq
    qrow = lambda bi, hp, qi: (qrow0 // tq + bi * nq + qi, hp)
    krow = lambda bi, hp, qi: (krow0 // l + bi, hp)
    ins = [q, k, v, prev]
    in_specs = [pl.BlockSpec((tq, HPS * HEAD_PAD), qrow), pl.BlockSpec((l, HPS * HEAD_PAD), krow),
                pl.BlockSpec((l, HPS * HEAD_PAD), krow), pl.BlockSpec(memory_space=pl.ANY)]
    aliases = {3: 0}
    return pl.pallas_call(
        _attn_kernel,
        grid=(nb, H_A // HPS, nq),
        in_specs=in_specs,
        out_specs=pl.BlockSpec((tq, HPS * V_A), qrow),
        out_shape=jax.ShapeDtypeStruct((t, H_A * V_A), BF16),
        input_output_aliases=aliases,
        compiler_params=_cparams(("arbitrary", "arbitrary", "arbitrary")),
        name="mla_attention",
    )(*ins)


def _log_sigmoid(x):
    return jnp.minimum(x, 0.0) - jnp.log(1.0 + jnp.exp(-jnp.abs(x)))


def _mlstm_direction(d, q_ref, k_ref, v_ref, g_ref, h_ref, c_s, n_s, m_s):
    gates = g_ref[...]
    lf_all = _log_sigmoid(gates)
    row = lax.broadcasted_iota(I32, (LC, LC), 0)
    col = lax.broadcasted_iota(I32, (LC, LC), 1)
    mask = (col <= row) if d == 0 else (col >= row)
    eye = row == col
    cum = jnp.dot(jnp.where(mask, 1.0, 0.0), lf_all, preferred_element_type=F32,
                  precision=lax.Precision.HIGHEST)

    for hh in range(H_B):
        sl = slice(hh * DH_B, (hh + 1) * DH_B)
        q = q_ref[:, sl]
        k = k_ref[:, sl]
        v = v_ref[:, sl]
        b_col = cum[:, 8 * d + 4 + hh:8 * d + 5 + hh]
        i_col = gates[:, 8 * d + hh:8 * d + hh + 1]
        m_prev = m_s[d, hh][:, 0:1]
        r_row = jnp.sum(jnp.where(eye, i_col - b_col, 0.0), axis=0, keepdims=True)
        dm = jnp.where(mask, b_col + r_row, NEG_INF)
        a_col = b_col + m_prev
        mt = jnp.maximum(a_col, jnp.max(dm, axis=-1, keepdims=True))
        w_inter = jnp.exp(a_col - mt)
        qk = lax.dot_general(q, k, (((1,), (1,)), ((), ())), preferred_element_type=F32)
        s = qk * jnp.exp(dm - mt)
        c_prev = c_s[d, hh]
        n_prev = n_s[d, hh]
        num = w_inter * jnp.dot(q, c_prev.astype(BF16), preferred_element_type=F32) \
            + jnp.dot(s.astype(BF16), v, preferred_element_type=F32)
        qf = q.astype(F32)
        den = w_inter * jnp.sum(qf * n_prev, axis=-1, keepdims=True) + jnp.sum(s, axis=-1, keepdims=True)
        h_ref[:, sl] = num / jnp.maximum(jnp.abs(den), jnp.exp(-mt))

        b_tot = b_col[LC - 1:LC, :] if d == 0 else b_col[0:1, :]
        g_col = b_tot - b_col + i_col
        m_new = jnp.maximum(b_tot + m_prev, jnp.max(g_col, axis=0, keepdims=True))
        w_c = jnp.exp(b_tot + m_prev - m_new)
        kw = k.astype(F32) * jnp.exp(g_col - m_new)
        c_s[d, hh] = w_c * c_prev + jnp.dot(kw.T.astype(BF16), v, preferred_element_type=F32)
        n_s[d, hh] = w_c * n_prev + jnp.sum(kw, axis=0, keepdims=True)
        m_s[d, hh] = jnp.broadcast_to(m_new, (1, LANES))


def _mlstm_kernel(qf_ref, kf_ref, vf_ref, gf_ref, qb_ref, kb_ref, vb_ref, gb_ref, c0_ref, n0_ref, m0_ref,
                  prevf_ref, prevb_ref, hf_ref, hb_ref, c_out, n_out, m_out, c_s, n_s, m_s):
    del prevf_ref, prevb_ref
    c = pl.program_id(1)

    @pl.when(c == 0)
    def _():
        c_s[...] = c0_ref[...]
        n_s[...] = n0_ref[...]
        m_s[...] = m0_ref[...]

    _mlstm_direction(0, qf_ref, kf_ref, vf_ref, gf_ref, hf_ref, c_s, n_s, m_s)
    _mlstm_direction(1, qb_ref, kb_ref, vb_ref, gb_ref, hb_ref, c_s, n_s, m_s)

    @pl.when(c == pl.num_programs(1) - 1)
    def _():
        c_out[...] = c_s[...]
        n_out[...] = n_s[...]
        m_out[...] = m_s[...]


def _mlstm(mqkv, misc, c0, n0, m0, row0, nb, s, prev):
    nc = s // LC
    blk0 = row0 // LC
    t = mqkv.shape[0]
    fwd = lambda bi, ci: blk0 + bi * nc + ci
    bwd = lambda bi, ci: blk0 + bi * nc + nc - 1 - ci
    chunk_specs = lambda rows: [
        pl.BlockSpec((LC, MB), lambda bi, ci: (rows(bi, ci), 0)),
        pl.BlockSpec((LC, MB), lambda bi, ci: (rows(bi, ci), 1)),
        pl.BlockSpec((LC, MB), lambda bi, ci: (rows(bi, ci), 2)),
        pl.BlockSpec((LC, LANES), lambda bi, ci: (rows(bi, ci), 0)),
    ]
    state_spec = lambda shape: pl.BlockSpec((None,) + shape, lambda bi, ci: (bi,) + (0,) * len(shape))
    state_specs = [state_spec((2, H_B, DH_B, DH_B)), state_spec((2, H_B, 1, DH_B)), state_spec((2, H_B, 1, LANES))]
    any_spec = pl.BlockSpec(memory_space=pl.ANY)
    return pl.pallas_call(
        _mlstm_kernel,
        grid=(nb, nc),
        in_specs=chunk_specs(fwd) + chunk_specs(bwd) + state_specs + [any_spec, any_spec],
        out_specs=[
            pl.BlockSpec((LC, MB), lambda bi, ci: (fwd(bi, ci), 0)),
            pl.BlockSpec((LC, MB), lambda bi, ci: (bwd(bi, ci), 0)),
        ] + state_specs,
        out_shape=[
            jax.ShapeDtypeStruct((t, MB), F32),
            jax.ShapeDtypeStruct((t, MB), F32),
            jax.ShapeDtypeStruct((nb, 2, H_B, DH_B, DH_B), F32),
            jax.ShapeDtypeStruct((nb, 2, H_B, 1, DH_B), F32),
            jax.ShapeDtypeStruct((nb, 2, H_B, 1, LANES), F32),
        ],
        scratch_shapes=[
            pltpu.VMEM((2, H_B, DH_B, DH_B), F32),
            pltpu.VMEM((2, H_B, 1, DH_B), F32),
            pltpu.VMEM((2, H_B, 1, LANES), F32),
        ],
        input_output_aliases={11: 0, 12: 1},
        compiler_params=_cparams(("arbitrary", "arbitrary")),
        name="mlstm_scan",
    )(mqkv, mqkv, mqkv, misc, mqkv, mqkv, mqkv, misc, c0, n0, m0, prev[0], prev[1])


def _ab_out_kernel(npt, oa_ref, hf_ref, hb_ref, mo_ref, mg_ref, wout_ref, xp_ref, xs_ref, mod_ref, n2g_ref, wr_ref,
                   br_ref, x1_ref, xc_ref, route_ref, cnt_ref):
    hsum = hf_ref[...] + hb_ref[...]
    parts = []
    for hh in range(H_B):
        hh_ = hsum[:, hh * DH_B:(hh + 1) * DH_B]
        mu = jnp.mean(hh_, axis=-1, keepdims=True)
        var = jnp.mean(jnp.square(hh_ - mu), axis=-1, keepdims=True)
        parts.append((hh_ - mu) * lax.rsqrt(var + 1e-5))
    hn = jnp.concatenate(parts, axis=-1) * mg_ref[...]
    ob = (hn * _sigmoid(mo_ref[...])).astype(BF16)
    o = jnp.dot(oa_ref[...], wout_ref[:H_A * V_A, :], preferred_element_type=F32) \
        + jnp.dot(ob, wout_ref[H_A * V_A:, :], preferred_element_type=F32)
    x = jnp.where(pl.program_id(0) < npt, xp_ref[...], xs_ref[...])
    x1 = x + mod_ref[2:3, :] * o
    x1_ref[...] = x1
    _moe_route(x1, mod_ref, n2g_ref[...], wr_ref, br_ref, xc_ref, route_ref, cnt_ref)


def _conv_in_kernel(rows_ref, x1_ref, route_ref, modp_ref, mod_ref, n1g_ref, w_ref, x2_ref, u_ref):
    x2 = x1_ref[...] + modp_ref[5:6, :] * _combine(route_ref[...], rows_ref[...])
    x2_ref[...] = x2
    h = _rms(x2, n1g_ref[...]) * (1.0 + mod_ref[1:2, :]) + mod_ref[0:1, :]
    z = jnp.dot(h.astype(BF16), w_ref[...], preferred_element_type=F32)
    u_ref[...] = z[:, :D] * _sigmoid(z[:, D:])


HALO = 16


def _conv_out_kernel(npt, tiles_per_seq, u_ref, up_ref, un_ref, wdw_ref, bdw_ref, lg_ref, lb_ref, w2_ref,
                     x_ref, mod_ref, n2g_ref, wr_ref, br_ref, x1_ref, xc_ref, route_ref, cnt_ref, ext_ref, sh_ref):
    i = pl.program_id(0)
    j = i - npt
    in_prompt = i < npt
    first = in_prompt | (j % tiles_per_seq == 0)
    last = in_prompt | (j % tiles_per_seq == tiles_per_seq - 1)
    ext_ref[0:HALO, :] = jnp.where(first, 0.0, up_ref[...])
    ext_ref[HALO:HALO + TM, :] = u_ref[...]
    ext_ref[HALO + TM:, :] = jnp.where(last, 0.0, un_ref[...])

    for b in range(SUBLANES):
        sh_ref[b] = ext_ref[b + 1:b + 1 + TM + 24, :]
    acc = jnp.zeros((TM, D), F32) + bdw_ref[...]
    for tap in range(CONV_K):
        a, b = divmod(tap, SUBLANES)
        acc = acc + wdw_ref[tap:tap + 1, :] * sh_ref[b, SUBLANES * a:SUBLANES * a + TM, :]
    mu = jnp.mean(acc, axis=-1, keepdims=True)
    var = jnp.mean(jnp.square(acc - mu), axis=-1, keepdims=True)
    ln = (acc - mu) * lax.rsqrt(var + 1e-5) * lg_ref[...] + lb_ref[...]
    act = (ln * _sigmoid(ln)).astype(BF16)
    o = jnp.dot(act, w2_ref[...], preferred_element_type=F32)
    x1 = x_ref[...] + mod_ref[2:3, :] * o
    x1_ref[...] = x1
    _moe_route(x1, mod_ref, n2g_ref[...], wr_ref, br_ref, xc_ref, route_ref, cnt_ref)


CPB = FB // CH


def _ffn_kernel(be_ref, nused_ref, nval_ref, src_ref, xc_in, win_ref, bin_ref, wout_ref, bout_ref, rows_hbm,
                win_s, wout_s, xbuf, ybuf, rsem, wsem):
    del xc_in
    i = pl.program_id(0)
    last = pl.num_programs(0) - 1
    e = be_ref[i]
    e_prev = be_ref[jnp.maximum(i - 1, 0)]

    def copy_in(blk, c):
        row = pl.multiple_of(src_ref[blk * CPB + c], CH)
        return pltpu.make_async_copy(rows_hbm.at[pl.ds(row, CH), :],
                                     xbuf.at[blk % 2, pl.ds(pl.multiple_of(c * CH, CH), CH), :], rsem.at[blk % 2])

    def copy_out(blk, c):
        row = pl.multiple_of(src_ref[blk * CPB + c], CH)
        return pltpu.make_async_copy(ybuf.at[blk % 2, pl.ds(pl.multiple_of(c * CH, CH), CH), :],
                                     rows_hbm.at[pl.ds(row, CH), :], wsem.at[blk % 2])

    def for_chunks(blk, fn):
        nv = nval_ref[blk]

        @pl.when(nv == CPB)
        def _():
            for c in range(CPB):
                fn(c)

        @pl.when(nv < CPB)
        def _():
            def body(c, carry):
                fn(c)
                return carry

            lax.fori_loop(0, nv, body, 0)

    @pl.when(i == 0)
    def _():
        xbuf[...] = jnp.zeros_like(xbuf)
        for_chunks(i, lambda c: copy_in(i, c).start())

    @pl.when(i < last)
    def _():
        for_chunks(i + 1, lambda c: copy_in(i + 1, c).start())

    @pl.when((i == 0) | (e != e_prev))
    def _():
        win_s[...] = win_ref[...].astype(BF16)
        wout_s[...] = wout_ref[...].astype(BF16)

    @pl.when(i >= 2)
    def _():
        for_chunks(i - 2, lambda c: copy_out(i - 2, c).wait())

    @pl.when(i < nused_ref[0])
    def _():
        for_chunks(i, lambda c: copy_in(i, c).wait())
        xa, xb = _unpack_pairs(xbuf[i % 2])
        hb = jnp.dot(xa, win_s[:D // 2, :], preferred_element_type=F32) \
            + jnp.dot(xb, win_s[D // 2:, :], preferred_element_type=F32) + bin_ref[...]
        g = jnp.minimum(hb[:, :D_FF], SWIGLU_LIMIT)
        u = jnp.clip(hb[:, D_FF:], -SWIGLU_LIMIT, SWIGLU_LIMIT)
        act = g * _sigmoid(SWIGLU_ALPHA * g) * (u + 1.0)
        y = jnp.dot(act.astype(BF16), wout_s[...], preferred_element_type=F32) + bout_ref[...]
        ybuf[i % 2] = _pack_pairs(y.astype(BF16).astype(F32))
        for_chunks(i, lambda c: copy_out(i, c).start())

    @pl.when(i == last)
    def _():
        @pl.when(i >= 1)
        def _():
            for_chunks(i - 1, lambda c: copy_out(i - 1, c).wait())

        for_chunks(i, lambda c: copy_out(i, c).wait())


def _expert_ffn(block_e, n_used, n_valid, src, xc, layer, w_in, b_in, w_out, b_out):
    n_blocks = block_e.shape[0]
    depth = w_in.shape[0]
    wspec = lambda shape: pl.BlockSpec((None, None) + shape, lambda i, be, *_: (layer, be[i], 0, 0))
    return pl.pallas_call(
        _ffn_kernel,
        grid_spec=pltpu.PrefetchScalarGridSpec(
            num_scalar_prefetch=4,
            grid=(n_blocks,),
            in_specs=[pl.BlockSpec(memory_space=pl.ANY), wspec((D, 2 * D_FF)), wspec((1, 2 * D_FF)),
                      wspec((D_FF, D)), wspec((1, D))],
            out_specs=pl.BlockSpec(memory_space=pl.ANY),
            scratch_shapes=[pltpu.VMEM((D, 2 * D_FF), BF16), pltpu.VMEM((D_FF, D), BF16),
                            pltpu.VMEM((2, FB, D // 2), U32), pltpu.VMEM((2, FB, D // 2), U32),
                            pltpu.SemaphoreType.DMA((2,)), pltpu.SemaphoreType.DMA((2,))],
        ),
        out_shape=jax.ShapeDtypeStruct(xc.shape, U32),
        input_output_aliases={4: 0},
        compiler_params=_cparams(("arbitrary",)),
        name="moe_expert_ffn",
    )(block_e, n_used, n_valid, src, xc, w_in, b_in.reshape(depth, N_EXPERTS, 1, 2 * D_FF), w_out,
      b_out.reshape(depth, N_EXPERTS, 1, D))


def _moe_plan(cnt):
    nt = cnt.shape[0]
    n_blocks = (nt * TM * TOP_K + nt * N_EXPERTS * (CH - 1)) // FB + N_EXPERTS + 1
    n = cnt[:, 0, :N_EXPERTS].astype(I32)
    nch = (n + CH - 1) // CH
    cend = jnp.cumsum(nch, axis=0)
    ce = cend[-1]
    pce = (ce + CPB - 1) // CPB * CPB
    e_end = jnp.cumsum(pce)
    e_start = e_end - pce
    first_slot = CH * (jnp.cumsum(nch, axis=1) - nch)

    blk0 = jnp.arange(n_blocks, dtype=I32) * CPB
    block_e = jnp.minimum(jnp.sum((blk0[:, None] >= e_end[None, :]).astype(I32), axis=-1), N_EXPERTS - 1)
    pick = lambda table, idx: jnp.sum(jnp.where(idx[:, None] == jnp.arange(table.shape[0], dtype=I32), table, 0), axis=-1)
    n_valid = jnp.clip(pick(e_start + ce, block_e) - blk0, 0, CPB)
    n_used = (e_end[-1] // CPB).reshape(1)

    d = jnp.arange(n_blocks * CPB, dtype=I32)
    e_of = jnp.repeat(block_e, CPB)
    local = d - pick(e_start, e_of)
    onehot_e = (e_of[:, None] == jnp.arange(N_EXPERTS, dtype=I32)).astype(F32)
    by_expert = lambda m: jnp.dot(onehot_e, m.T.astype(F32), precision=lax.Precision.HIGHEST).astype(I32)
    cend_e = by_expert(cend)
    in_tile = (local[:, None] >= cend_e - by_expert(nch)) & (local[:, None] < cend_e)
    tile_iota = jnp.arange(nt, dtype=I32)
    at_tile = lambda m: jnp.sum(jnp.where(in_tile, m, 0), axis=-1)
    src = at_tile(tile_iota * SLOTS + by_expert(first_slot) + CH * (local[:, None] - (cend_e - by_expert(nch))))
    return src.astype(I32), block_e.astype(I32), n_valid.astype(I32), n_used.astype(I32)


def _moe(cnt, xc, layer, w_in, b_in, w_out, b_out):
    src, block_e, n_valid, n_used = _moe_plan(cnt)
    return _expert_ffn(block_e, n_used, n_valid, src, xc, layer, w_in, b_in, w_out, b_out)


def _final_kernel(npt, rows_ref, x1_ref, route_ref, modp_ref, g_ref, op_ref, os_ref):
    i = pl.program_id(0)
    x2 = x1_ref[...] + modp_ref[5:6, :] * _combine(route_ref[...], rows_ref[...])
    out = _rms(x2, g_ref[...])

    @pl.when(i < npt)
    def _():
        op_ref[...] = out

    @pl.when(i >= npt)
    def _():
        os_ref[...] = out


def _rope_tables(n_tokens):
    rows = n_tokens // GRID_W
    pos_row = jnp.repeat(jnp.arange(rows, dtype=F32), GRID_W)
    pos_col = jnp.tile(jnp.arange(GRID_W, dtype=F32), rows)
    n_freq = ROPE_DIM // 4
    inv_freq = ROPE_BASE ** (-jnp.arange(n_freq, dtype=F32) / n_freq)
    ang = jnp.stack([pos_row[:, None] * inv_freq, pos_col[:, None] * inv_freq], axis=1)
    cos, sin = jnp.cos(ang), jnp.sin(ang)
    zero = jnp.zeros_like(sin)
    c = jnp.stack([cos, cos], axis=2).reshape(n_tokens, ROPE_DIM)
    sm = jnp.stack([-sin, zero], axis=2).reshape(n_tokens, ROPE_DIM)
    sp = jnp.stack([zero, sin], axis=2).reshape(n_tokens, ROPE_DIM)

    def place(a, fill):
        left = jnp.full((n_tokens, ROPE_LANE0), fill, F32)
        right = jnp.full((n_tokens, LANES - ROPE_LANE0 - ROPE_DIM), fill, F32)
        body = jnp.concatenate([left, a, right], axis=1)
        ident = jnp.full((TM, LANES), fill, F32)
        return jnp.concatenate([body, ident], axis=0)

    return jnp.stack([place(c, 1.0), place(sm, 0.0), place(sp, 0.0)], axis=0)


def kernel(x_prompt, x_sample, c, cache_mla_ckv, cache_mla_krope, state_mlstm_C, state_mlstm_n, state_mlstm_m, c_ctx, ada_w, ada_b, norm1_g, norm2_g, ab_w_in, mla_q_norm_g, mla_w_uq, mla_kv_norm_g, mla_w_ukv, mlstm_gate_b, mlstm_norm_g, ab_w_out, conv_w_pw1, conv_w_dw, conv_b_dw, conv_ln_g, conv_ln_b, conv_w_pw2, router_w, router_b, moe_w_in, moe_b_in, moe_w_out, moe_b_out, final_norm_g):
    bp, sp_, _ = x_prompt.shape
    bs, ss, _ = x_sample.shape
    past = cache_mla_ckv.shape[2]
    tp, ts = bp * sp_, bs * ss
    t = tp + ts
    nt = t // TM
    npt = tp // TM
    tps = ss // TM
    assert sp_ == TM and ss % TM == 0 and bs <= CTX_ROW and tp % ss == 0

    xp2 = x_prompt.reshape(tp, D)
    xs2 = x_sample.reshape(ts, D)
    cond = jnp.zeros((N_COND, D), F32).at[:bs].set(c).at[CTX_ROW].set(c_ctx)
    mod = _modulation(cond, ada_w, ada_b)

    def mod_row(i):
        return jnp.where(i < npt, CTX_ROW, (i - npt) // tps)

    def mod_spec(layer):
        return pl.BlockSpec((None, None, 6, D), lambda i, *_: (layer, mod_row(i), 0, 0))

    row_spec = lambda w: pl.BlockSpec((TM, w), lambda i, *_: (i, 0))
    prompt_spec = pl.BlockSpec((TM, D), lambda i, *_: (jnp.minimum(i, npt - 1), 0))
    sample_spec = pl.BlockSpec((TM, D), lambda i, *_: (jnp.maximum(i - npt, 0), 0))
    full = lambda a: pl.BlockSpec(a.shape, lambda i, *_: (0,) * a.ndim)
    vec = lambda a: a.reshape(1, -1)
    any_spec = pl.BlockSpec(memory_space=pl.ANY)

    def router_args(layer):
        wr = jnp.pad(router_w[layer], ((0, 0), (0, LANES - N_EXPERTS))).astype(BF16)
        br = jnp.pad(router_b[layer], (0, LANES - N_EXPERTS)).reshape(1, LANES)
        return wr, br

    moe_out_shapes = [
        jax.ShapeDtypeStruct((t, D), F32),
        jax.ShapeDtypeStruct((nt * SLOTS, D // 2), U32),
        jax.ShapeDtypeStruct((t, LANES), F32),
        jax.ShapeDtypeStruct((nt, SUBLANES, LANES), F32),
    ]
    moe_out_specs = [row_spec(D), pl.BlockSpec((SLOTS, D // 2), lambda i: (i, 0)), row_spec(LANES),
                     pl.BlockSpec((None, SUBLANES, LANES), lambda i: (i, 0, 0))]

    j = 0
    w = ab_w_in[j]
    zc = lambda n: jnp.zeros((D, n), F32)
    misc_w = jnp.concatenate([w[:, 2464:2480], zc(ROPE_LANE0 - 16), w[:, 384:416], zc(LANES - ROPE_LANE0 - ROPE_DIM)], axis=1)
    w_in = jnp.concatenate([w[:, :384], misc_w, w[:, 416:2464]], axis=1).astype(BF16)
    wq = jnp.pad(mla_w_uq[j].reshape(Q_RANK, H_A, NOPE + ROPE_DIM), ((0, 0), (0, 0), (0, HEAD_PAD - NOPE - ROPE_DIM)))
    wq = wq.reshape(Q_RANK, H_A * HEAD_PAD).astype(BF16)
    wkv = mla_w_ukv[j].reshape(KV_RANK, H_A, NOPE + V_A)
    wk = jnp.pad(wkv[:, :, :NOPE], ((0, 0), (0, 0), (0, HEAD_PAD - NOPE))).reshape(KV_RANK, H_A * HEAD_PAD).astype(BF16)
    wv = jnp.pad(wkv[:, :, NOPE:], ((0, 0), (0, 0), (0, HEAD_PAD - V_A))).reshape(KV_RANK, H_A * HEAD_PAD).astype(BF16)
    gate_b = jnp.pad(mlstm_gate_b[j], (0, LANES - 4 * H_B)).reshape(1, LANES)
    tabs = _rope_tables(ss)
    pos_blocks = ss // TM

    def tab_block(i):
        return jnp.where(i < npt, pos_blocks, (i - npt) % tps)

    nctx = bs * past
    ctx_ckv = cache_mla_ckv[:, j].reshape(nctx, KV_RANK)
    ctx_kr = jnp.pad(cache_mla_krope[:, j].reshape(nctx, ROPE_DIM), ((0, 0), (ROPE_LANE0, LANES - ROPE_LANE0 - ROPE_DIM)))
    nct, cps, nts = nctx // TM, past // TM, ts // TM
    kps = cps + tps
    kv_rows = bs * (past + ss) + tp

    def kv_block(i):
        own = ((i - npt) // tps) * kps + cps + (i - npt) % tps
        cached = ((i - nt) // cps) * kps + (i - nt) % cps
        return jnp.where(i < npt, bs * kps + i, jnp.where(i < nt, own, cached))

    tile_spec = lambda w: pl.BlockSpec((TM, w), lambda i: (jnp.minimum(i, nt - 1), 0))
    ctx_spec = pl.BlockSpec((TM, LANES), lambda i: (jnp.clip(i - nt, 0, nct - 1), 0))
    kv_spec = pl.BlockSpec((TM, H_A * HEAD_PAD), lambda i: (kv_block(i), 0))
    ins = [xp2, xs2, ctx_ckv, ctx_kr, mod, vec(norm1_g[0]), w_in, vec(mla_q_norm_g[j]), wq, vec(mla_kv_norm_g[j]), wk, wv,
           gate_b, tabs]
    in_specs = [prompt_spec, pl.BlockSpec((TM, D), lambda i: (jnp.clip(i - npt, 0, nts - 1), 0)), ctx_spec, ctx_spec,
                mod_spec(0)] + [full(a) for a in ins[5:13]] + [
        pl.BlockSpec((3, TM, LANES), lambda i: (0, tab_block(i), 0))]
    q, k, v, ckv, misc, mqkv, mo = pl.pallas_call(
        functools.partial(_ab_in_kernel, npt, nt),
        grid=(nt + nct,),
        in_specs=in_specs,
        out_specs=[tile_spec(H_A * HEAD_PAD), kv_spec, kv_spec, tile_spec(KV_RANK), tile_spec(LANES),
                   tile_spec(3 * MB), tile_spec(MB)],
        out_shape=[
            jax.ShapeDtypeStruct((t, H_A * HEAD_PAD), BF16), jax.ShapeDtypeStruct((kv_rows, H_A * HEAD_PAD), BF16),
            jax.ShapeDtypeStruct((kv_rows, H_A * HEAD_PAD), BF16), jax.ShapeDtypeStruct((t, KV_RANK), F32),
            jax.ShapeDtypeStruct((t, LANES), F32), jax.ShapeDtypeStruct((t, 3 * MB), BF16),
            jax.ShapeDtypeStruct((t, MB), F32)],
        compiler_params=_cparams(("arbitrary",)),
        name="ab_in_proj",
    )(*ins)

    o_a = _attention(q, k, v, 0, bp, sp_, bs * (past + ss), sp_, jnp.zeros((t, H_A * V_A), BF16))
    o_a = _attention(q, k, v, tp, bs, ss, 0, past + ss, o_a)

    zc0 = jnp.zeros((bp, 2, H_B, DH_B, DH_B), F32)
    zn0 = jnp.zeros((bp, 2, H_B, 1, DH_B), F32)
    zm0 = jnp.zeros((bp, 2, H_B, 1, LANES), F32)
    h_f, h_b, c_p, n_p, m_p = _mlstm(mqkv, misc, zc0, zn0, zm0, 0, bp, sp_,
                                     (jnp.zeros((t, MB), F32), jnp.zeros((t, MB), F32)))
    sm0 = jnp.broadcast_to(state_mlstm_m[:, j][..., None, None], (bs, 2, H_B, 1, LANES))
    h_f, h_b, _, _, _ = _mlstm(mqkv, misc, state_mlstm_C[:, j], state_mlstm_n[:, j][:, :, :, None, :], sm0, tp, bs, ss,
                               (h_f, h_b))

    wr, br = router_args(0)
    ins = [o_a, h_f, h_b, mo, vec(mlstm_norm_g[j]), ab_w_out[j].astype(BF16), xp2, xs2, mod, vec(norm2_g[0]), wr, br]
    in_specs = [row_spec(H_A * V_A), row_spec(MB), row_spec(MB), row_spec(MB), full(ins[4]), full(ins[5]),
                prompt_spec, sample_spec, mod_spec(0), full(ins[9]), full(wr), full(br)]
    x1, xc, route, cnt = pl.pallas_call(
        functools.partial(_ab_out_kernel, npt),
        grid=(nt,),
        in_specs=in_specs,
        out_specs=moe_out_specs,
        out_shape=moe_out_shapes,
        compiler_params=_cparams(("arbitrary",)),
        name="ab_out_proj",
    )(*ins)
    ys = _moe(cnt, xc, 0, moe_w_in, moe_b_in, moe_w_out, moe_b_out)

    slots_spec = pl.BlockSpec((SLOTS, D // 2), lambda i, *_: (i, 0))
    ins = [ys, x1, route, mod, mod, vec(norm1_g[1]), conv_w_pw1[0].astype(BF16)]
    x2, u = pl.pallas_call(
        _conv_in_kernel,
        grid=(nt,),
        in_specs=[slots_spec, row_spec(D), row_spec(LANES), mod_spec(0), mod_spec(1), full(ins[5]), full(ins[6])],
        out_specs=[row_spec(D), row_spec(D)],
        out_shape=[jax.ShapeDtypeStruct((t, D), F32), jax.ShapeDtypeStruct((t, D), F32)],
        compiler_params=_cparams(("arbitrary",)),
        name="conv_in_proj",
    )(*ins)

    hpt = TM // HALO
    nhalo = t // HALO
    wr, br = router_args(1)
    ins = [u, u, u, conv_w_dw[0], vec(conv_b_dw[0]), vec(conv_ln_g[0]), vec(conv_ln_b[0]), conv_w_pw2[0].astype(BF16),
           x2, mod, vec(norm2_g[1]), wr, br]
    in_specs = [row_spec(D),
                pl.BlockSpec((HALO, D), lambda i: (jnp.maximum(i * hpt - 1, 0), 0)),
                pl.BlockSpec((HALO, D), lambda i: (jnp.minimum((i + 1) * hpt, nhalo - 1), 0)),
                full(ins[3]), full(ins[4]), full(ins[5]), full(ins[6]), full(ins[7]), row_spec(D), mod_spec(1),
                full(ins[10]), full(wr), full(br)]
    x1, xc, route, cnt = pl.pallas_call(
        functools.partial(_conv_out_kernel, npt, tps),
        grid=(nt,),
        in_specs=in_specs,
        out_specs=moe_out_specs,
        out_shape=moe_out_shapes,
        scratch_shapes=[pltpu.VMEM((TM + 2 * HALO, D), F32), pltpu.VMEM((SUBLANES, TM + 24, D), F32)],
        compiler_params=_cparams(("arbitrary",)),
        name="conv_out_proj",
    )(*ins)
    ys = _moe(cnt, xc, 1, moe_w_in, moe_b_in, moe_w_out, moe_b_out)

    ins = [ys, x1, route, mod, vec(final_norm_g)]
    y_p, y_s = pl.pallas_call(
        functools.partial(_final_kernel, npt),
        grid=(nt,),
        in_specs=[slots_spec, row_spec(D), row_spec(LANES), mod_spec(1), full(ins[4])],
        out_specs=[prompt_spec, sample_spec],
        out_shape=[jax.ShapeDtypeStruct((tp, D), F32), jax.ShapeDtypeStruct((ts, D), F32)],
        compiler_params=_cparams(("arbitrary",)),
        name="final_norm",
    )(*ins)

    y_prompt = y_p.reshape(bp, sp_, D)
    y_sample = y_s.reshape(bs, ss, D)
    new_ckv = ckv[:tp].reshape(bp, 1, sp_, KV_RANK)
    new_krope = misc[:tp, ROPE_LANE0:ROPE_LANE0 + ROPE_DIM].reshape(bp, 1, sp_, ROPE_DIM)
    new_c = c_p[:, None]
    new_n = n_p[:, None, :, :, 0, :]
    new_m = m_p[:, None, :, :, 0, 0]
    return (y_prompt, y_sample, new_ckv, new_krope, new_c, new_n, new_m)
```

```python
import functools
import math

import jax
import jax.numpy as jnp
from jax import lax
from jax.experimental import pallas as pl
from jax.experimental.pallas import tpu as pltpu

F32 = jnp.float32
BF16 = jnp.bfloat16
I32 = jnp.int32
U32 = jnp.uint32

D = 1024
GRID_W = 64
H_A = 8
Q_RANK = 256
KV_RANK = 128
NOPE = 64
ROPE_DIM = 32
V_A = 64
ROPE_BASE = 10000.0
H_B = 4
DH_B = 128
MB = H_B * DH_B
CONV_K = 31
N_EXPERTS = 32
TOP_K = 4
D_FF = 1024
SWIGLU_LIMIT = 7.0
SWIGLU_ALPHA = 1.702

LANES = 128
SUBLANES = 8
HEAD_PAD = 128
ROPE_LANE0 = 64
N_COND = 16
CTX_ROW = 8

TM = 256
FB = 512
LC = 256
TQ = 256
HPS = 4
CH = SUBLANES
SLOTS = TOP_K * TM + N_EXPERTS * CH
VMEM_LIMIT = 56 * 1024 * 1024

NEG_INF = float("-inf")
LOG2_E = math.log2(math.e)


def _cparams(sem):
    return pltpu.CompilerParams(dimension_semantics=sem, vmem_limit_bytes=VMEM_LIMIT)


def _sigmoid(x):
    return 1.0 / (1.0 + jnp.exp(-x))


def _rms(x, g, eps=1e-6):
    return x * lax.rsqrt(jnp.mean(x * x, axis=-1, keepdims=True) + eps) * g


def _lane(shape):
    return lax.broadcasted_iota(I32, shape, len(shape) - 1)


def _pack_pairs(x):
    w = x.shape[1] // 2
    return pltpu.bitcast(x[:, :w], U32) | (pltpu.bitcast(x[:, w:], U32) >> 16)


def _unpack_pairs(wd):
    a = pltpu.bitcast(wd & jnp.uint32(0xFFFF0000), F32).astype(BF16)
    b = pltpu.bitcast(wd << 16, F32).astype(BF16)
    return a, b


def _mod_kernel(cond_ref, w_ref, b_ref, o_ref):
    c = cond_ref[...]
    s = (c * _sigmoid(c)).astype(BF16)
    o_ref[...] = jnp.dot(s, w_ref[...].astype(BF16), preferred_element_type=F32) + b_ref[...]


def _modulation(cond, ada_w, ada_b):
    depth = ada_w.shape[0]
    nj = ada_w.shape[2] // D
    out = pl.pallas_call(
        _mod_kernel,
        grid=(depth, nj),
        in_specs=[
            pl.BlockSpec((N_COND, D), lambda l, j: (0, 0)),
            pl.BlockSpec((None, D, D), lambda l, j: (l, 0, j)),
            pl.BlockSpec((None, 1, D), lambda l, j: (l, 0, j)),
        ],
        out_specs=pl.BlockSpec((None, N_COND, D), lambda l, j: (l, 0, j)),
        out_shape=jax.ShapeDtypeStruct((depth, N_COND, nj * D), F32),
        compiler_params=_cparams(("arbitrary", "arbitrary")),
        name="ada_modulation",
    )(cond, ada_w, ada_b.reshape(depth, 1, nj * D))
    return out.reshape(depth, N_COND, nj, D)


def _combine(route, rows):
    lane = _lane((route.shape[0], SLOTS)).astype(F32)
    g = jnp.zeros(lane.shape, F32)
    for k in range(TOP_K):
        g = jnp.where(lane == route[:, 8 + k:9 + k], route[:, 4 + k:5 + k], g)
    gb = g.astype(BF16)
    ya, yb = _unpack_pairs(rows)
    return jnp.concatenate([jnp.dot(gb, ya, preferred_element_type=F32),
                            jnp.dot(gb, yb, preferred_element_type=F32)], axis=-1)


def _moe_route(x1, mod_ref, n2g, wr_ref, br_ref, xc_ref, route_ref, cnt_ref):
    h2 = _rms(x1, n2g) * (1.0 + mod_ref[4:5, :]) + mod_ref[3:4, :]
    h2b = h2.astype(BF16)
    logits = jnp.dot(h2b, wr_ref[...], preferred_element_type=F32) + br_ref[...]
    tm = logits.shape[0]
    lane = _lane((tm, LANES))
    lanef = lane.astype(F32)
    l = jnp.where(lane < N_EXPERTS, logits, NEG_INF)
    tops, idxs, sels = [], [], []
    for _ in range(TOP_K):
        mv = jnp.max(l, axis=-1, keepdims=True)
        idx = jnp.min(jnp.where(l == mv, lanef, float(LANES)), axis=-1, keepdims=True)
        sel = lanef == idx
        l = jnp.where(sel, NEG_INF, l)
        tops.append(mv)
        idxs.append(idx)
        sels.append(sel)
    ex = [jnp.exp(t - tops[0]) for t in tops]
    den = ex[0] + ex[1] + ex[2] + ex[3]
    gates = [e / den for e in ex]

    onehot = jnp.zeros((tm, LANES), F32)
    for sel in sels:
        onehot = onehot + jnp.where(sel, 1.0, 0.0)
    row = lax.broadcasted_iota(I32, (tm, tm), 0)
    col = lax.broadcasted_iota(I32, (tm, tm), 1)
    strict = jnp.where(col < row, 1.0, 0.0).astype(BF16)
    before = jnp.dot(strict, onehot.astype(BF16), preferred_element_type=F32)
    n_e = jnp.sum(onehot, axis=0, keepdims=True)
    cnt_ref[...] = jnp.broadcast_to(n_e, cnt_ref.shape)
    nch = jnp.right_shift(n_e.astype(I32) + (CH - 1), CH.bit_length() - 1).astype(F32)
    r128 = lax.broadcasted_iota(I32, (LANES, LANES), 0)
    c128 = lax.broadcasted_iota(I32, (LANES, LANES), 1)
    upper = jnp.where(r128 < c128, 1.0, 0.0).astype(BF16)
    seg0 = float(CH) * jnp.dot(jnp.broadcast_to(nch, (SUBLANES, LANES)).astype(BF16), upper,
                               preferred_element_type=F32)[0:1, :]
    slot_of = seg0 + before
    slots = [jnp.sum(jnp.where(sel, slot_of, 0.0), axis=-1, keepdims=True) for sel in sels]

    r = jnp.zeros((tm, LANES), F32)
    for j, colv in enumerate(idxs + gates + slots):
        r = jnp.where(lane == j, colv, r)
    route_ref[...] = r

    lane_s = _lane((tm, SLOTS)).astype(F32)
    pt = jnp.zeros(lane_s.shape, F32)
    for k in range(TOP_K):
        pt = jnp.where(lane_s == slots[k], 1.0, pt)
    xc = jnp.dot(pt.T.astype(BF16), h2b, preferred_element_type=F32)
    xc_ref[...] = _pack_pairs(xc)


def _values_with_ones(ckvb, wv_ref):
    vv = jnp.dot(ckvb, wv_ref[...], preferred_element_type=F32)
    return jnp.where((_lane(vv.shape) & (HEAD_PAD - 1)) == V_A, 1.0, vv).astype(BF16)


def _rope(x, tab_ref):
    return x * tab_ref[0] + pltpu.roll(x, LANES - 8, 1) * tab_ref[1] + pltpu.roll(x, 8, 1) * tab_ref[2]


def _ab_in_kernel(npt, nt, *refs):
    @pl.when(pl.program_id(0) < nt)
    def _():
        _ab_in_tokens(npt, *refs)

    @pl.when(pl.program_id(0) >= nt)
    def _():
        _ab_in_context(*refs)


def _ab_in_context(xp_ref, xs_ref, cckv_ref, ckr_ref, mod_ref, n1g_ref, win_ref, qg_ref, wq_ref, kvg_ref, wk_ref, wv_ref,
                   gb_ref, tab_ref, q_ref, k_ref, v_ref, ckv_ref, misc_ref, mqkv_ref, mo_ref):
    ckvb = cckv_ref[...].astype(BF16)
    kn = jnp.dot(ckvb, wk_ref[...], preferred_element_type=F32)
    v_ref[...] = _values_with_ones(ckvb, wv_ref)
    kr = ckr_ref[...]
    for hh in range(H_A):
        sl = slice(hh * HEAD_PAD, (hh + 1) * HEAD_PAD)
        k_ref[:, sl] = (kn[:, sl] + kr).astype(BF16)


def _ab_in_tokens(npt, xp_ref, xs_ref, cckv_ref, ckr_ref, mod_ref, n1g_ref, win_ref, qg_ref, wq_ref, kvg_ref, wk_ref,
                  wv_ref, gb_ref, tab_ref, q_ref, k_ref, v_ref, ckv_ref, misc_ref, mqkv_ref, mo_ref):
    x = jnp.where(pl.program_id(0) < npt, xp_ref[...], xs_ref[...])
    h = _rms(x, n1g_ref[...]) * (1.0 + mod_ref[1:2, :]) + mod_ref[0:1, :]
    z = jnp.dot(h.astype(BF16), win_ref[...], preferred_element_type=F32)

    qn = _rms(z[:, :Q_RANK], qg_ref[...]).astype(BF16)
    q = jnp.dot(qn, wq_ref[...], preferred_element_type=F32)
    ckv = _rms(z[:, Q_RANK:Q_RANK + KV_RANK], kvg_ref[...])
    ckv_ref[...] = ckv
    ckvb = ckv.astype(BF16)
    kn = jnp.dot(ckvb, wk_ref[...], preferred_element_type=F32)
    v_ref[...] = _values_with_ones(ckvb, wv_ref)

    misc = z[:, Q_RANK + KV_RANK:Q_RANK + KV_RANK + LANES]
    misc_ref[...] = misc + gb_ref[...]
    lane = _lane(misc.shape)
    kr = jnp.where((lane >= ROPE_LANE0) & (lane < ROPE_LANE0 + ROPE_DIM), _rope(misc, tab_ref), 0.0)
    scale = float((NOPE + ROPE_DIM) ** -0.5 * LOG2_E)
    for hh in range(H_A):
        sl = slice(hh * HEAD_PAD, (hh + 1) * HEAD_PAD)
        q_ref[:, sl] = (_rope(q[:, sl], tab_ref) * scale).astype(BF16)
        k_ref[:, sl] = (kn[:, sl] + kr).astype(BF16)

    m0 = Q_RANK + KV_RANK + LANES
    mqkv_ref[:, :MB] = z[:, m0:m0 + MB].astype(BF16)
    mqkv_ref[:, MB:2 * MB] = (z[:, m0 + MB:m0 + 2 * MB] * float(DH_B ** -0.5)).astype(BF16)
    mqkv_ref[:, 2 * MB:] = z[:, m0 + 2 * MB:m0 + 3 * MB].astype(BF16)
    mo_ref[...] = z[:, m0 + 3 * MB:m0 + 4 * MB]


def _attn_kernel(q_ref, k_ref, v_ref, o_ref):
    outs = []
    for j in range(HPS):
        sl = slice(j * HEAD_PAD, (j + 1) * HEAD_PAD)
        s = lax.dot_general(q_ref[:, sl], k_ref[:, sl], (((1,), (1,)), ((), ())), preferred_element_type=F32)
        m = jnp.max(s, axis=-1, keepdims=True)
        p = jnp.exp2(s - m).astype(BF16)
        o = jnp.dot(p, v_ref[:, sl], preferred_element_type=F32)
        outs.append(o / o[:, V_A:V_A + 1])
    lane = _lane(outs[0].shape)
    for j in range(0, HPS, 2):
        o_ref[:, j * V_A:(j + 2) * V_A] = jnp.where(lane < V_A, outs[j], pltpu.roll(outs[j + 1], V_A, 1)).astype(BF16)


def _attention(q, k, v, qrow0, nb, s, krow0, l):
    tq = min(TQ, s)
    nq = s // tq
    qrow = lambda bi, hp, qi: (qrow0 // tq + bi * nq + qi, hp)
    krow = lambda bi, hp, qi: (krow0 // l + bi, hp)
    return pl.pallas_call(
        _attn_kernel,
        grid=(nb, H_A // HPS, nq),
        in_specs=[pl.BlockSpec((tq, HPS * HEAD_PAD), qrow), pl.BlockSpec((l, HPS * HEAD_PAD), krow),
                  pl.BlockSpec((l, HPS * HEAD_PAD), krow)],
        out_specs=pl.BlockSpec((tq, HPS * V_A), lambda bi, hp, qi: (bi * nq + qi, hp)),
        out_shape=jax.ShapeDtypeStruct((nb * s, H_A * V_A), BF16),
        compiler_params=_cparams(("arbitrary", "arbitrary", "arbitrary")),
        name="mla_attention",
    )(q, k, v)


def _log_sigmoid(x):
    return jnp.minimum(x, 0.0) - jnp.log(1.0 + jnp.exp(-jnp.abs(x)))


def _mlstm_direction(d, q_ref, k_ref, v_ref, g_ref, h_ref, c_s, n_s, m_s):
    gates = g_ref[...]
    lf_all = _log_sigmoid(gates)
    row = lax.broadcasted_iota(I32, (LC, LC), 0)
    col = lax.broadcasted_iota(I32, (LC, LC), 1)
    mask = (col <= row) if d == 0 else (col >= row)
    eye = row == col
    cum = jnp.dot(jnp.where(mask, 1.0, 0.0), lf_all, preferred_element_type=F32,
                  precision=lax.Precision.HIGHEST)

    for hh in range(H_B):
        sl = slice(hh * DH_B, (hh + 1) * DH_B)
        q = q_ref[:, sl]
        k = k_ref[:, sl]
        v = v_ref[:, sl]
        b_col = cum[:, 8 * d + 4 + hh:8 * d + 5 + hh]
        i_col = gates[:, 8 * d + hh:8 * d + hh + 1]
        m_prev = m_s[d, hh][:, 0:1]
        r_row = jnp.sum(jnp.where(eye, i_col - b_col, 0.0), axis=0, keepdims=True)
        dm = jnp.where(mask, b_col + r_row, NEG_INF)
        a_col = b_col + m_prev
        mt = jnp.maximum(a_col, jnp.max(dm, axis=-1, keepdims=True))
        w_inter = jnp.exp(a_col - mt)
        qk = lax.dot_general(q, k, (((1,), (1,)), ((), ())), preferred_element_type=F32)
        s = qk * jnp.exp(dm - mt)
        c_prev = c_s[d, hh]
        n_prev = n_s[d, hh]
        num = w_inter * jnp.dot(q, c_prev.astype(BF16), preferred_element_type=F32) \
            + jnp.dot(s.astype(BF16), v, preferred_element_type=F32)
        qf = q.astype(F32)
        den = w_inter * jnp.sum(qf * n_prev, axis=-1, keepdims=True) + jnp.sum(s, axis=-1, keepdims=True)
        h_ref[:, sl] = num / jnp.maximum(jnp.abs(den), jnp.exp(-mt))

        b_tot = b_col[LC - 1:LC, :] if d == 0 else b_col[0:1, :]
        g_col = b_tot - b_col + i_col
        m_new = jnp.maximum(b_tot + m_prev, jnp.max(g_col, axis=0, keepdims=True))
        w_c = jnp.exp(b_tot + m_prev - m_new)
        kw = k.astype(F32) * jnp.exp(g_col - m_new)
        c_s[d, hh] = w_c * c_prev + jnp.dot(kw.T.astype(BF16), v, preferred_element_type=F32)
        n_s[d, hh] = w_c * n_prev + jnp.sum(kw, axis=0, keepdims=True)
        m_s[d, hh] = jnp.broadcast_to(m_new, (1, LANES))


def _mlstm_kernel(qf_ref, kf_ref, vf_ref, gf_ref, qb_ref, kb_ref, vb_ref, gb_ref, c0_ref, n0_ref, m0_ref,
                  hf_ref, hb_ref, c_out, n_out, m_out, c_s, n_s, m_s):
    c = pl.program_id(1)

    @pl.when(c == 0)
    def _():
        c_s[...] = c0_ref[...]
        n_s[...] = n0_ref[...]
        m_s[...] = m0_ref[...]

    _mlstm_direction(0, qf_ref, kf_ref, vf_ref, gf_ref, hf_ref, c_s, n_s, m_s)
    _mlstm_direction(1, qb_ref, kb_ref, vb_ref, gb_ref, hb_ref, c_s, n_s, m_s)

    @pl.when(c == pl.num_programs(1) - 1)
    def _():
        c_out[...] = c_s[...]
        n_out[...] = n_s[...]
        m_out[...] = m_s[...]


def _mlstm(mqkv, misc, c0, n0, m0, row0, nb, s):
    nc = s // LC
    blk0 = row0 // LC
    fwd = lambda bi, ci: blk0 + bi * nc + ci
    bwd = lambda bi, ci: blk0 + bi * nc + nc - 1 - ci
    chunk_specs = lambda rows: [
        pl.BlockSpec((LC, MB), lambda bi, ci: (rows(bi, ci), 0)),
        pl.BlockSpec((LC, MB), lambda bi, ci: (rows(bi, ci), 1)),
        pl.BlockSpec((LC, MB), lambda bi, ci: (rows(bi, ci), 2)),
        pl.BlockSpec((LC, LANES), lambda bi, ci: (rows(bi, ci), 0)),
    ]
    state_spec = lambda shape: pl.BlockSpec((None,) + shape, lambda bi, ci: (bi,) + (0,) * len(shape))
    state_specs = [state_spec((2, H_B, DH_B, DH_B)), state_spec((2, H_B, 1, DH_B)), state_spec((2, H_B, 1, LANES))]
    return pl.pallas_call(
        _mlstm_kernel,
        grid=(nb, nc),
        in_specs=chunk_specs(fwd) + chunk_specs(bwd) + state_specs,
        out_specs=[
            pl.BlockSpec((LC, MB), lambda bi, ci: (bi * nc + ci, 0)),
            pl.BlockSpec((LC, MB), lambda bi, ci: (bi * nc + nc - 1 - ci, 0)),
        ] + state_specs,
        out_shape=[
            jax.ShapeDtypeStruct((nb * s, MB), F32),
            jax.ShapeDtypeStruct((nb * s, MB), F32),
            jax.ShapeDtypeStruct((nb, 2, H_B, DH_B, DH_B), F32),
            jax.ShapeDtypeStruct((nb, 2, H_B, 1, DH_B), F32),
            jax.ShapeDtypeStruct((nb, 2, H_B, 1, LANES), F32),
        ],
        scratch_shapes=[
            pltpu.VMEM((2, H_B, DH_B, DH_B), F32),
            pltpu.VMEM((2, H_B, 1, DH_B), F32),
            pltpu.VMEM((2, H_B, 1, LANES), F32),
        ],
        compiler_params=_cparams(("arbitrary", "arbitrary")),
        name="mlstm_scan",
    )(mqkv, mqkv, mqkv, misc, mqkv, mqkv, mqkv, misc, c0, n0, m0)


def _ab_out_kernel(npt, oap_ref, oas_ref, hfp_ref, hfs_ref, hbp_ref, hbs_ref, mo_ref, mg_ref, wout_ref, xp_ref, xs_ref,
                   mod_ref, n2g_ref, wr_ref, br_ref, x1_ref, xc_ref, route_ref, cnt_ref):
    in_prompt = pl.program_id(0) < npt
    pick = lambda p_ref, s_ref: jnp.where(in_prompt, p_ref[...], s_ref[...])
    hsum = pick(hfp_ref, hfs_ref) + pick(hbp_ref, hbs_ref)
    parts = []
    for hh in range(H_B):
        hh_ = hsum[:, hh * DH_B:(hh + 1) * DH_B]
        mu = jnp.mean(hh_, axis=-1, keepdims=True)
        var = jnp.mean(jnp.square(hh_ - mu), axis=-1, keepdims=True)
        parts.append((hh_ - mu) * lax.rsqrt(var + 1e-5))
    hn = jnp.concatenate(parts, axis=-1) * mg_ref[...]
    ob = (hn * _sigmoid(mo_ref[...])).astype(BF16)
    o = jnp.dot(pick(oap_ref, oas_ref), wout_ref[:H_A * V_A, :], preferred_element_type=F32) \
        + jnp.dot(ob, wout_ref[H_A * V_A:, :], preferred_element_type=F32)
    x = pick(xp_ref, xs_ref)
    x1 = x + mod_ref[2:3, :] * o
    x1_ref[...] = x1
    _moe_route(x1, mod_ref, n2g_ref[...], wr_ref, br_ref, xc_ref, route_ref, cnt_ref)


def _conv_in_kernel(rows_ref, x1_ref, route_ref, modp_ref, mod_ref, n1g_ref, w_ref, x2_ref, u_ref):
    x2 = x1_ref[...] + modp_ref[5:6, :] * _combine(route_ref[...], rows_ref[...])
    x2_ref[...] = x2
    h = _rms(x2, n1g_ref[...]) * (1.0 + mod_ref[1:2, :]) + mod_ref[0:1, :]
    z = jnp.dot(h.astype(BF16), w_ref[...], preferred_element_type=F32)
    u_ref[...] = z[:, :D] * _sigmoid(z[:, D:])


HALO = 16


def _conv_out_kernel(npt, tiles_per_seq, u_ref, up_ref, un_ref, wdw_ref, bdw_ref, lg_ref, lb_ref, w2_ref,
                     x_ref, mod_ref, n2g_ref, wr_ref, br_ref, x1_ref, xc_ref, route_ref, cnt_ref, ext_ref, sh_ref):
    i = pl.program_id(0)
    j = i - npt
    in_prompt = i < npt
    first = in_prompt | (j % tiles_per_seq == 0)
    last = in_prompt | (j % tiles_per_seq == tiles_per_seq - 1)
    ext_ref[0:HALO, :] = jnp.where(first, 0.0, up_ref[...])
    ext_ref[HALO:HALO + TM, :] = u_ref[...]
    ext_ref[HALO + TM:, :] = jnp.where(last, 0.0, un_ref[...])

    for b in range(SUBLANES):
        sh_ref[b] = ext_ref[b + 1:b + 1 + TM + 24, :]
    acc = jnp.zeros((TM, D), F32) + bdw_ref[...]
    for tap in range(CONV_K):
        a, b = divmod(tap, SUBLANES)
        acc = acc + wdw_ref[tap:tap + 1, :] * sh_ref[b, SUBLANES * a:SUBLANES * a + TM, :]
    mu = jnp.mean(acc, axis=-1, keepdims=True)
    var = jnp.mean(jnp.square(acc - mu), axis=-1, keepdims=True)
    ln = (acc - mu) * lax.rsqrt(var + 1e-5) * lg_ref[...] + lb_ref[...]
    act = (ln * _sigmoid(ln)).astype(BF16)
    o = jnp.dot(act, w2_ref[...], preferred_element_type=F32)
    x1 = x_ref[...] + mod_ref[2:3, :] * o
    x1_ref[...] = x1
    _moe_route(x1, mod_ref, n2g_ref[...], wr_ref, br_ref, xc_ref, route_ref, cnt_ref)


CPB = FB // CH


def _ffn_kernel(be_ref, nused_ref, nval_ref, src_ref, xc_in, win_ref, bin_ref, wout_ref, bout_ref, rows_hbm,
                win_s, wout_s, xbuf, ybuf, rsem, wsem):
    del xc_in
    i = pl.program_id(0)
    last = pl.num_programs(0) - 1
    e = be_ref[i]
    e_prev = be_ref[jnp.maximum(i - 1, 0)]

    def copy_in(blk, c):
        row = pl.multiple_of(src_ref[blk * CPB + c], CH)
        return pltpu.make_async_copy(rows_hbm.at[pl.ds(row, CH), :],
                                     xbuf.at[blk % 2, pl.ds(pl.multiple_of(c * CH, CH), CH), :], rsem.at[blk % 2])

    def copy_out(blk, c):
        row = pl.multiple_of(src_ref[blk * CPB + c], CH)
        return pltpu.make_async_copy(ybuf.at[blk % 2, pl.ds(pl.multiple_of(c * CH, CH), CH), :],
                                     rows_hbm.at[pl.ds(row, CH), :], wsem.at[blk % 2])

    def for_chunks(blk, fn):
        nv = nval_ref[blk]

        @pl.when(nv == CPB)
        def _():
            for c in range(CPB):
                fn(c)

        @pl.when(nv < CPB)
        def _():
            def body(c, carry):
                fn(c)
                return carry

            lax.fori_loop(0, nv, body, 0)

    @pl.when(i == 0)
    def _():
        xbuf[...] = jnp.zeros_like(xbuf)
        for_chunks(i, lambda c: copy_in(i, c).start())

    @pl.when(i < last)
    def _():
        for_chunks(i + 1, lambda c: copy_in(i + 1, c).start())

    @pl.when((i == 0) | (e != e_prev))
    def _():
        win_s[...] = win_ref[...].astype(BF16)
        wout_s[...] = wout_ref[...].astype(BF16)

    @pl.when(i >= 2)
    def _():
        for_chunks(i - 2, lambda c: copy_out(i - 2, c).wait())

    @pl.when(i < nused_ref[0])
    def _():
        for_chunks(i, lambda c: copy_in(i, c).wait())
        xa, xb = _unpack_pairs(xbuf[i % 2])
        hb = jnp.dot(xa, win_s[:D // 2, :], preferred_element_type=F32) \
            + jnp.dot(xb, win_s[D // 2:, :], preferred_element_type=F32) + bin_ref[...]
        g = jnp.minimum(hb[:, :D_FF], SWIGLU_LIMIT)
        u = jnp.clip(hb[:, D_FF:], -SWIGLU_LIMIT, SWIGLU_LIMIT)
        act = g * _sigmoid(SWIGLU_ALPHA * g) * (u + 1.0)
        y = jnp.dot(act.astype(BF16), wout_s[...], preferred_element_type=F32) + bout_ref[...]
        ybuf[i % 2] = _pack_pairs(y.astype(BF16).astype(F32))
        for_chunks(i, lambda c: copy_out(i, c).start())

    @pl.when(i == last)
    def _():
        @pl.when(i >= 1)
        def _():
            for_chunks(i - 1, lambda c: copy_out(i - 1, c).wait())

        for_chunks(i, lambda c: copy_out(i, c).wait())


def _expert_ffn(block_e, n_used, n_valid, src, xc, layer, w_in, b_in, w_out, b_out):
    n_blocks = block_e.shape[0]
    depth = w_in.shape[0]
    wspec = lambda shape: pl.BlockSpec((None, None) + shape, lambda i, be, *_: (layer, be[i], 0, 0))
    return pl.pallas_call(
        _ffn_kernel,
        grid_spec=pltpu.PrefetchScalarGridSpec(
            num_scalar_prefetch=4,
            grid=(n_blocks,),
            in_specs=[pl.BlockSpec(memory_space=pl.ANY), wspec((D, 2 * D_FF)), wspec((1, 2 * D_FF)),
                      wspec((D_FF, D)), wspec((1, D))],
            out_specs=pl.BlockSpec(memory_space=pl.ANY),
            scratch_shapes=[pltpu.VMEM((D, 2 * D_FF), BF16), pltpu.VMEM((D_FF, D), BF16),
                            pltpu.VMEM((2, FB, D // 2), U32), pltpu.VMEM((2, FB, D // 2), U32),
                            pltpu.SemaphoreType.DMA((2,)), pltpu.SemaphoreType.DMA((2,))],
        ),
        out_shape=jax.ShapeDtypeStruct(xc.shape, U32),
        input_output_aliases={4: 0},
        compiler_params=_cparams(("arbitrary",)),
        name="moe_expert_ffn",
    )(block_e, n_used, n_valid, src, xc, w_in, b_in.reshape(depth, N_EXPERTS, 1, 2 * D_FF), w_out,
      b_out.reshape(depth, N_EXPERTS, 1, D))


def _moe_plan(cnt):
    nt = cnt.shape[0]
    n_blocks = (nt * TM * TOP_K + nt * N_EXPERTS * (CH - 1)) // FB + N_EXPERTS + 1
    n = cnt[:, 0, :N_EXPERTS].astype(I32)
    nch = (n + CH - 1) // CH
    cend = jnp.cumsum(nch, axis=0)
    ce = cend[-1]
    pce = (ce + CPB - 1) // CPB * CPB
    e_end = jnp.cumsum(pce)
    e_start = e_end - pce
    first_slot = CH * (jnp.cumsum(nch, axis=1) - nch)

    blk0 = jnp.arange(n_blocks, dtype=I32) * CPB
    block_e = jnp.minimum(jnp.sum((blk0[:, None] >= e_end[None, :]).astype(I32), axis=-1), N_EXPERTS - 1)
    pick = lambda table, idx: jnp.sum(jnp.where(idx[:, None] == jnp.arange(table.shape[0], dtype=I32), table, 0), axis=-1)
    n_valid = jnp.clip(pick(e_start + ce, block_e) - blk0, 0, CPB)
    n_used = (e_end[-1] // CPB).reshape(1)

    d = jnp.arange(n_blocks * CPB, dtype=I32)
    e_of = jnp.repeat(block_e, CPB)
    local = d - pick(e_start, e_of)
    onehot_e = (e_of[:, None] == jnp.arange(N_EXPERTS, dtype=I32)).astype(F32)
    by_expert = lambda m: jnp.dot(onehot_e, m.T.astype(F32), precision=lax.Precision.HIGHEST).astype(I32)
    cend_e = by_expert(cend)
    in_tile = (local[:, None] >= cend_e - by_expert(nch)) & (local[:, None] < cend_e)
    tile_iota = jnp.arange(nt, dtype=I32)
    at_tile = lambda m: jnp.sum(jnp.where(in_tile, m, 0), axis=-1)
    src = at_tile(tile_iota * SLOTS + by_expert(first_slot) + CH * (local[:, None] - (cend_e - by_expert(nch))))
    return src.astype(I32), block_e.astype(I32), n_valid.astype(I32), n_used.astype(I32)


def _moe(cnt, xc, layer, w_in, b_in, w_out, b_out):
    src, block_e, n_valid, n_used = _moe_plan(cnt)
    return _expert_ffn(block_e, n_used, n_valid, src, xc, layer, w_in, b_in, w_out, b_out)


def _final_kernel(npt, rows_ref, x1_ref, route_ref, modp_ref, g_ref, op_ref, os_ref):
    i = pl.program_id(0)
    x2 = x1_ref[...] + modp_ref[5:6, :] * _combine(route_ref[...], rows_ref[...])
    out = _rms(x2, g_ref[...])

    @pl.when(i < npt)
    def _():
        op_ref[...] = out

    @pl.when(i >= npt)
    def _():
        os_ref[...] = out


def _rope_tables(n_tokens):
    rows = n_tokens // GRID_W
    pos_row = jnp.repeat(jnp.arange(rows, dtype=F32), GRID_W)
    pos_col = jnp.tile(jnp.arange(GRID_W, dtype=F32), rows)
    n_freq = ROPE_DIM // 4
    inv_freq = ROPE_BASE ** (-jnp.arange(n_freq, dtype=F32) / n_freq)
    ang = jnp.stack([pos_row[:, None] * inv_freq, pos_col[:, None] * inv_freq], axis=1)
    cos, sin = jnp.cos(ang), jnp.sin(ang)
    zero = jnp.zeros_like(sin)
    c = jnp.stack([cos, cos], axis=2).reshape(n_tokens, ROPE_DIM)
    sm = jnp.stack([-sin, zero], axis=2).reshape(n_tokens, ROPE_DIM)
    sp = jnp.stack([zero, sin], axis=2).reshape(n_tokens, ROPE_DIM)

    def place(a, fill):
        left = jnp.full((n_tokens, ROPE_LANE0), fill, F32)
        right = jnp.full((n_tokens, LANES - ROPE_LANE0 - ROPE_DIM), fill, F32)
        body = jnp.concatenate([left, a, right], axis=1)
        ident = jnp.full((TM, LANES), fill, F32)
        return jnp.concatenate([body, ident], axis=0)

    return jnp.stack([place(c, 1.0), place(sm, 0.0), place(sp, 0.0)], axis=0)


def kernel(x_prompt, x_sample, c, cache_mla_ckv, cache_mla_krope, state_mlstm_C, state_mlstm_n, state_mlstm_m, c_ctx, ada_w, ada_b, norm1_g, norm2_g, ab_w_in, mla_q_norm_g, mla_w_uq, mla_kv_norm_g, mla_w_ukv, mlstm_gate_b, mlstm_norm_g, ab_w_out, conv_w_pw1, conv_w_dw, conv_b_dw, conv_ln_g, conv_ln_b, conv_w_pw2, router_w, router_b, moe_w_in, moe_b_in, moe_w_out, moe_b_out, final_norm_g):
    bp, sp_, _ = x_prompt.shape
    bs, ss, _ = x_sample.shape
    past = cache_mla_ckv.shape[2]
    tp, ts = bp * sp_, bs * ss
    t = tp + ts
    nt = t // TM
    npt = tp // TM
    tps = ss // TM
    assert sp_ == TM and ss % TM == 0 and bs <= CTX_ROW and tp % ss == 0

    xp2 = x_prompt.reshape(tp, D)
    xs2 = x_sample.reshape(ts, D)
    cond = jnp.zeros((N_COND, D), F32).at[:bs].set(c).at[CTX_ROW].set(c_ctx)
    mod = _modulation(cond, ada_w, ada_b)

    def mod_row(i):
        return jnp.where(i < npt, CTX_ROW, (i - npt) // tps)

    def mod_spec(layer):
        return pl.BlockSpec((None, None, 6, D), lambda i, *_: (layer, mod_row(i), 0, 0))

    row_spec = lambda w: pl.BlockSpec((TM, w), lambda i, *_: (i, 0))
    prompt_spec = pl.BlockSpec((TM, D), lambda i, *_: (jnp.minimum(i, npt - 1), 0))
    sample_spec = pl.BlockSpec((TM, D), lambda i, *_: (jnp.maximum(i - npt, 0), 0))
    full = lambda a: pl.BlockSpec(a.shape, lambda i, *_: (0,) * a.ndim)
    vec = lambda a: a.reshape(1, -1)
    any_spec = pl.BlockSpec(memory_space=pl.ANY)

    def router_args(layer):
        wr = jnp.pad(router_w[layer], ((0, 0), (0, LANES - N_EXPERTS))).astype(BF16)
        br = jnp.pad(router_b[layer], (0, LANES - N_EXPERTS)).reshape(1, LANES)
        return wr, br

    moe_out_shapes = [
        jax.ShapeDtypeStruct((t, D), F32),
        jax.ShapeDtypeStruct((nt * SLOTS, D // 2), U32),
        jax.ShapeDtypeStruct((t, LANES), F32),
        jax.ShapeDtypeStruct((nt, SUBLANES, LANES), F32),
    ]
    moe_out_specs = [row_spec(D), pl.BlockSpec((SLOTS, D // 2), lambda i: (i, 0)), row_spec(LANES),
                     pl.BlockSpec((None, SUBLANES, LANES), lambda i: (i, 0, 0))]

    j = 0
    w = ab_w_in[j]
    zc = lambda n: jnp.zeros((D, n), F32)
    misc_w = jnp.concatenate([w[:, 2464:2480], zc(ROPE_LANE0 - 16), w[:, 384:416], zc(LANES - ROPE_LANE0 - ROPE_DIM)], axis=1)
    w_in = jnp.concatenate([w[:, :384], misc_w, w[:, 416:2464]], axis=1).astype(BF16)
    wq = jnp.pad(mla_w_uq[j].reshape(Q_RANK, H_A, NOPE + ROPE_DIM), ((0, 0), (0, 0), (0, HEAD_PAD - NOPE - ROPE_DIM)))
    wq = wq.reshape(Q_RANK, H_A * HEAD_PAD).astype(BF16)
    wkv = mla_w_ukv[j].reshape(KV_RANK, H_A, NOPE + V_A)
    wk = jnp.pad(wkv[:, :, :NOPE], ((0, 0), (0, 0), (0, HEAD_PAD - NOPE))).reshape(KV_RANK, H_A * HEAD_PAD).astype(BF16)
    wv = jnp.pad(wkv[:, :, NOPE:], ((0, 0), (0, 0), (0, HEAD_PAD - V_A))).reshape(KV_RANK, H_A * HEAD_PAD).astype(BF16)
    gate_b = jnp.pad(mlstm_gate_b[j], (0, LANES - 4 * H_B)).reshape(1, LANES)
    tabs = _rope_tables(ss)
    pos_blocks = ss // TM

    def tab_block(i):
        return jnp.where(i < npt, pos_blocks, (i - npt) % tps)

    nctx = bs * past
    ctx_ckv = cache_mla_ckv[:, j].reshape(nctx, KV_RANK)
    ctx_kr = jnp.pad(cache_mla_krope[:, j].reshape(nctx, ROPE_DIM), ((0, 0), (ROPE_LANE0, LANES - ROPE_LANE0 - ROPE_DIM)))
    nct, cps, nts = nctx // TM, past // TM, ts // TM
    kps = cps + tps
    kv_rows = bs * (past + ss) + tp

    def kv_block(i):
        own = ((i - npt) // tps) * kps + cps + (i - npt) % tps
        cached = ((i - nt) // cps) * kps + (i - nt) % cps
        return jnp.where(i < npt, bs * kps + i, jnp.where(i < nt, own, cached))

    tile_spec = lambda w: pl.BlockSpec((TM, w), lambda i: (jnp.minimum(i, nt - 1), 0))
    ctx_spec = pl.BlockSpec((TM, LANES), lambda i: (jnp.clip(i - nt, 0, nct - 1), 0))
    kv_spec = pl.BlockSpec((TM, H_A * HEAD_PAD), lambda i: (kv_block(i), 0))
    ins = [xp2, xs2, ctx_ckv, ctx_kr, mod, vec(norm1_g[0]), w_in, vec(mla_q_norm_g[j]), wq, vec(mla_kv_norm_g[j]), wk, wv,
           gate_b, tabs]
    in_specs = [prompt_spec, pl.BlockSpec((TM, D), lambda i: (jnp.clip(i - npt, 0, nts - 1), 0)), ctx_spec, ctx_spec,
                mod_spec(0)] + [full(a) for a in ins[5:13]] + [
        pl.BlockSpec((3, TM, LANES), lambda i: (0, tab_block(i), 0))]
    q, k, v, ckv, misc, mqkv, mo = pl.pallas_call(
        functools.partial(_ab_in_kernel, npt, nt),
        grid=(nt + nct,),
        in_specs=in_specs,
        out_specs=[tile_spec(H_A * HEAD_PAD), kv_spec, kv_spec, tile_spec(KV_RANK), tile_spec(LANES),
                   tile_spec(3 * MB), tile_spec(MB)],
        out_shape=[
            jax.ShapeDtypeStruct((t, H_A * HEAD_PAD), BF16), jax.ShapeDtypeStruct((kv_rows, H_A * HEAD_PAD), BF16),
            jax.ShapeDtypeStruct((kv_rows, H_A * HEAD_PAD), BF16), jax.ShapeDtypeStruct((t, KV_RANK), F32),
            jax.ShapeDtypeStruct((t, LANES), F32), jax.ShapeDtypeStruct((t, 3 * MB), BF16),
            jax.ShapeDtypeStruct((t, MB), F32)],
        compiler_params=_cparams(("arbitrary",)),
        name="ab_in_proj",
    )(*ins)

    oa_p = _attention(q, k, v, 0, bp, sp_, bs * (past + ss), sp_)
    oa_s = _attention(q, k, v, tp, bs, ss, 0, past + ss)

    zc0 = jnp.zeros((bp, 2, H_B, DH_B, DH_B), F32)
    zn0 = jnp.zeros((bp, 2, H_B, 1, DH_B), F32)
    zm0 = jnp.zeros((bp, 2, H_B, 1, LANES), F32)
    hf_p, hb_p, c_p, n_p, m_p = _mlstm(mqkv, misc, zc0, zn0, zm0, 0, bp, sp_)
    sm0 = jnp.broadcast_to(state_mlstm_m[:, j][..., None, None], (bs, 2, H_B, 1, LANES))
    hf_s, hb_s, _, _, _ = _mlstm(mqkv, misc, state_mlstm_C[:, j], state_mlstm_n[:, j][:, :, :, None, :], sm0, tp, bs, ss)

    wr, br = router_args(0)
    prompt_w = lambda w: pl.BlockSpec((TM, w), lambda i: (jnp.minimum(i, npt - 1), 0))
    sample_w = lambda w: pl.BlockSpec((TM, w), lambda i: (jnp.maximum(i - npt, 0), 0))
    ins = [oa_p, oa_s, hf_p, hf_s, hb_p, hb_s, mo, vec(mlstm_norm_g[j]), ab_w_out[j].astype(BF16), xp2, xs2, mod,
           vec(norm2_g[0]), wr, br]
    in_specs = [prompt_w(H_A * V_A), sample_w(H_A * V_A), prompt_w(MB), sample_w(MB), prompt_w(MB), sample_w(MB),
                row_spec(MB), full(ins[7]), full(ins[8]),
                prompt_spec, sample_spec, mod_spec(0), full(ins[12]), full(wr), full(br)]
    x1, xc, route, cnt = pl.pallas_call(
        functools.partial(_ab_out_kernel, npt),
        grid=(nt,),
        in_specs=in_specs,
        out_specs=moe_out_specs,
        out_shape=moe_out_shapes,
        compiler_params=_cparams(("arbitrary",)),
        name="ab_out_proj",
    )(*ins)
    ys = _moe(cnt, xc, 0, moe_w_in, moe_b_in, moe_w_out, moe_b_out)

    slots_spec = pl.BlockSpec((SLOTS, D // 2), lambda i, *_: (i, 0))
    ins = [ys, x1, route, mod, mod, vec(norm1_g[1]), conv_w_pw1[0].astype(BF16)]
    x2, u = pl.pallas_call(
        _conv_in_kernel,
        grid=(nt,),
        in_specs=[slots_spec, row_spec(D), row_spec(LANES), mod_spec(0), mod_spec(1), full(ins[5]), full(ins[6])],
        out_specs=[row_spec(D), row_spec(D)],
        out_shape=[jax.ShapeDtypeStruct((t, D), F32), jax.ShapeDtypeStruct((t, D), F32)],
        compiler_params=_cparams(("arbitrary",)),
        name="conv_in_proj",
    )(*ins)

    hpt = TM // HALO
    nhalo = t // HALO
    wr, br = router_args(1)
    ins = [u, u, u, conv_w_dw[0], vec(conv_b_dw[0]), vec(conv_ln_g[0]), vec(conv_ln_b[0]), conv_w_pw2[0].astype(BF16),
           x2, mod, vec(norm2_g[1]), wr, br]
    in_specs = [row_spec(D),
                pl.BlockSpec((HALO, D), lambda i: (jnp.maximum(i * hpt - 1, 0), 0)),
                pl.BlockSpec((HALO, D), lambda i: (jnp.minimum((i + 1) * hpt, nhalo - 1), 0)),
                full(ins[3]), full(ins[4]), full(ins[5]), full(ins[6]), full(ins[7]), row_spec(D), mod_spec(1),
                full(ins[10]), full(wr), full(br)]
    x1, xc, route, cnt = pl.pallas_call(
        functools.partial(_conv_out_kernel, npt, tps),
        grid=(nt,),
        in_specs=in_specs,
        out_specs=moe_out_specs,
        out_shape=moe_out_shapes,
        scratch_shapes=[pltpu.VMEM((TM + 2 * HALO, D), F32), pltpu.VMEM((SUBLANES, TM + 24, D), F32)],
        compiler_params=_cparams(("arbitrary",)),
        name="conv_out_proj",
    )(*ins)
    ys = _moe(cnt, xc, 1, moe_w_in, moe_b_in, moe_w_out, moe_b_out)

    ins = [ys, x1, route, mod, vec(final_norm_g)]
    y_p, y_s = pl.pallas_call(
        functools.partial(_final_kernel, npt),
        grid=(nt,),
        in_specs=[slots_spec, row_spec(D), row_spec(LANES), mod_spec(1), full(ins[4])],
        out_specs=[prompt_spec, sample_spec],
        out_shape=[jax.ShapeDtypeStruct((tp, D), F32), jax.ShapeDtypeStruct((ts, D), F32)],
        compiler_params=_cparams(("arbitrary",)),
        name="final_norm",
    )(*ins)

    y_prompt = y_p.reshape(bp, sp_, D)
    y_sample = y_s.reshape(bs, ss, D)
    new_ckv = ckv[:tp].reshape(bp, 1, sp_, KV_RANK)
    new_krope = misc[:tp, ROPE_LANE0:ROPE_LANE0 + ROPE_DIM].reshape(bp, 1, sp_, ROPE_DIM)
    new_c = c_p[:, None]
    new_n = n_p[:, None, :, :, 0, :]
    new_m = m_p[:, None, :, :, 0, 0]
    return (y_prompt, y_sample, new_ckv, new_krope, new_c, new_n, new_m)
```

```python
import functools
import math

import jax
import jax.numpy as jnp
from jax import lax
from jax.experimental import pallas as pl
from jax.experimental.pallas import tpu as pltpu

F32 = jnp.float32
BF16 = jnp.bfloat16
I32 = jnp.int32
U32 = jnp.uint32

D = 1024
GRID_W = 64
H_A = 8
Q_RANK = 256
KV_RANK = 128
NOPE = 64
ROPE_DIM = 32
V_A = 64
ROPE_BASE = 10000.0
H_B = 4
DH_B = 128
MB = H_B * DH_B
CONV_K = 31
N_EXPERTS = 32
TOP_K = 4
D_FF = 1024
SWIGLU_LIMIT = 7.0
SWIGLU_ALPHA = 1.702

LANES = 128
SUBLANES = 8
HEAD_PAD = 128
ROPE_LANE0 = 64
N_COND = 16
CTX_ROW = 8

TM = 256
FB = 512
LC = 256
TQ = 256
HPS = 4
CH = SUBLANES
SLOTS = TOP_K * TM + N_EXPERTS * CH
VMEM_LIMIT = 56 * 1024 * 1024

NEG_INF = float("-inf")
LOG2_E = math.log2(math.e)


def _cparams(sem):
    return pltpu.CompilerParams(dimension_semantics=sem, vmem_limit_bytes=VMEM_LIMIT)


def _sigmoid(x):
    return 1.0 / (1.0 + jnp.exp(-x))


def _rms(x, g, eps=1e-6):
    return x * lax.rsqrt(jnp.mean(x * x, axis=-1, keepdims=True) + eps) * g


def _lane(shape):
    return lax.broadcasted_iota(I32, shape, len(shape) - 1)


def _pack_pairs(x):
    w = x.shape[1] // 2
    return pltpu.bitcast(x[:, :w], U32) | (pltpu.bitcast(x[:, w:], U32) >> 16)


def _unpack_pairs(wd):
    a = pltpu.bitcast(wd & jnp.uint32(0xFFFF0000), F32).astype(BF16)
    b = pltpu.bitcast(wd << 16, F32).astype(BF16)
    return a, b


def _mod_kernel(cond_ref, w_ref, b_ref, o_ref):
    c = cond_ref[...]
    s = (c * _sigmoid(c)).astype(BF16)
    o_ref[...] = jnp.dot(s, w_ref[...].astype(BF16), preferred_element_type=F32) + b_ref[...]


def _modulation(cond, ada_w, ada_b):
    depth = ada_w.shape[0]
    nj = ada_w.shape[2] // D
    out = pl.pallas_call(
        _mod_kernel,
        grid=(depth, nj),
        in_specs=[
            pl.BlockSpec((N_COND, D), lambda l, j: (0, 0)),
            pl.BlockSpec((None, D, D), lambda l, j: (l, 0, j)),
            pl.BlockSpec((None, 1, D), lambda l, j: (l, 0, j)),
        ],
        out_specs=pl.BlockSpec((None, N_COND, D), lambda l, j: (l, 0, j)),
        out_shape=jax.ShapeDtypeStruct((depth, N_COND, nj * D), F32),
        compiler_params=_cparams(("arbitrary", "arbitrary")),
        name="ada_modulation",
    )(cond, ada_w, ada_b.reshape(depth, 1, nj * D))
    return out.reshape(depth, N_COND, nj, D)


def _combine(route, rows):
    lane = _lane((route.shape[0], SLOTS)).astype(F32)
    g = jnp.zeros(lane.shape, F32)
    for k in range(TOP_K):
        g = jnp.where(lane == route[:, 8 + k:9 + k], route[:, 4 + k:5 + k], g)
    gb = g.astype(BF16)
    ya, yb = _unpack_pairs(rows)
    return jnp.concatenate([jnp.dot(gb, ya, preferred_element_type=F32),
                            jnp.dot(gb, yb, preferred_element_type=F32)], axis=-1)


def _moe_route(x1, mod_ref, n2g, wr_ref, br_ref, xc_ref, route_ref, cnt_ref):
    h2 = _rms(x1, n2g) * (1.0 + mod_ref[4:5, :]) + mod_ref[3:4, :]
    h2b = h2.astype(BF16)
    logits = jnp.dot(h2b, wr_ref[...], preferred_element_type=F32) + br_ref[...]
    tm = logits.shape[0]
    lane = _lane((tm, LANES))
    lanef = lane.astype(F32)
    l = jnp.where(lane < N_EXPERTS, logits, NEG_INF)
    tops, idxs, sels = [], [], []
    for _ in range(TOP_K):
        mv = jnp.max(l, axis=-1, keepdims=True)
        idx = jnp.min(jnp.where(l == mv, lanef, float(LANES)), axis=-1, keepdims=True)
        sel = lanef == idx
        l = jnp.where(sel, NEG_INF, l)
        tops.append(mv)
        idxs.append(idx)
        sels.append(sel)
    ex = [jnp.exp(t - tops[0]) for t in tops]
    den = ex[0] + ex[1] + ex[2] + ex[3]
    gates = [e / den for e in ex]

    onehot = jnp.zeros((tm, LANES), F32)
    for sel in sels:
        onehot = onehot + jnp.where(sel, 1.0, 0.0)
    row = lax.broadcasted_iota(I32, (tm, tm), 0)
    col = lax.broadcasted_iota(I32, (tm, tm), 1)
    strict = jnp.where(col < row, 1.0, 0.0).astype(BF16)
    before = jnp.dot(strict, onehot.astype(BF16), preferred_element_type=F32)
    n_e = jnp.sum(onehot, axis=0, keepdims=True)
    cnt_ref[...] = jnp.broadcast_to(n_e, cnt_ref.shape)
    nch = jnp.right_shift(n_e.astype(I32) + (CH - 1), CH.bit_length() - 1).astype(F32)
    r128 = lax.broadcasted_iota(I32, (LANES, LANES), 0)
    c128 = lax.broadcasted_iota(I32, (LANES, LANES), 1)
    upper = jnp.where(r128 < c128, 1.0, 0.0).astype(BF16)
    seg0 = float(CH) * jnp.dot(jnp.broadcast_to(nch, (SUBLANES, LANES)).astype(BF16), upper,
                               preferred_element_type=F32)[0:1, :]
    slot_of = seg0 + before
    slots = [jnp.sum(jnp.where(sel, slot_of, 0.0), axis=-1, keepdims=True) for sel in sels]

    r = jnp.zeros((tm, LANES), F32)
    for j, colv in enumerate(idxs + gates + slots):
        r = jnp.where(lane == j, colv, r)
    route_ref[...] = r

    lane_s = _lane((tm, SLOTS)).astype(F32)
    pt = jnp.zeros(lane_s.shape, F32)
    for k in range(TOP_K):
        pt = jnp.where(lane_s == slots[k], 1.0, pt)
    xc = jnp.dot(pt.T.astype(BF16), h2b, preferred_element_type=F32)
    xc_ref[...] = _pack_pairs(xc)


def _values_with_ones(ckvb, wv_ref):
    vv = jnp.dot(ckvb, wv_ref[...], preferred_element_type=F32)
    return jnp.where((_lane(vv.shape) & (HEAD_PAD - 1)) == V_A, 1.0, vv).astype(BF16)


def _rope(x, tab_ref):
    return x * tab_ref[0] + pltpu.roll(x, LANES - 8, 1) * tab_ref[1] + pltpu.roll(x, 8, 1) * tab_ref[2]


def _ab_in_kernel(npt, nt, *refs):
    @pl.when(pl.program_id(0) < nt)
    def _():
        _ab_in_tokens(npt, *refs)

    @pl.when(pl.program_id(0) >= nt)
    def _():
        _ab_in_context(*refs)


def _ab_in_context(xp_ref, xs_ref, cckv_ref, ckr_ref, mod_ref, n1g_ref, win_ref, qg_ref, wq_ref, kvg_ref, wk_ref, wv_ref,
                   gb_ref, tab_ref, q_ref, k_ref, v_ref, ckv_ref, misc_ref, mqkv_ref, mo_ref):
    ckvb = cckv_ref[...].astype(BF16)
    kn = jnp.dot(ckvb, wk_ref[...], preferred_element_type=F32)
    v_ref[...] = _values_with_ones(ckvb, wv_ref)
    kr = ckr_ref[...]
    for hh in range(H_A):
        sl = slice(hh * HEAD_PAD, (hh + 1) * HEAD_PAD)
        k_ref[:, sl] = (kn[:, sl] + kr).astype(BF16)


def _ab_in_tokens(npt, xp_ref, xs_ref, cckv_ref, ckr_ref, mod_ref, n1g_ref, win_ref, qg_ref, wq_ref, kvg_ref, wk_ref,
                  wv_ref, gb_ref, tab_ref, q_ref, k_ref, v_ref, ckv_ref, misc_ref, mqkv_ref, mo_ref):
    x = jnp.where(pl.program_id(0) < npt, xp_ref[...], xs_ref[...])
    h = _rms(x, n1g_ref[...]) * (1.0 + mod_ref[1:2, :]) + mod_ref[0:1, :]
    z = jnp.dot(h.astype(BF16), win_ref[...], preferred_element_type=F32)

    qn = _rms(z[:, :Q_RANK], qg_ref[...]).astype(BF16)
    q = jnp.dot(qn, wq_ref[...], preferred_element_type=F32)
    ckv = _rms(z[:, Q_RANK:Q_RANK + KV_RANK], kvg_ref[...])
    ckv_ref[...] = ckv
    ckvb = ckv.astype(BF16)
    kn = jnp.dot(ckvb, wk_ref[...], preferred_element_type=F32)
    v_ref[...] = _values_with_ones(ckvb, wv_ref)

    misc = z[:, Q_RANK + KV_RANK:Q_RANK + KV_RANK + LANES]
    misc_ref[...] = misc + gb_ref[...]
    lane = _lane(misc.shape)
    kr = jnp.where((lane >= ROPE_LANE0) & (lane < ROPE_LANE0 + ROPE_DIM), _rope(misc, tab_ref), 0.0)
    scale = float((NOPE + ROPE_DIM) ** -0.5 * LOG2_E)
    for hh in range(H_A):
        sl = slice(hh * HEAD_PAD, (hh + 1) * HEAD_PAD)
        q_ref[:, sl] = (_rope(q[:, sl], tab_ref) * scale).astype(BF16)
        k_ref[:, sl] = (kn[:, sl] + kr).astype(BF16)

    m0 = Q_RANK + KV_RANK + LANES
    mqkv_ref[:, :MB] = z[:, m0:m0 + MB].astype(BF16)
    mqkv_ref[:, MB:2 * MB] = (z[:, m0 + MB:m0 + 2 * MB] * float(DH_B ** -0.5)).astype(BF16)
    mqkv_ref[:, 2 * MB:] = z[:, m0 + 2 * MB:m0 + 3 * MB].astype(BF16)
    mo_ref[...] = z[:, m0 + 3 * MB:m0 + 4 * MB]


def _attn_kernel(q_ref, k_ref, v_ref, o_ref):
    outs = []
    for j in range(HPS):
        sl = slice(j * HEAD_PAD, (j + 1) * HEAD_PAD)
        s = lax.dot_general(q_ref[:, sl], k_ref[:, sl], (((1,), (1,)), ((), ())), preferred_element_type=F32)
        m = jnp.max(s, axis=-1, keepdims=True)
        p = jnp.exp2(s - m).astype(BF16)
        o = jnp.dot(p, v_ref[:, sl], preferred_element_type=F32)
        outs.append(o / o[:, V_A:V_A + 1])
    lane = _lane(outs[0].shape)
    for j in range(0, HPS, 2):
        o_ref[:, j * V_A:(j + 2) * V_A] = jnp.where(lane < V_A, outs[j], pltpu.roll(outs[j + 1], V_A, 1)).astype(BF16)


def _attention(q, k, v, qrow0, nb, s, krow0, l):
    tq = min(TQ, s)
    nq = s // tq
    qrow = lambda bi, hp, qi: (qrow0 // tq + bi * nq + qi, hp)
    krow = lambda bi, hp, qi: (krow0 // l + bi, hp)
    return pl.pallas_call(
        _attn_kernel,
        grid=(nb, H_A // HPS, nq),
        in_specs=[pl.BlockSpec((tq, HPS * HEAD_PAD), qrow), pl.BlockSpec((l, HPS * HEAD_PAD), krow),
                  pl.BlockSpec((l, HPS * HEAD_PAD), krow)],
        out_specs=pl.BlockSpec((tq, HPS * V_A), lambda bi, hp, qi: (bi * nq + qi, hp)),
        out_shape=jax.ShapeDtypeStruct((nb * s, H_A * V_A), BF16),
        compiler_params=_cparams(("arbitrary", "arbitrary", "arbitrary")),
        name="mla_attention",
    )(q, k, v)


def _log_sigmoid(x):
    return jnp.minimum(x, 0.0) - jnp.log(1.0 + jnp.exp(-jnp.abs(x)))


def _mlstm_direction(d, q_ref, k_ref, v_ref, g_ref, h_ref, c_s, n_s, m_s):
    gates = g_ref[...]
    lf_all = _log_sigmoid(gates)
    row = lax.broadcasted_iota(I32, (LC, LC), 0)
    col = lax.broadcasted_iota(I32, (LC, LC), 1)
    mask = (col <= row) if d == 0 else (col >= row)
    eye = row == col
    ones = jnp.where(mask, 1.0, 0.0).astype(BF16)
    cum = jnp.zeros((LC, LANES), F32)
    rest = lf_all
    for _ in range(3):
        piece = rest.astype(BF16)
        cum = cum + jnp.dot(ones, piece, preferred_element_type=F32)
        rest = rest - piece.astype(F32)

    for hh in range(H_B):
        sl = slice(hh * DH_B, (hh + 1) * DH_B)
        q = q_ref[:, sl]
        k = k_ref[:, sl]
        v = v_ref[:, sl]
        b_col = cum[:, 8 * d + 4 + hh:8 * d + 5 + hh]
        i_col = gates[:, 8 * d + hh:8 * d + hh + 1]
        m_prev = m_s[d, hh][:, 0:1]
        r_row = jnp.sum(jnp.where(eye, i_col - b_col, 0.0), axis=0, keepdims=True)
        dm = jnp.where(mask, b_col + r_row, NEG_INF)
        a_col = b_col + m_prev
        mt = jnp.maximum(a_col, jnp.max(dm, axis=-1, keepdims=True))
        w_inter = jnp.exp(a_col - mt)
        qk = lax.dot_general(q, k, (((1,), (1,)), ((), ())), preferred_element_type=F32)
        s = qk * jnp.exp(dm - mt)
        c_prev = c_s[d, hh]
        n_prev = n_s[d, hh]
        num = w_inter * jnp.dot(q, c_prev.astype(BF16), preferred_element_type=F32) \
            + jnp.dot(s.astype(BF16), v, preferred_element_type=F32)
        n_rows = jnp.broadcast_to(n_prev, (SUBLANES, DH_B)).astype(BF16)
        qn = lax.dot_general(q, n_rows, (((1,), (1,)), ((), ())), preferred_element_type=F32)[:, 0:1]
        den = w_inter * qn + jnp.sum(s, axis=-1, keepdims=True)
        h_ref[:, sl] = num / jnp.maximum(jnp.abs(den), jnp.exp(-mt))

        b_tot = b_col[LC - 1:LC, :] if d == 0 else b_col[0:1, :]
        g_col = b_tot - b_col + i_col
        m_new = jnp.maximum(b_tot + m_prev, jnp.max(g_col, axis=0, keepdims=True))
        w_c = jnp.exp(b_tot + m_prev - m_new)
        kw = k.astype(F32) * jnp.exp(g_col - m_new)
        c_s[d, hh] = w_c * c_prev + jnp.dot(kw.T.astype(BF16), v, preferred_element_type=F32)
        n_s[d, hh] = w_c * n_prev + jnp.sum(kw, axis=0, keepdims=True)
        m_s[d, hh] = jnp.broadcast_to(m_new, (1, LANES))


def _mlstm_kernel(qf_ref, kf_ref, vf_ref, gf_ref, qb_ref, kb_ref, vb_ref, gb_ref, c0_ref, n0_ref, m0_ref,
                  hf_ref, hb_ref, c_out, n_out, m_out, c_s, n_s, m_s):
    c = pl.program_id(1)

    @pl.when(c == 0)
    def _():
        c_s[...] = c0_ref[...]
        n_s[...] = n0_ref[...]
        m_s[...] = m0_ref[...]

    _mlstm_direction(0, qf_ref, kf_ref, vf_ref, gf_ref, hf_ref, c_s, n_s, m_s)
    _mlstm_direction(1, qb_ref, kb_ref, vb_ref, gb_ref, hb_ref, c_s, n_s, m_s)

    @pl.when(c == pl.num_programs(1) - 1)
    def _():
        c_out[...] = c_s[...]
        n_out[...] = n_s[...]
        m_out[...] = m_s[...]


def _mlstm(mqkv, misc, c0, n0, m0, row0, nb, s):
    nc = s // LC
    blk0 = row0 // LC
    fwd = lambda bi, ci: blk0 + bi * nc + ci
    bwd = lambda bi, ci: blk0 + bi * nc + nc - 1 - ci
    chunk_specs = lambda rows: [
        pl.BlockSpec((LC, MB), lambda bi, ci: (rows(bi, ci), 0)),
        pl.BlockSpec((LC, MB), lambda bi, ci: (rows(bi, ci), 1)),
        pl.BlockSpec((LC, MB), lambda bi, ci: (rows(bi, ci), 2)),
        pl.BlockSpec((LC, LANES), lambda bi, ci: (rows(bi, ci), 0)),
    ]
    state_spec = lambda shape: pl.BlockSpec((None,) + shape, lambda bi, ci: (bi,) + (0,) * len(shape))
    state_specs = [state_spec((2, H_B, DH_B, DH_B)), state_spec((2, H_B, 1, DH_B)), state_spec((2, H_B, 1, LANES))]
    return pl.pallas_call(
        _mlstm_kernel,
        grid=(nb, nc),
        in_specs=chunk_specs(fwd) + chunk_specs(bwd) + state_specs,
        out_specs=[
            pl.BlockSpec((LC, MB), lambda bi, ci: (bi * nc + ci, 0)),
            pl.BlockSpec((LC, MB), lambda bi, ci: (bi * nc + nc - 1 - ci, 0)),
        ] + state_specs,
        out_shape=[
            jax.ShapeDtypeStruct((nb * s, MB), F32),
            jax.ShapeDtypeStruct((nb * s, MB), F32),
            jax.ShapeDtypeStruct((nb, 2, H_B, DH_B, DH_B), F32),
            jax.ShapeDtypeStruct((nb, 2, H_B, 1, DH_B), F32),
            jax.ShapeDtypeStruct((nb, 2, H_B, 1, LANES), F32),
        ],
        scratch_shapes=[
            pltpu.VMEM((2, H_B, DH_B, DH_B), F32),
            pltpu.VMEM((2, H_B, 1, DH_B), F32),
            pltpu.VMEM((2, H_B, 1, LANES), F32),
        ],
        compiler_params=_cparams(("arbitrary", "arbitrary")),
        name="mlstm_scan",
    )(mqkv, mqkv, mqkv, misc, mqkv, mqkv, mqkv, misc, c0, n0, m0)


def _ab_out_kernel(npt, oap_ref, oas_ref, hfp_ref, hfs_ref, hbp_ref, hbs_ref, mo_ref, mg_ref, wout_ref, xp_ref, xs_ref,
                   mod_ref, n2g_ref, wr_ref, br_ref, x1_ref, xc_ref, route_ref, cnt_ref):
    in_prompt = pl.program_id(0) < npt
    pick = lambda p_ref, s_ref: jnp.where(in_prompt, p_ref[...], s_ref[...])
    hsum = pick(hfp_ref, hfs_ref) + pick(hbp_ref, hbs_ref)
    parts = []
    for hh in range(H_B):
        hh_ = hsum[:, hh * DH_B:(hh + 1) * DH_B]
        mu = jnp.mean(hh_, axis=-1, keepdims=True)
        var = jnp.mean(jnp.square(hh_ - mu), axis=-1, keepdims=True)
        parts.append((hh_ - mu) * lax.rsqrt(var + 1e-5))
    hn = jnp.concatenate(parts, axis=-1) * mg_ref[...]
    ob = (hn * _sigmoid(mo_ref[...])).astype(BF16)
    o = jnp.dot(pick(oap_ref, oas_ref), wout_ref[:H_A * V_A, :], preferred_element_type=F32) \
        + jnp.dot(ob, wout_ref[H_A * V_A:, :], preferred_element_type=F32)
    x = pick(xp_ref, xs_ref)
    x1 = x + mod_ref[2:3, :] * o
    x1_ref[...] = x1
    _moe_route(x1, mod_ref, n2g_ref[...], wr_ref, br_ref, xc_ref, route_ref, cnt_ref)


def _conv_in_kernel(rows_ref, x1_ref, route_ref, modp_ref, mod_ref, n1g_ref, w_ref, x2_ref, u_ref):
    x2 = x1_ref[...] + modp_ref[5:6, :] * _combine(route_ref[...], rows_ref[...])
    x2_ref[...] = x2
    h = _rms(x2, n1g_ref[...]) * (1.0 + mod_ref[1:2, :]) + mod_ref[0:1, :]
    z = jnp.dot(h.astype(BF16), w_ref[...], preferred_element_type=F32)
    u_ref[...] = z[:, :D] * _sigmoid(z[:, D:])


HALO = 16


def _conv_out_kernel(npt, tiles_per_seq, u_ref, up_ref, un_ref, wdw_ref, bdw_ref, lg_ref, lb_ref, w2_ref,
                     x_ref, mod_ref, n2g_ref, wr_ref, br_ref, x1_ref, xc_ref, route_ref, cnt_ref, ext_ref, sh_ref):
    i = pl.program_id(0)
    j = i - npt
    in_prompt = i < npt
    first = in_prompt | (j % tiles_per_seq == 0)
    last = in_prompt | (j % tiles_per_seq == tiles_per_seq - 1)
    ext_ref[0:HALO, :] = jnp.where(first, 0.0, up_ref[...])
    ext_ref[HALO:HALO + TM, :] = u_ref[...]
    ext_ref[HALO + TM:, :] = jnp.where(last, 0.0, un_ref[...])

    for b in range(SUBLANES):
        sh_ref[b] = ext_ref[b + 1:b + 1 + TM + 24, :]
    acc = jnp.zeros((TM, D), F32) + bdw_ref[...]
    for tap in range(CONV_K):
        a, b = divmod(tap, SUBLANES)
        acc = acc + wdw_ref[tap:tap + 1, :] * sh_ref[b, SUBLANES * a:SUBLANES * a + TM, :]
    mu = jnp.mean(acc, axis=-1, keepdims=True)
    var = jnp.mean(jnp.square(acc - mu), axis=-1, keepdims=True)
    ln = (acc - mu) * lax.rsqrt(var + 1e-5) * lg_ref[...] + lb_ref[...]
    act = (ln * _sigmoid(ln)).astype(BF16)
    o = jnp.dot(act, w2_ref[...], preferred_element_type=F32)
    x1 = x_ref[...] + mod_ref[2:3, :] * o
    x1_ref[...] = x1
    _moe_route(x1, mod_ref, n2g_ref[...], wr_ref, br_ref, xc_ref, route_ref, cnt_ref)


CPB = FB // CH


def _ffn_kernel(be_ref, nused_ref, nval_ref, src_ref, xc_in, win_ref, bin_ref, wout_ref, bout_ref, rows_hbm,
                win_s, wout_s, xbuf, ybuf, rsem, wsem):
    del xc_in
    i = pl.program_id(0)
    last = pl.num_programs(0) - 1
    e = be_ref[i]
    e_prev = be_ref[jnp.maximum(i - 1, 0)]

    def copy_in(blk, c):
        row = pl.multiple_of(src_ref[blk * CPB + c], CH)
        return pltpu.make_async_copy(rows_hbm.at[pl.ds(row, CH), :],
                                     xbuf.at[blk % 2, pl.ds(pl.multiple_of(c * CH, CH), CH), :], rsem.at[blk % 2])

    def copy_out(blk, c):
        row = pl.multiple_of(src_ref[blk * CPB + c], CH)
        return pltpu.make_async_copy(ybuf.at[blk % 2, pl.ds(pl.multiple_of(c * CH, CH), CH), :],
                                     rows_hbm.at[pl.ds(row, CH), :], wsem.at[blk % 2])

    def for_chunks(blk, fn):
        nv = nval_ref[blk]

        @pl.when(nv == CPB)
        def _():
            for c in range(CPB):
                fn(c)

        @pl.when(nv < CPB)
        def _():
            def body(c, carry):
                fn(c)
                return carry

            lax.fori_loop(0, nv, body, 0)

    def wait_chunks(blk, chunk_copy, whole_copy):
        nv = nval_ref[blk]

        @pl.when(nv == CPB)
        def _():
            whole_copy.wait()

        @pl.when(nv < CPB)
        def _():
            def body(c, carry):
                chunk_copy(blk, c).wait()
                return carry

            lax.fori_loop(0, nv, body, 0)

    def wait_reads(blk):
        wait_chunks(blk, copy_in, pltpu.make_async_copy(rows_hbm.at[pl.ds(0, FB), :], xbuf.at[blk % 2], rsem.at[blk % 2]))

    def wait_writes(blk):
        wait_chunks(blk, copy_out, pltpu.make_async_copy(ybuf.at[blk % 2], rows_hbm.at[pl.ds(0, FB), :], wsem.at[blk % 2]))

    @pl.when(i == 0)
    def _():
        xbuf[...] = jnp.zeros_like(xbuf)
        for_chunks(i, lambda c: copy_in(i, c).start())

    @pl.when(i < last)
    def _():
        for_chunks(i + 1, lambda c: copy_in(i + 1, c).start())

    @pl.when((i == 0) | (e != e_prev))
    def _():
        win_s[...] = win_ref[...].astype(BF16)
        wout_s[...] = wout_ref[...].astype(BF16)

    @pl.when(i >= 2)
    def _():
        wait_writes(i - 2)

    @pl.when(i < nused_ref[0])
    def _():
        wait_reads(i)
        xa, xb = _unpack_pairs(xbuf[i % 2])
        hb = jnp.dot(xa, win_s[:D // 2, :], preferred_element_type=F32) \
            + jnp.dot(xb, win_s[D // 2:, :], preferred_element_type=F32) + bin_ref[...]
        g = jnp.minimum(hb[:, :D_FF], SWIGLU_LIMIT)
        u = jnp.clip(hb[:, D_FF:], -SWIGLU_LIMIT, SWIGLU_LIMIT)
        act = g * _sigmoid(SWIGLU_ALPHA * g) * (u + 1.0)
        y = jnp.dot(act.astype(BF16), wout_s[...], preferred_element_type=F32) + bout_ref[...]
        ybuf[i % 2] = _pack_pairs(y.astype(BF16).astype(F32))
        for_chunks(i, lambda c: copy_out(i, c).start())

    @pl.when(i == last)
    def _():
        @pl.when(i >= 1)
        def _():
            wait_writes(i - 1)

        wait_writes(i)


def _expert_ffn(block_e, n_used, n_valid, src, xc, layer, w_in, b_in, w_out, b_out):
    n_blocks = block_e.shape[0]
    depth = w_in.shape[0]
    wspec = lambda shape: pl.BlockSpec((None, None) + shape, lambda i, be, *_: (layer, be[i], 0, 0))
    return pl.pallas_call(
        _ffn_kernel,
        grid_spec=pltpu.PrefetchScalarGridSpec(
            num_scalar_prefetch=4,
            grid=(n_blocks,),
            in_specs=[pl.BlockSpec(memory_space=pl.ANY), wspec((D, 2 * D_FF)), wspec((1, 2 * D_FF)),
                      wspec((D_FF, D)), wspec((1, D))],
            out_specs=pl.BlockSpec(memory_space=pl.ANY),
            scratch_shapes=[pltpu.VMEM((D, 2 * D_FF), BF16), pltpu.VMEM((D_FF, D), BF16),
                            pltpu.VMEM((2, FB, D // 2), U32), pltpu.VMEM((2, FB, D // 2), U32),
                            pltpu.SemaphoreType.DMA((2,)), pltpu.SemaphoreType.DMA((2,))],
        ),
        out_shape=jax.ShapeDtypeStruct(xc.shape, U32),
        input_output_aliases={4: 0},
        compiler_params=_cparams(("arbitrary",)),
        name="moe_expert_ffn",
    )(block_e, n_used, n_valid, src, xc, w_in, b_in.reshape(depth, N_EXPERTS, 1, 2 * D_FF), w_out,
      b_out.reshape(depth, N_EXPERTS, 1, D))


def _moe_plan(cnt):
    nt = cnt.shape[0]
    n_blocks = (nt * TM * TOP_K + nt * N_EXPERTS * (CH - 1)) // FB + N_EXPERTS + 1
    n = cnt[:, 0, :N_EXPERTS].astype(I32)
    nch = (n + CH - 1) // CH
    cend = jnp.cumsum(nch, axis=0)
    ce = cend[-1]
    pce = (ce + CPB - 1) // CPB * CPB
    e_end = jnp.cumsum(pce)
    e_start = e_end - pce
    first_slot = CH * (jnp.cumsum(nch, axis=1) - nch)

    blk0 = jnp.arange(n_blocks, dtype=I32) * CPB
    block_e = jnp.minimum(jnp.sum((blk0[:, None] >= e_end[None, :]).astype(I32), axis=-1), N_EXPERTS - 1)
    pick = lambda table, idx: jnp.sum(jnp.where(idx[:, None] == jnp.arange(table.shape[0], dtype=I32), table, 0), axis=-1)
    n_valid = jnp.clip(pick(e_start + ce, block_e) - blk0, 0, CPB)
    n_used = (e_end[-1] // CPB).reshape(1)

    d = jnp.arange(n_blocks * CPB, dtype=I32)
    e_of = jnp.repeat(block_e, CPB)
    local = d - pick(e_start, e_of)
    onehot_e = (e_of[:, None] == jnp.arange(N_EXPERTS, dtype=I32)).astype(F32)
    by_expert = lambda m: jnp.dot(onehot_e, m.T.astype(F32), precision=lax.Precision.HIGHEST).astype(I32)
    cend_e = by_expert(cend)
    in_tile = (local[:, None] >= cend_e - by_expert(nch)) & (local[:, None] < cend_e)
    tile_iota = jnp.arange(nt, dtype=I32)
    at_tile = lambda m: jnp.sum(jnp.where(in_tile, m, 0), axis=-1)
    src = at_tile(tile_iota * SLOTS + by_expert(first_slot) + CH * (local[:, None] - (cend_e - by_expert(nch))))
    return src.astype(I32), block_e.astype(I32), n_valid.astype(I32), n_used.astype(I32)


def _moe(cnt, xc, layer, w_in, b_in, w_out, b_out):
    src, block_e, n_valid, n_used = _moe_plan(cnt)
    return _expert_ffn(block_e, n_used, n_valid, src, xc, layer, w_in, b_in, w_out, b_out)


def _final_kernel(npt, rows_ref, x1_ref, route_ref, modp_ref, g_ref, op_ref, os_ref):
    i = pl.program_id(0)
    x2 = x1_ref[...] + modp_ref[5:6, :] * _combine(route_ref[...], rows_ref[...])
    out = _rms(x2, g_ref[...])

    @pl.when(i < npt)
    def _():
        op_ref[...] = out

    @pl.when(i >= npt)
    def _():
        os_ref[...] = out


def _rope_tables(n_tokens):
    rows = n_tokens // GRID_W
    pos_row = jnp.repeat(jnp.arange(rows, dtype=F32), GRID_W)
    pos_col = jnp.tile(jnp.arange(GRID_W, dtype=F32), rows)
    n_freq = ROPE_DIM // 4
    inv_freq = ROPE_BASE ** (-jnp.arange(n_freq, dtype=F32) / n_freq)
    ang = jnp.stack([pos_row[:, None] * inv_freq, pos_col[:, None] * inv_freq], axis=1)
    cos, sin = jnp.cos(ang), jnp.sin(ang)
    zero = jnp.zeros_like(sin)
    c = jnp.stack([cos, cos], axis=2).reshape(n_tokens, ROPE_DIM)
    sm = jnp.stack([-sin, zero], axis=2).reshape(n_tokens, ROPE_DIM)
    sp = jnp.stack([zero, sin], axis=2).reshape(n_tokens, ROPE_DIM)

    def place(a, fill):
        left = jnp.full((n_tokens, ROPE_LANE0), fill, F32)
        right = jnp.full((n_tokens, LANES - ROPE_LANE0 - ROPE_DIM), fill, F32)
        body = jnp.concatenate([left, a, right], axis=1)
        ident = jnp.full((TM, LANES), fill, F32)
        return jnp.concatenate([body, ident], axis=0)

    return jnp.stack([place(c, 1.0), place(sm, 0.0), place(sp, 0.0)], axis=0)


def kernel(x_prompt, x_sample, c, cache_mla_ckv, cache_mla_krope, state_mlstm_C, state_mlstm_n, state_mlstm_m, c_ctx, ada_w, ada_b, norm1_g, norm2_g, ab_w_in, mla_q_norm_g, mla_w_uq, mla_kv_norm_g, mla_w_ukv, mlstm_gate_b, mlstm_norm_g, ab_w_out, conv_w_pw1, conv_w_dw, conv_b_dw, conv_ln_g, conv_ln_b, conv_w_pw2, router_w, router_b, moe_w_in, moe_b_in, moe_w_out, moe_b_out, final_norm_g):
    bp, sp_, _ = x_prompt.shape
    bs, ss, _ = x_sample.shape
    past = cache_mla_ckv.shape[2]
    tp, ts = bp * sp_, bs * ss
    t = tp + ts
    nt = t // TM
    npt = tp // TM
    tps = ss // TM
    assert sp_ == TM and ss % TM == 0 and bs <= CTX_ROW and tp % ss == 0

    xp2 = x_prompt.reshape(tp, D)
    xs2 = x_sample.reshape(ts, D)
    cond = jnp.zeros((N_COND, D), F32).at[:bs].set(c).at[CTX_ROW].set(c_ctx)
    mod = _modulation(cond, ada_w, ada_b)

    def mod_row(i):
        return jnp.where(i < npt, CTX_ROW, (i - npt) // tps)

    def mod_spec(layer):
        return pl.BlockSpec((None, None, 6, D), lambda i, *_: (layer, mod_row(i), 0, 0))

    row_spec = lambda w: pl.BlockSpec((TM, w), lambda i, *_: (i, 0))
    prompt_spec = pl.BlockSpec((TM, D), lambda i, *_: (jnp.minimum(i, npt - 1), 0))
    sample_spec = pl.BlockSpec((TM, D), lambda i, *_: (jnp.maximum(i - npt, 0), 0))
    full = lambda a: pl.BlockSpec(a.shape, lambda i, *_: (0,) * a.ndim)
    vec = lambda a: a.reshape(1, -1)

    def router_args(layer):
        wr = jnp.pad(router_w[layer], ((0, 0), (0, LANES - N_EXPERTS))).astype(BF16)
        br = jnp.pad(router_b[layer], (0, LANES - N_EXPERTS)).reshape(1, LANES)
        return wr, br

    moe_out_shapes = [
        jax.ShapeDtypeStruct((t, D), F32),
        jax.ShapeDtypeStruct((nt * SLOTS, D // 2), U32),
        jax.ShapeDtypeStruct((t, LANES), F32),
        jax.ShapeDtypeStruct((nt, SUBLANES, LANES), F32),
    ]
    moe_out_specs = [row_spec(D), pl.BlockSpec((SLOTS, D // 2), lambda i: (i, 0)), row_spec(LANES),
                     pl.BlockSpec((None, SUBLANES, LANES), lambda i: (i, 0, 0))]

    j = 0
    w = ab_w_in[j]
    zc = lambda n: jnp.zeros((D, n), F32)
    misc_w = jnp.concatenate([w[:, 2464:2480], zc(ROPE_LANE0 - 16), w[:, 384:416], zc(LANES - ROPE_LANE0 - ROPE_DIM)], axis=1)
    w_in = jnp.concatenate([w[:, :384], misc_w, w[:, 416:2464]], axis=1).astype(BF16)
    wq = jnp.pad(mla_w_uq[j].reshape(Q_RANK, H_A, NOPE + ROPE_DIM), ((0, 0), (0, 0), (0, HEAD_PAD - NOPE - ROPE_DIM)))
    wq = wq.reshape(Q_RANK, H_A * HEAD_PAD).astype(BF16)
    wkv = mla_w_ukv[j].reshape(KV_RANK, H_A, NOPE + V_A)
    wk = jnp.pad(wkv[:, :, :NOPE], ((0, 0), (0, 0), (0, HEAD_PAD - NOPE))).reshape(KV_RANK, H_A * HEAD_PAD).astype(BF16)
    wv = jnp.pad(wkv[:, :, NOPE:], ((0, 0), (0, 0), (0, HEAD_PAD - V_A))).reshape(KV_RANK, H_A * HEAD_PAD).astype(BF16)
    gate_b = jnp.pad(mlstm_gate_b[j], (0, LANES - 4 * H_B)).reshape(1, LANES)
    tabs = _rope_tables(ss)
    pos_blocks = ss // TM

    def tab_block(i):
        return jnp.where(i < npt, pos_blocks, (i - npt) % tps)

    nctx = bs * past
    ctx_ckv = cache_mla_ckv[:, j].reshape(nctx, KV_RANK)
    ctx_kr = jnp.pad(cache_mla_krope[:, j].reshape(nctx, ROPE_DIM), ((0, 0), (ROPE_LANE0, LANES - ROPE_LANE0 - ROPE_DIM)))
    nct, cps, nts = nctx // TM, past // TM, ts // TM
    kps = cps + tps
    kv_rows = bs * (past + ss) + tp

    def kv_block(i):
        own = ((i - npt) // tps) * kps + cps + (i - npt) % tps
        cached = ((i - nt) // cps) * kps + (i - nt) % cps
        return jnp.where(i < npt, bs * kps + i, jnp.where(i < nt, own, cached))

    tile_spec = lambda w: pl.BlockSpec((TM, w), lambda i: (jnp.minimum(i, nt - 1), 0))
    ctx_spec = pl.BlockSpec((TM, LANES), lambda i: (jnp.clip(i - nt, 0, nct - 1), 0))
    kv_spec = pl.BlockSpec((TM, H_A * HEAD_PAD), lambda i: (kv_block(i), 0))
    ins = [xp2, xs2, ctx_ckv, ctx_kr, mod, vec(norm1_g[0]), w_in, vec(mla_q_norm_g[j]), wq, vec(mla_kv_norm_g[j]), wk, wv,
           gate_b, tabs]
    in_specs = [prompt_spec, pl.BlockSpec((TM, D), lambda i: (jnp.clip(i - npt, 0, nts - 1), 0)), ctx_spec, ctx_spec,
                mod_spec(0)] + [full(a) for a in ins[5:13]] + [
        pl.BlockSpec((3, TM, LANES), lambda i: (0, tab_block(i), 0))]
    q, k, v, ckv, misc, mqkv, mo = pl.pallas_call(
        functools.partial(_ab_in_kernel, npt, nt),
        grid=(nt + nct,),
        in_specs=in_specs,
        out_specs=[tile_spec(H_A * HEAD_PAD), kv_spec, kv_spec, tile_spec(KV_RANK), tile_spec(LANES),
                   tile_spec(3 * MB), tile_spec(MB)],
        out_shape=[
            jax.ShapeDtypeStruct((t, H_A * HEAD_PAD), BF16), jax.ShapeDtypeStruct((kv_rows, H_A * HEAD_PAD), BF16),
            jax.ShapeDtypeStruct((kv_rows, H_A * HEAD_PAD), BF16), jax.ShapeDtypeStruct((t, KV_RANK), F32),
            jax.ShapeDtypeStruct((t, LANES), F32), jax.ShapeDtypeStruct((t, 3 * MB), BF16),
            jax.ShapeDtypeStruct((t, MB), F32)],
        compiler_params=_cparams(("arbitrary",)),
        name="ab_in_proj",
    )(*ins)

    oa_p = _attention(q, k, v, 0, bp, sp_, bs * (past + ss), sp_)
    oa_s = _attention(q, k, v, tp, bs, ss, 0, past + ss)

    zc0 = jnp.zeros((bp, 2, H_B, DH_B, DH_B), F32)
    zn0 = jnp.zeros((bp, 2, H_B, 1, DH_B), F32)
    zm0 = jnp.zeros((bp, 2, H_B, 1, LANES), F32)
    hf_p, hb_p, c_p, n_p, m_p = _mlstm(mqkv, misc, zc0, zn0, zm0, 0, bp, sp_)
    sm0 = jnp.broadcast_to(state_mlstm_m[:, j][..., None, None], (bs, 2, H_B, 1, LANES))
    hf_s, hb_s, _, _, _ = _mlstm(mqkv, misc, state_mlstm_C[:, j], state_mlstm_n[:, j][:, :, :, None, :], sm0, tp, bs, ss)

    wr, br = router_args(0)
    prompt_w = lambda w: pl.BlockSpec((TM, w), lambda i: (jnp.minimum(i, npt - 1), 0))
    sample_w = lambda w: pl.BlockSpec((TM, w), lambda i: (jnp.maximum(i - npt, 0), 0))
    ins = [oa_p, oa_s, hf_p, hf_s, hb_p, hb_s, mo, vec(mlstm_norm_g[j]), ab_w_out[j].astype(BF16), xp2, xs2, mod,
           vec(norm2_g[0]), wr, br]
    in_specs = [prompt_w(H_A * V_A), sample_w(H_A * V_A), prompt_w(MB), sample_w(MB), prompt_w(MB), sample_w(MB),
                row_spec(MB), full(ins[7]), full(ins[8]),
                prompt_spec, sample_spec, mod_spec(0), full(ins[12]), full(wr), full(br)]
    x1, xc, route, cnt = pl.pallas_call(
        functools.partial(_ab_out_kernel, npt),
        grid=(nt,),
        in_specs=in_specs,
        out_specs=moe_out_specs,
        out_shape=moe_out_shapes,
        compiler_params=_cparams(("arbitrary",)),
        name="ab_out_proj",
    )(*ins)
    ys = _moe(cnt, xc, 0, moe_w_in, moe_b_in, moe_w_out, moe_b_out)

    slots_spec = pl.BlockSpec((SLOTS, D // 2), lambda i, *_: (i, 0))
    ins = [ys, x1, route, mod, mod, vec(norm1_g[1]), conv_w_pw1[0].astype(BF16)]
    x2, u = pl.pallas_call(
        _conv_in_kernel,
        grid=(nt,),
        in_specs=[slots_spec, row_spec(D), row_spec(LANES), mod_spec(0), mod_spec(1), full(ins[5]), full(ins[6])],
        out_specs=[row_spec(D), row_spec(D)],
        out_shape=[jax.ShapeDtypeStruct((t, D), F32), jax.ShapeDtypeStruct((t, D), F32)],
        compiler_params=_cparams(("arbitrary",)),
        name="conv_in_proj",
    )(*ins)

    hpt = TM // HALO
    nhalo = t // HALO
    wr, br = router_args(1)
    ins = [u, u, u, conv_w_dw[0], vec(conv_b_dw[0]), vec(conv_ln_g[0]), vec(conv_ln_b[0]), conv_w_pw2[0].astype(BF16),
           x2, mod, vec(norm2_g[1]), wr, br]
    in_specs = [row_spec(D),
                pl.BlockSpec((HALO, D), lambda i: (jnp.maximum(i * hpt - 1, 0), 0)),
                pl.BlockSpec((HALO, D), lambda i: (jnp.minimum((i + 1) * hpt, nhalo - 1), 0)),
                full(ins[3]), full(ins[4]), full(ins[5]), full(ins[6]), full(ins[7]), row_spec(D), mod_spec(1),
                full(ins[10]), full(wr), full(br)]
    x1, xc, route, cnt = pl.pallas_call(
        functools.partial(_conv_out_kernel, npt, tps),
        grid=(nt,),
        in_specs=in_specs,
        out_specs=moe_out_specs,
        out_shape=moe_out_shapes,
        scratch_shapes=[pltpu.VMEM((TM + 2 * HALO, D), F32), pltpu.VMEM((SUBLANES, TM + 24, D), F32)],
        compiler_params=_cparams(("arbitrary",)),
        name="conv_out_proj",
    )(*ins)
    ys = _moe(cnt, xc, 1, moe_w_in, moe_b_in, moe_w_out, moe_b_out)

    ins = [ys, x1, route, mod, vec(final_norm_g)]
    y_p, y_s = pl.pallas_call(
        functools.partial(_final_kernel, npt),
        grid=(nt,),
        in_specs=[slots_spec, row_spec(D), row_spec(LANES), mod_spec(1), full(ins[4])],
        out_specs=[prompt_spec, sample_spec],
        out_shape=[jax.ShapeDtypeStruct((tp, D), F32), jax.ShapeDtypeStruct((ts, D), F32)],
        compiler_params=_cparams(("arbitrary",)),
        name="final_norm",
    )(*ins)

    y_prompt = y_p.reshape(bp, sp_, D)
    y_sample = y_s.reshape(bs, ss, D)
    new_ckv = ckv[:tp].reshape(bp, 1, sp_, KV_RANK)
    new_krope = misc[:tp, ROPE_LANE0:ROPE_LANE0 + ROPE_DIM].reshape(bp, 1, sp_, ROPE_DIM)
    new_c = c_p[:, None]
    new_n = n_p[:, None, :, :, 0, :]
    new_m = m_p[:, None, :, :, 0, 0]
    return (y_prompt, y_sample, new_ckv, new_krope, new_c, new_n, new_m)
```

```python
import functools
import math

import jax
import jax.numpy as jnp
from jax import lax
from jax.experimental import pallas as pl
from jax.experimental.pallas import tpu as pltpu

F32 = jnp.float32
BF16 = jnp.bfloat16
I32 = jnp.int32
U32 = jnp.uint32

D = 1024
GRID_W = 64
H_A = 8
Q_RANK = 256
KV_RANK = 128
NOPE = 64
ROPE_DIM = 32
V_A = 64
ROPE_BASE = 10000.0
H_B = 4
DH_B = 128
MB = H_B * DH_B
CONV_K = 31
N_EXPERTS = 32
TOP_K = 4
D_FF = 1024
SWIGLU_LIMIT = 7.0
SWIGLU_ALPHA = 1.702

LANES = 128
SUBLANES = 8
HEAD_PAD = 128
ROPE_LANE0 = 64
N_COND = 16
CTX_ROW = 8

TM = 256
FB = 512
LC = 256
TQ = 256
HPS = 4
CH = SUBLANES
SLOTS = TOP_K * TM + N_EXPERTS * CH
VMEM_LIMIT = 56 * 1024 * 1024

NEG_INF = float("-inf")
LOG2_E = math.log2(math.e)


def _cparams(sem):
    return pltpu.CompilerParams(dimension_semantics=sem, vmem_limit_bytes=VMEM_LIMIT)


def _sigmoid(x):
    return 1.0 / (1.0 + jnp.exp(-x))


def _rms(x, g, eps=1e-6):
    return x * lax.rsqrt(jnp.mean(x * x, axis=-1, keepdims=True) + eps) * g


def _lane(shape):
    return lax.broadcasted_iota(I32, shape, len(shape) - 1)


def _pack_pairs(x):
    w = x.shape[1] // 2
    return pltpu.bitcast(x[:, :w], U32) | (pltpu.bitcast(x[:, w:], U32) >> 16)


def _unpack_pairs(wd):
    a = pltpu.bitcast(wd & jnp.uint32(0xFFFF0000), F32).astype(BF16)
    b = pltpu.bitcast(wd << 16, F32).astype(BF16)
    return a, b


def _mod_kernel(cond_ref, w_ref, b_ref, o_ref):
    c = cond_ref[...]
    s = (c * _sigmoid(c)).astype(BF16)
    o_ref[...] = jnp.dot(s, w_ref[...].astype(BF16), preferred_element_type=F32) + b_ref[...]


def _modulation(cond, ada_w, ada_b):
    depth = ada_w.shape[0]
    nj = ada_w.shape[2] // D
    out = pl.pallas_call(
        _mod_kernel,
        grid=(depth, nj),
        in_specs=[
            pl.BlockSpec((N_COND, D), lambda l, j: (0, 0)),
            pl.BlockSpec((None, D, D), lambda l, j: (l, 0, j)),
            pl.BlockSpec((None, 1, D), lambda l, j: (l, 0, j)),
        ],
        out_specs=pl.BlockSpec((None, N_COND, D), lambda l, j: (l, 0, j)),
        out_shape=jax.ShapeDtypeStruct((depth, N_COND, nj * D), F32),
        compiler_params=_cparams(("arbitrary", "arbitrary")),
        name="ada_modulation",
    )(cond, ada_w, ada_b.reshape(depth, 1, nj * D))
    return out.reshape(depth, N_COND, nj, D)


def _combine(route, rows):
    lane = _lane((route.shape[0], SLOTS)).astype(F32)
    g = jnp.zeros(lane.shape, F32)
    for k in range(TOP_K):
        g = jnp.where(lane == route[:, 8 + k:9 + k], route[:, 4 + k:5 + k], g)
    gb = g.astype(BF16)
    ya, yb = _unpack_pairs(rows)
    return jnp.concatenate([jnp.dot(gb, ya, preferred_element_type=F32),
                            jnp.dot(gb, yb, preferred_element_type=F32)], axis=-1)


def _moe_route(x1, mod_ref, n2g, wr_ref, br_ref, xc_ref, route_ref, cnt_ref):
    h2 = _rms(x1, n2g) * (1.0 + mod_ref[4:5, :]) + mod_ref[3:4, :]
    h2b = h2.astype(BF16)
    logits = jnp.dot(h2b, wr_ref[...], preferred_element_type=F32) + br_ref[...]
    tm = logits.shape[0]
    lane = _lane((tm, LANES))
    lanef = lane.astype(F32)
    l = jnp.where(lane < N_EXPERTS, logits, NEG_INF)
    tops, idxs, sels = [], [], []
    for _ in range(TOP_K):
        mv = jnp.max(l, axis=-1, keepdims=True)
        idx = jnp.min(jnp.where(l == mv, lanef, float(LANES)), axis=-1, keepdims=True)
        sel = lanef == idx
        l = jnp.where(sel, NEG_INF, l)
        tops.append(mv)
        idxs.append(idx)
        sels.append(sel)
    ex = [jnp.exp(t - tops[0]) for t in tops]
    den = ex[0] + ex[1] + ex[2] + ex[3]
    gates = [e / den for e in ex]

    onehot = jnp.zeros((tm, LANES), F32)
    for sel in sels:
        onehot = onehot + jnp.where(sel, 1.0, 0.0)
    row = lax.broadcasted_iota(I32, (tm, tm), 0)
    col = lax.broadcasted_iota(I32, (tm, tm), 1)
    strict = jnp.where(col < row, 1.0, 0.0).astype(BF16)
    before = jnp.dot(strict, onehot.astype(BF16), preferred_element_type=F32)
    n_e = jnp.sum(onehot, axis=0, keepdims=True)
    cnt_ref[...] = jnp.broadcast_to(n_e, cnt_ref.shape)
    nch = jnp.right_shift(n_e.astype(I32) + (CH - 1), CH.bit_length() - 1).astype(F32)
    r128 = lax.broadcasted_iota(I32, (LANES, LANES), 0)
    c128 = lax.broadcasted_iota(I32, (LANES, LANES), 1)
    upper = jnp.where(r128 < c128, 1.0, 0.0).astype(BF16)
    seg0 = float(CH) * jnp.dot(jnp.broadcast_to(nch, (SUBLANES, LANES)).astype(BF16), upper,
                               preferred_element_type=F32)[0:1, :]
    slot_of = seg0 + before
    slots = [jnp.sum(jnp.where(sel, slot_of, 0.0), axis=-1, keepdims=True) for sel in sels]

    r = jnp.zeros((tm, LANES), F32)
    for j, colv in enumerate(idxs + gates + slots):
        r = jnp.where(lane == j, colv, r)
    route_ref[...] = r

    lane_s = _lane((tm, SLOTS)).astype(F32)
    pt = jnp.zeros(lane_s.shape, F32)
    for k in range(TOP_K):
        pt = jnp.where(lane_s == slots[k], 1.0, pt)
    xc = jnp.dot(pt.T.astype(BF16), h2b, preferred_element_type=F32)
    xc_ref[...] = _pack_pairs(xc)


def _values_with_ones(ckvb, wv_ref):
    vv = jnp.dot(ckvb, wv_ref[...], preferred_element_type=F32)
    return jnp.where((_lane(vv.shape) & (HEAD_PAD - 1)) == V_A, 1.0, vv).astype(BF16)


def _rope(x, tab_ref):
    return x * tab_ref[0] + pltpu.roll(x, LANES - 8, 1) * tab_ref[1] + pltpu.roll(x, 8, 1) * tab_ref[2]


def _ab_in_kernel(npt, nt, *refs):
    @pl.when(pl.program_id(0) < nt)
    def _():
        _ab_in_tokens(npt, *refs)

    @pl.when(pl.program_id(0) >= nt)
    def _():
        _ab_in_context(*refs)


def _ab_in_context(xp_ref, xs_ref, cckv_ref, ckr_ref, mod_ref, n1g_ref, win_ref, qg_ref, wq_ref, kvg_ref, wk_ref, wv_ref,
                   gb_ref, tab_ref, q_ref, k_ref, v_ref, ckv_ref, misc_ref, mqkv_ref, mo_ref):
    ckvb = cckv_ref[...].astype(BF16)
    kn = jnp.dot(ckvb, wk_ref[...], preferred_element_type=F32)
    v_ref[...] = _values_with_ones(ckvb, wv_ref)
    kr = ckr_ref[...]
    for hh in range(H_A):
        sl = slice(hh * HEAD_PAD, (hh + 1) * HEAD_PAD)
        k_ref[:, sl] = (kn[:, sl] + kr).astype(BF16)


def _ab_in_tokens(npt, xp_ref, xs_ref, cckv_ref, ckr_ref, mod_ref, n1g_ref, win_ref, qg_ref, wq_ref, kvg_ref, wk_ref,
                  wv_ref, gb_ref, tab_ref, q_ref, k_ref, v_ref, ckv_ref, misc_ref, mqkv_ref, mo_ref):
    x = jnp.where(pl.program_id(0) < npt, xp_ref[...], xs_ref[...])
    h = _rms(x, n1g_ref[...]) * (1.0 + mod_ref[1:2, :]) + mod_ref[0:1, :]
    z = jnp.dot(h.astype(BF16), win_ref[...], preferred_element_type=F32)

    qn = _rms(z[:, :Q_RANK], qg_ref[...]).astype(BF16)
    q = jnp.dot(qn, wq_ref[...], preferred_element_type=F32)
    ckv = _rms(z[:, Q_RANK:Q_RANK + KV_RANK], kvg_ref[...])
    ckv_ref[...] = ckv
    ckvb = ckv.astype(BF16)
    kn = jnp.dot(ckvb, wk_ref[...], preferred_element_type=F32)
    v_ref[...] = _values_with_ones(ckvb, wv_ref)

    misc = z[:, Q_RANK + KV_RANK:Q_RANK + KV_RANK + LANES]
    misc_ref[...] = misc + gb_ref[...]
    lane = _lane(misc.shape)
    kr = jnp.where((lane >= ROPE_LANE0) & (lane < ROPE_LANE0 + ROPE_DIM), _rope(misc, tab_ref), 0.0)
    scale = float((NOPE + ROPE_DIM) ** -0.5 * LOG2_E)
    for hh in range(H_A):
        sl = slice(hh * HEAD_PAD, (hh + 1) * HEAD_PAD)
        q_ref[:, sl] = (_rope(q[:, sl], tab_ref) * scale).astype(BF16)
        k_ref[:, sl] = (kn[:, sl] + kr).astype(BF16)

    m0 = Q_RANK + KV_RANK + LANES
    mqkv_ref[:, :MB] = z[:, m0:m0 + MB].astype(BF16)
    mqkv_ref[:, MB:2 * MB] = (z[:, m0 + MB:m0 + 2 * MB] * float(DH_B ** -0.5)).astype(BF16)
    mqkv_ref[:, 2 * MB:] = z[:, m0 + 2 * MB:m0 + 3 * MB].astype(BF16)
    mo_ref[...] = z[:, m0 + 3 * MB:m0 + 4 * MB]


def _attn_kernel(q_ref, k_ref, v_ref, o_ref):
    outs = []
    for j in range(HPS):
        sl = slice(j * HEAD_PAD, (j + 1) * HEAD_PAD)
        s = lax.dot_general(q_ref[:, sl], k_ref[:, sl], (((1,), (1,)), ((), ())), preferred_element_type=F32)
        m = jnp.max(s, axis=-1, keepdims=True)
        p = jnp.exp2(s - m).astype(BF16)
        o = jnp.dot(p, v_ref[:, sl], preferred_element_type=F32)
        outs.append(o / o[:, V_A:V_A + 1])
    lane = _lane(outs[0].shape)
    for j in range(0, HPS, 2):
        o_ref[:, j * V_A:(j + 2) * V_A] = jnp.where(lane < V_A, outs[j], pltpu.roll(outs[j + 1], V_A, 1)).astype(BF16)


def _attention(q, k, v, qrow0, nb, s, krow0, l):
    tq = min(TQ, s)
    nq = s // tq
    qrow = lambda bi, hp, qi: (qrow0 // tq + bi * nq + qi, hp)
    krow = lambda bi, hp, qi: (krow0 // l + bi, hp)
    return pl.pallas_call(
        _attn_kernel,
        grid=(nb, H_A // HPS, nq),
        in_specs=[pl.BlockSpec((tq, HPS * HEAD_PAD), qrow), pl.BlockSpec((l, HPS * HEAD_PAD), krow),
                  pl.BlockSpec((l, HPS * HEAD_PAD), krow)],
        out_specs=pl.BlockSpec((tq, HPS * V_A), lambda bi, hp, qi: (bi * nq + qi, hp)),
        out_shape=jax.ShapeDtypeStruct((nb * s, H_A * V_A), BF16),
        compiler_params=_cparams(("arbitrary", "arbitrary", "arbitrary")),
        name="mla_attention",
    )(q, k, v)


def _log_sigmoid(x):
    return jnp.minimum(x, 0.0) - jnp.log(1.0 + jnp.exp(-jnp.abs(x)))


def _mlstm_direction(d, q_ref, k_ref, v_ref, g_ref, h_ref, c_s, n_s, m_s):
    gates = g_ref[...]
    lf_all = _log_sigmoid(gates)
    row = lax.broadcasted_iota(I32, (LC, LC), 0)
    col = lax.broadcasted_iota(I32, (LC, LC), 1)
    mask = (col <= row) if d == 0 else (col >= row)
    eye = row == col
    ones = jnp.where(mask, 1.0, 0.0).astype(BF16)
    cum = jnp.zeros((LC, LANES), F32)
    rest = lf_all
    for _ in range(3):
        piece = rest.astype(BF16)
        cum = cum + jnp.dot(ones, piece, preferred_element_type=F32)
        rest = rest - piece.astype(F32)

    for hh in range(H_B):
        sl = slice(hh * DH_B, (hh + 1) * DH_B)
        q = q_ref[:, sl]
        k = k_ref[:, sl]
        v = v_ref[:, sl]
        b_col = cum[:, 8 * d + 4 + hh:8 * d + 5 + hh]
        i_col = gates[:, 8 * d + hh:8 * d + hh + 1]
        m_prev = m_s[d, hh][:, 0:1]
        r_row = jnp.sum(jnp.where(eye, i_col - b_col, 0.0), axis=0, keepdims=True)
        dm = jnp.where(mask, b_col + r_row, NEG_INF)
        a_col = b_col + m_prev
        mt = jnp.maximum(a_col, jnp.max(dm, axis=-1, keepdims=True))
        w_inter = jnp.exp(a_col - mt)
        qk = lax.dot_general(q, k, (((1,), (1,)), ((), ())), preferred_element_type=F32)
        s = qk * jnp.exp(dm - mt)
        c_prev = c_s[d, hh]
        n_prev = n_s[d, hh]
        num = w_inter * jnp.dot(q, c_prev.astype(BF16), preferred_element_type=F32) \
            + jnp.dot(s.astype(BF16), v, preferred_element_type=F32)
        n_rows = jnp.broadcast_to(n_prev, (SUBLANES, DH_B)).astype(BF16)
        qn = lax.dot_general(q, n_rows, (((1,), (1,)), ((), ())), preferred_element_type=F32)[:, 0:1]
        den = w_inter * qn + jnp.sum(s, axis=-1, keepdims=True)
        h_ref[:, sl] = num / jnp.maximum(jnp.abs(den), jnp.exp(-mt))

        b_tot = b_col[LC - 1:LC, :] if d == 0 else b_col[0:1, :]
        g_col = b_tot - b_col + i_col
        m_new = jnp.maximum(b_tot + m_prev, jnp.max(g_col, axis=0, keepdims=True))
        w_c = jnp.exp(b_tot + m_prev - m_new)
        kw = k.astype(F32) * jnp.exp(g_col - m_new)
        c_s[d, hh] = w_c * c_prev + jnp.dot(kw.T.astype(BF16), v, preferred_element_type=F32)
        n_s[d, hh] = w_c * n_prev + jnp.sum(kw, axis=0, keepdims=True)
        m_s[d, hh] = jnp.broadcast_to(m_new, (1, LANES))


def _mlstm_kernel(qf_ref, kf_ref, vf_ref, gf_ref, qb_ref, kb_ref, vb_ref, gb_ref, c0_ref, n0_ref, m0_ref,
                  hf_ref, hb_ref, c_out, n_out, m_out, c_s, n_s, m_s):
    c = pl.program_id(1)

    @pl.when(c == 0)
    def _():
        c_s[...] = c0_ref[...]
        n_s[...] = n0_ref[...]
        m_s[...] = m0_ref[...]

    _mlstm_direction(0, qf_ref, kf_ref, vf_ref, gf_ref, hf_ref, c_s, n_s, m_s)
    _mlstm_direction(1, qb_ref, kb_ref, vb_ref, gb_ref, hb_ref, c_s, n_s, m_s)

    @pl.when(c == pl.num_programs(1) - 1)
    def _():
        c_out[...] = c_s[...]
        n_out[...] = n_s[...]
        m_out[...] = m_s[...]


def _mlstm(mqkv, misc, c0, n0, m0, row0, nb, s):
    nc = s // LC
    blk0 = row0 // LC
    fwd = lambda bi, ci: blk0 + bi * nc + ci
    bwd = lambda bi, ci: blk0 + bi * nc + nc - 1 - ci
    chunk_specs = lambda rows: [
        pl.BlockSpec((LC, MB), lambda bi, ci: (rows(bi, ci), 0)),
        pl.BlockSpec((LC, MB), lambda bi, ci: (rows(bi, ci), 1)),
        pl.BlockSpec((LC, MB), lambda bi, ci: (rows(bi, ci), 2)),
        pl.BlockSpec((LC, LANES), lambda bi, ci: (rows(bi, ci), 0)),
    ]
    state_spec = lambda shape: pl.BlockSpec((None,) + shape, lambda bi, ci: (bi,) + (0,) * len(shape))
    state_specs = [state_spec((2, H_B, DH_B, DH_B)), state_spec((2, H_B, 1, DH_B)), state_spec((2, H_B, 1, LANES))]
    return pl.pallas_call(
        _mlstm_kernel,
        grid=(nb, nc),
        in_specs=chunk_specs(fwd) + chunk_specs(bwd) + state_specs,
        out_specs=[
            pl.BlockSpec((LC, MB), lambda bi, ci: (bi * nc + ci, 0)),
            pl.BlockSpec((LC, MB), lambda bi, ci: (bi * nc + nc - 1 - ci, 0)),
        ] + state_specs,
        out_shape=[
            jax.ShapeDtypeStruct((nb * s, MB), F32),
            jax.ShapeDtypeStruct((nb * s, MB), F32),
            jax.ShapeDtypeStruct((nb, 2, H_B, DH_B, DH_B), F32),
            jax.ShapeDtypeStruct((nb, 2, H_B, 1, DH_B), F32),
            jax.ShapeDtypeStruct((nb, 2, H_B, 1, LANES), F32),
        ],
        scratch_shapes=[
            pltpu.VMEM((2, H_B, DH_B, DH_B), F32),
            pltpu.VMEM((2, H_B, 1, DH_B), F32),
            pltpu.VMEM((2, H_B, 1, LANES), F32),
        ],
        compiler_params=_cparams(("arbitrary", "arbitrary")),
        name="mlstm_scan",
    )(mqkv, mqkv, mqkv, misc, mqkv, mqkv, mqkv, misc, c0, n0, m0)


def _ab_out_kernel(npt, oap_ref, oas_ref, hfp_ref, hfs_ref, hbp_ref, hbs_ref, mo_ref, mg_ref, wout_ref, xp_ref, xs_ref,
                   mod_ref, n2g_ref, wr_ref, br_ref, x1_ref, xc_ref, route_ref, cnt_ref):
    in_prompt = pl.program_id(0) < npt
    pick = lambda p_ref, s_ref: jnp.where(in_prompt, p_ref[...], s_ref[...])
    hsum = pick(hfp_ref, hfs_ref) + pick(hbp_ref, hbs_ref)
    parts = []
    for hh in range(H_B):
        hh_ = hsum[:, hh * DH_B:(hh + 1) * DH_B]
        mu = jnp.mean(hh_, axis=-1, keepdims=True)
        var = jnp.mean(jnp.square(hh_ - mu), axis=-1, keepdims=True)
        parts.append((hh_ - mu) * lax.rsqrt(var + 1e-5))
    hn = jnp.concatenate(parts, axis=-1) * mg_ref[...]
    ob = (hn * _sigmoid(mo_ref[...])).astype(BF16)
    o = jnp.dot(pick(oap_ref, oas_ref), wout_ref[:H_A * V_A, :], preferred_element_type=F32) \
        + jnp.dot(ob, wout_ref[H_A * V_A:, :], preferred_element_type=F32)
    x = pick(xp_ref, xs_ref)
    x1 = x + mod_ref[2:3, :] * o
    x1_ref[...] = x1
    _moe_route(x1, mod_ref, n2g_ref[...], wr_ref, br_ref, xc_ref, route_ref, cnt_ref)


def _conv_in_kernel(rows_ref, x1_ref, route_ref, modp_ref, mod_ref, n1g_ref, w_ref, x2_ref, u_ref):
    x2 = x1_ref[...] + modp_ref[5:6, :] * _combine(route_ref[...], rows_ref[...])
    x2_ref[...] = x2
    h = _rms(x2, n1g_ref[...]) * (1.0 + mod_ref[1:2, :]) + mod_ref[0:1, :]
    z = jnp.dot(h.astype(BF16), w_ref[...], preferred_element_type=F32)
    u_ref[...] = z[:, :D] * _sigmoid(z[:, D:])


HALO = 16


def _conv_out_kernel(npt, tiles_per_seq, u_ref, up_ref, un_ref, wdw_ref, bdw_ref, lg_ref, lb_ref, w2_ref,
                     x_ref, mod_ref, n2g_ref, wr_ref, br_ref, x1_ref, xc_ref, route_ref, cnt_ref, ext_ref, sh_ref):
    i = pl.program_id(0)
    j = i - npt
    in_prompt = i < npt
    first = in_prompt | (j % tiles_per_seq == 0)
    last = in_prompt | (j % tiles_per_seq == tiles_per_seq - 1)
    ext_ref[0:HALO, :] = jnp.where(first, 0.0, up_ref[...])
    ext_ref[HALO:HALO + TM, :] = u_ref[...]
    ext_ref[HALO + TM:, :] = jnp.where(last, 0.0, un_ref[...])

    for b in range(SUBLANES):
        sh_ref[b] = ext_ref[b + 1:b + 1 + TM + 24, :]
    acc = jnp.zeros((TM, D), F32) + bdw_ref[...]
    for tap in range(CONV_K):
        a, b = divmod(tap, SUBLANES)
        acc = acc + wdw_ref[tap:tap + 1, :] * sh_ref[b, SUBLANES * a:SUBLANES * a + TM, :]
    mu = jnp.mean(acc, axis=-1, keepdims=True)
    var = jnp.mean(jnp.square(acc - mu), axis=-1, keepdims=True)
    ln = (acc - mu) * lax.rsqrt(var + 1e-5) * lg_ref[...] + lb_ref[...]
    act = (ln * _sigmoid(ln)).astype(BF16)
    o = jnp.dot(act, w2_ref[...], preferred_element_type=F32)
    x1 = x_ref[...] + mod_ref[2:3, :] * o
    x1_ref[...] = x1
    _moe_route(x1, mod_ref, n2g_ref[...], wr_ref, br_ref, xc_ref, route_ref, cnt_ref)


CPB = FB // CH


def _ffn_kernel(be_ref, nused_ref, nval_ref, src_ref, xc_in, win_ref, bin_ref, wout_ref, bout_ref, rows_hbm,
                win_s, wout_s, xbuf, ybuf, rsem, wsem):
    del xc_in
    i = pl.program_id(0)
    last = pl.num_programs(0) - 1
    e = be_ref[i]
    e_prev = be_ref[jnp.maximum(i - 1, 0)]

    def copy_in(blk, c):
        row = pl.multiple_of(src_ref[blk * CPB + c], CH)
        return pltpu.make_async_copy(rows_hbm.at[pl.ds(row, CH), :],
                                     xbuf.at[blk % 2, pl.ds(pl.multiple_of(c * CH, CH), CH), :], rsem.at[blk % 2])

    def copy_out(blk, c):
        row = pl.multiple_of(src_ref[blk * CPB + c], CH)
        return pltpu.make_async_copy(ybuf.at[blk % 2, pl.ds(pl.multiple_of(c * CH, CH), CH), :],
                                     rows_hbm.at[pl.ds(row, CH), :], wsem.at[blk % 2])

    def start_chunks(blk, chunk_copy):
        nv = nval_ref[blk]

        @pl.when(nv == CPB)
        def _():
            for c in range(CPB):
                chunk_copy(blk, c).start(priority=c % 2)

        @pl.when(nv < CPB)
        def _():
            def body(c, carry):
                chunk_copy(blk, c).start()
                return carry

            lax.fori_loop(0, nv, body, 0)

    def wait_chunks(blk, chunk_copy, whole_copy):
        nv = nval_ref[blk]

        @pl.when(nv == CPB)
        def _():
            whole_copy.wait()

        @pl.when(nv < CPB)
        def _():
            def body(c, carry):
                chunk_copy(blk, c).wait()
                return carry

            lax.fori_loop(0, nv, body, 0)

    def wait_reads(blk):
        wait_chunks(blk, copy_in, pltpu.make_async_copy(rows_hbm.at[pl.ds(0, FB), :], xbuf.at[blk % 2], rsem.at[blk % 2]))

    def wait_writes(blk):
        wait_chunks(blk, copy_out, pltpu.make_async_copy(ybuf.at[blk % 2], rows_hbm.at[pl.ds(0, FB), :], wsem.at[blk % 2]))

    @pl.when(i == 0)
    def _():
        xbuf[...] = jnp.zeros_like(xbuf)
        start_chunks(i, copy_in)

    @pl.when(i < last)
    def _():
        start_chunks(i + 1, copy_in)

    @pl.when((i == 0) | (e != e_prev))
    def _():
        win_s[...] = win_ref[...].astype(BF16)
        wout_s[...] = wout_ref[...].astype(BF16)

    @pl.when(i >= 2)
    def _():
        wait_writes(i - 2)

    @pl.when(i < nused_ref[0])
    def _():
        wait_reads(i)
        xa, xb = _unpack_pairs(xbuf[i % 2])
        hb = jnp.dot(xa, win_s[:D // 2, :], preferred_element_type=F32) \
            + jnp.dot(xb, win_s[D // 2:, :], preferred_element_type=F32) + bin_ref[...]
        g = jnp.minimum(hb[:, :D_FF], SWIGLU_LIMIT)
        u = jnp.clip(hb[:, D_FF:], -SWIGLU_LIMIT, SWIGLU_LIMIT)
        act = g * _sigmoid(SWIGLU_ALPHA * g) * (u + 1.0)
        y = jnp.dot(act.astype(BF16), wout_s[...], preferred_element_type=F32) + bout_ref[...]
        ybuf[i % 2] = _pack_pairs(y.astype(BF16).astype(F32))
        start_chunks(i, copy_out)

    @pl.when(i == last)
    def _():
        @pl.when(i >= 1)
        def _():
            wait_writes(i - 1)

        wait_writes(i)


def _expert_ffn(block_e, n_used, n_valid, src, xc, layer, w_in, b_in, w_out, b_out):
    n_blocks = block_e.shape[0]
    depth = w_in.shape[0]
    wspec = lambda shape: pl.BlockSpec((None, None) + shape, lambda i, be, *_: (layer, be[i], 0, 0))
    return pl.pallas_call(
        _ffn_kernel,
        grid_spec=pltpu.PrefetchScalarGridSpec(
            num_scalar_prefetch=4,
            grid=(n_blocks,),
            in_specs=[pl.BlockSpec(memory_space=pl.ANY), wspec((D, 2 * D_FF)), wspec((1, 2 * D_FF)),
                      wspec((D_FF, D)), wspec((1, D))],
            out_specs=pl.BlockSpec(memory_space=pl.ANY),
            scratch_shapes=[pltpu.VMEM((D, 2 * D_FF), BF16), pltpu.VMEM((D_FF, D), BF16),
                            pltpu.VMEM((2, FB, D // 2), U32), pltpu.VMEM((2, FB, D // 2), U32),
                            pltpu.SemaphoreType.DMA((2,)), pltpu.SemaphoreType.DMA((2,))],
        ),
        out_shape=jax.ShapeDtypeStruct(xc.shape, U32),
        input_output_aliases={4: 0},
        compiler_params=_cparams(("arbitrary",)),
        name="moe_expert_ffn",
    )(block_e, n_used, n_valid, src, xc, w_in, b_in.reshape(depth, N_EXPERTS, 1, 2 * D_FF), w_out,
      b_out.reshape(depth, N_EXPERTS, 1, D))


def _moe_plan(cnt):
    nt = cnt.shape[0]
    n_blocks = (nt * TM * TOP_K + nt * N_EXPERTS * (CH - 1)) // FB + N_EXPERTS + 1
    n = cnt[:, 0, :N_EXPERTS].astype(I32)
    nch = (n + CH - 1) // CH
    cend = jnp.cumsum(nch, axis=0)
    ce = cend[-1]
    pce = (ce + CPB - 1) // CPB * CPB
    e_end = jnp.cumsum(pce)
    e_start = e_end - pce
    first_slot = CH * (jnp.cumsum(nch, axis=1) - nch)

    blk0 = jnp.arange(n_blocks, dtype=I32) * CPB
    block_e = jnp.minimum(jnp.sum((blk0[:, None] >= e_end[None, :]).astype(I32), axis=-1), N_EXPERTS - 1)
    pick = lambda table, idx: jnp.sum(jnp.where(idx[:, None] == jnp.arange(table.shape[0], dtype=I32), table, 0), axis=-1)
    n_valid = jnp.clip(pick(e_start + ce, block_e) - blk0, 0, CPB)
    n_used = (e_end[-1] // CPB).reshape(1)

    d = jnp.arange(n_blocks * CPB, dtype=I32)
    e_of = jnp.repeat(block_e, CPB)
    local = d - pick(e_start, e_of)
    onehot_e = (e_of[:, None] == jnp.arange(N_EXPERTS, dtype=I32)).astype(F32)
    by_expert = lambda m: jnp.dot(onehot_e, m.T.astype(F32), precision=lax.Precision.HIGHEST).astype(I32)
    cend_e = by_expert(cend)
    in_tile = (local[:, None] >= cend_e - by_expert(nch)) & (local[:, None] < cend_e)
    tile_iota = jnp.arange(nt, dtype=I32)
    at_tile = lambda m: jnp.sum(jnp.where(in_tile, m, 0), axis=-1)
    src = at_tile(tile_iota * SLOTS + by_expert(first_slot) + CH * (local[:, None] - (cend_e - by_expert(nch))))
    return src.astype(I32), block_e.astype(I32), n_valid.astype(I32), n_used.astype(I32)


def _moe(cnt, xc, layer, w_in, b_in, w_out, b_out):
    src, block_e, n_valid, n_used = _moe_plan(cnt)
    return _expert_ffn(block_e, n_used, n_valid, src, xc, layer, w_in, b_in, w_out, b_out)


def _final_kernel(npt, rows_ref, x1_ref, route_ref, modp_ref, g_ref, op_ref, os_ref):
    i = pl.program_id(0)
    x2 = x1_ref[...] + modp_ref[5:6, :] * _combine(route_ref[...], rows_ref[...])
    out = _rms(x2, g_ref[...])

    @pl.when(i < npt)
    def _():
        op_ref[...] = out

    @pl.when(i >= npt)
    def _():
        os_ref[...] = out


def _rope_tables(n_tokens):
    rows = n_tokens // GRID_W
    pos_row = jnp.repeat(jnp.arange(rows, dtype=F32), GRID_W)
    pos_col = jnp.tile(jnp.arange(GRID_W, dtype=F32), rows)
    n_freq = ROPE_DIM // 4
    inv_freq = ROPE_BASE ** (-jnp.arange(n_freq, dtype=F32) / n_freq)
    ang = jnp.stack([pos_row[:, None] * inv_freq, pos_col[:, None] * inv_freq], axis=1)
    cos, sin = jnp.cos(ang), jnp.sin(ang)
    zero = jnp.zeros_like(sin)
    c = jnp.stack([cos, cos], axis=2).reshape(n_tokens, ROPE_DIM)
    sm = jnp.stack([-sin, zero], axis=2).reshape(n_tokens, ROPE_DIM)
    sp = jnp.stack([zero, sin], axis=2).reshape(n_tokens, ROPE_DIM)

    def place(a, fill):
        left = jnp.full((n_tokens, ROPE_LANE0), fill, F32)
        right = jnp.full((n_tokens, LANES - ROPE_LANE0 - ROPE_DIM), fill, F32)
        body = jnp.concatenate([left, a, right], axis=1)
        ident = jnp.full((TM, LANES), fill, F32)
        return jnp.concatenate([body, ident], axis=0)

    return jnp.stack([place(c, 1.0), place(sm, 0.0), place(sp, 0.0)], axis=0)


def kernel(x_prompt, x_sample, c, cache_mla_ckv, cache_mla_krope, state_mlstm_C, state_mlstm_n, state_mlstm_m, c_ctx, ada_w, ada_b, norm1_g, norm2_g, ab_w_in, mla_q_norm_g, mla_w_uq, mla_kv_norm_g, mla_w_ukv, mlstm_gate_b, mlstm_norm_g, ab_w_out, conv_w_pw1, conv_w_dw, conv_b_dw, conv_ln_g, conv_ln_b, conv_w_pw2, router_w, router_b, moe_w_in, moe_b_in, moe_w_out, moe_b_out, final_norm_g):
    bp, sp_, _ = x_prompt.shape
    bs, ss, _ = x_sample.shape
    past = cache_mla_ckv.shape[2]
    tp, ts = bp * sp_, bs * ss
    t = tp + ts
    nt = t // TM
    npt = tp // TM
    tps = ss // TM
    assert sp_ == TM and ss % TM == 0 and bs <= CTX_ROW and tp % ss == 0

    xp2 = x_prompt.reshape(tp, D)
    xs2 = x_sample.reshape(ts, D)
    cond = jnp.zeros((N_COND, D), F32).at[:bs].set(c).at[CTX_ROW].set(c_ctx)
    mod = _modulation(cond, ada_w, ada_b)

    def mod_row(i):
        return jnp.where(i < npt, CTX_ROW, (i - npt) // tps)

    def mod_spec(layer):
        return pl.BlockSpec((None, None, 6, D), lambda i, *_: (layer, mod_row(i), 0, 0))

    row_spec = lambda w: pl.BlockSpec((TM, w), lambda i, *_: (i, 0))
    prompt_spec = pl.BlockSpec((TM, D), lambda i, *_: (jnp.minimum(i, npt - 1), 0))
    sample_spec = pl.BlockSpec((TM, D), lambda i, *_: (jnp.maximum(i - npt, 0), 0))
    full = lambda a: pl.BlockSpec(a.shape, lambda i, *_: (0,) * a.ndim)
    vec = lambda a: a.reshape(1, -1)

    def router_args(layer):
        wr = jnp.pad(router_w[layer], ((0, 0), (0, LANES - N_EXPERTS))).astype(BF16)
        br = jnp.pad(router_b[layer], (0, LANES - N_EXPERTS)).reshape(1, LANES)
        return wr, br

    moe_out_shapes = [
        jax.ShapeDtypeStruct((t, D), F32),
        jax.ShapeDtypeStruct((nt * SLOTS, D // 2), U32),
        jax.ShapeDtypeStruct((t, LANES), F32),
        jax.ShapeDtypeStruct((nt, SUBLANES, LANES), F32),
    ]
    moe_out_specs = [row_spec(D), pl.BlockSpec((SLOTS, D // 2), lambda i: (i, 0)), row_spec(LANES),
                     pl.BlockSpec((None, SUBLANES, LANES), lambda i: (i, 0, 0))]

    j = 0
    w = ab_w_in[j]
    zc = lambda n: jnp.zeros((D, n), F32)
    misc_w = jnp.concatenate([w[:, 2464:2480], zc(ROPE_LANE0 - 16), w[:, 384:416], zc(LANES - ROPE_LANE0 - ROPE_DIM)], axis=1)
    w_in = jnp.concatenate([w[:, :384], misc_w, w[:, 416:2464]], axis=1).astype(BF16)
    wq = jnp.pad(mla_w_uq[j].reshape(Q_RANK, H_A, NOPE + ROPE_DIM), ((0, 0), (0, 0), (0, HEAD_PAD - NOPE - ROPE_DIM)))
    wq = wq.reshape(Q_RANK, H_A * HEAD_PAD).astype(BF16)
    wkv = mla_w_ukv[j].reshape(KV_RANK, H_A, NOPE + V_A)
    wk = jnp.pad(wkv[:, :, :NOPE], ((0, 0), (0, 0), (0, HEAD_PAD - NOPE))).reshape(KV_RANK, H_A * HEAD_PAD).astype(BF16)
    wv = jnp.pad(wkv[:, :, NOPE:], ((0, 0), (0, 0), (0, HEAD_PAD - V_A))).reshape(KV_RANK, H_A * HEAD_PAD).astype(BF16)
    gate_b = jnp.pad(mlstm_gate_b[j], (0, LANES - 4 * H_B)).reshape(1, LANES)
    tabs = _rope_tables(ss)
    pos_blocks = ss // TM

    def tab_block(i):
        return jnp.where(i < npt, pos_blocks, (i - npt) % tps)

    nctx = bs * past
    ctx_ckv = cache_mla_ckv[:, j].reshape(nctx, KV_RANK)
    ctx_kr = jnp.pad(cache_mla_krope[:, j].reshape(nctx, ROPE_DIM), ((0, 0), (ROPE_LANE0, LANES - ROPE_LANE0 - ROPE_DIM)))
    nct, cps, nts = nctx // TM, past // TM, ts // TM
    kps = cps + tps
    kv_rows = bs * (past + ss) + tp

    def kv_block(i):
        own = ((i - npt) // tps) * kps + cps + (i - npt) % tps
        cached = ((i - nt) // cps) * kps + (i - nt) % cps
        return jnp.where(i < npt, bs * kps + i, jnp.where(i < nt, own, cached))

    tile_spec = lambda w: pl.BlockSpec((TM, w), lambda i: (jnp.minimum(i, nt - 1), 0))
    ctx_spec = pl.BlockSpec((TM, LANES), lambda i: (jnp.clip(i - nt, 0, nct - 1), 0))
    kv_spec = pl.BlockSpec((TM, H_A * HEAD_PAD), lambda i: (kv_block(i), 0))
    ins = [xp2, xs2, ctx_ckv, ctx_kr, mod, vec(norm1_g[0]), w_in, vec(mla_q_norm_g[j]), wq, vec(mla_kv_norm_g[j]), wk, wv,
           gate_b, tabs]
    in_specs = [prompt_spec, pl.BlockSpec((TM, D), lambda i: (jnp.clip(i - npt, 0, nts - 1), 0)), ctx_spec, ctx_spec,
                mod_spec(0)] + [full(a) for a in ins[5:13]] + [
        pl.BlockSpec((3, TM, LANES), lambda i: (0, tab_block(i), 0))]
    q, k, v, ckv, misc, mqkv, mo = pl.pallas_call(
        functools.partial(_ab_in_kernel, npt, nt),
        grid=(nt + nct,),
        in_specs=in_specs,
        out_specs=[tile_spec(H_A * HEAD_PAD), kv_spec, kv_spec, tile_spec(KV_RANK), tile_spec(LANES),
                   tile_spec(3 * MB), tile_spec(MB)],
        out_shape=[
            jax.ShapeDtypeStruct((t, H_A * HEAD_PAD), BF16), jax.ShapeDtypeStruct((kv_rows, H_A * HEAD_PAD), BF16),
            jax.ShapeDtypeStruct((kv_rows, H_A * HEAD_PAD), BF16), jax.ShapeDtypeStruct((t, KV_RANK), F32),
            jax.ShapeDtypeStruct((t, LANES), F32), jax.ShapeDtypeStruct((t, 3 * MB), BF16),
            jax.ShapeDtypeStruct((t, MB), F32)],
        compiler_params=_cparams(("arbitrary",)),
        name="ab_in_proj",
    )(*ins)

    oa_p = _attention(q, k, v, 0, bp, sp_, bs * (past + ss), sp_)
    oa_s = _attention(q, k, v, tp, bs, ss, 0, past + ss)

    zc0 = jnp.zeros((bp, 2, H_B, DH_B, DH_B), F32)
    zn0 = jnp.zeros((bp, 2, H_B, 1, DH_B), F32)
    zm0 = jnp.zeros((bp, 2, H_B, 1, LANES), F32)
    hf_p, hb_p, c_p, n_p, m_p = _mlstm(mqkv, misc, zc0, zn0, zm0, 0, bp, sp_)
    sm0 = jnp.broadcast_to(state_mlstm_m[:, j][..., None, None], (bs, 2, H_B, 1, LANES))
    hf_s, hb_s, _, _, _ = _mlstm(mqkv, misc, state_mlstm_C[:, j], state_mlstm_n[:, j][:, :, :, None, :], sm0, tp, bs, ss)

    wr, br = router_args(0)
    prompt_w = lambda w: pl.BlockSpec((TM, w), lambda i: (jnp.minimum(i, npt - 1), 0))
    sample_w = lambda w: pl.BlockSpec((TM, w), lambda i: (jnp.maximum(i - npt, 0), 0))
    ins = [oa_p, oa_s, hf_p, hf_s, hb_p, hb_s, mo, vec(mlstm_norm_g[j]), ab_w_out[j].astype(BF16), xp2, xs2, mod,
           vec(norm2_g[0]), wr, br]
    in_specs = [prompt_w(H_A * V_A), sample_w(H_A * V_A), prompt_w(MB), sample_w(MB), prompt_w(MB), sample_w(MB),
                row_spec(MB), full(ins[7]), full(ins[8]),
                prompt_spec, sample_spec, mod_spec(0), full(ins[12]), full(wr), full(br)]
    x1, xc, route, cnt = pl.pallas_call(
        functools.partial(_ab_out_kernel, npt),
        grid=(nt,),
        in_specs=in_specs,
        out_specs=moe_out_specs,
        out_shape=moe_out_shapes,
        compiler_params=_cparams(("arbitrary",)),
        name="ab_out_proj",
    )(*ins)
    ys = _moe(cnt, xc, 0, moe_w_in, moe_b_in, moe_w_out, moe_b_out)

    slots_spec = pl.BlockSpec((SLOTS, D // 2), lambda i, *_: (i, 0))
    ins = [ys, x1, route, mod, mod, vec(norm1_g[1]), conv_w_pw1[0].astype(BF16)]
    x2, u = pl.pallas_call(
        _conv_in_kernel,
        grid=(nt,),
        in_specs=[slots_spec, row_spec(D), row_spec(LANES), mod_spec(0), mod_spec(1), full(ins[5]), full(ins[6])],
        out_specs=[row_spec(D), row_spec(D)],
        out_shape=[jax.ShapeDtypeStruct((t, D), F32), jax.ShapeDtypeStruct((t, D), F32)],
        compiler_params=_cparams(("arbitrary",)),
        name="conv_in_proj",
    )(*ins)

    hpt = TM // HALO
    nhalo = t // HALO
    wr, br = router_args(1)
    ins = [u, u, u, conv_w_dw[0], vec(conv_b_dw[0]), vec(conv_ln_g[0]), vec(conv_ln_b[0]), conv_w_pw2[0].astype(BF16),
           x2, mod, vec(norm2_g[1]), wr, br]
    in_specs = [row_spec(D),
                pl.BlockSpec((HALO, D), lambda i: (jnp.maximum(i * hpt - 1, 0), 0)),
                pl.BlockSpec((HALO, D), lambda i: (jnp.minimum((i + 1) * hpt, nhalo - 1), 0)),
                full(ins[3]), full(ins[4]), full(ins[5]), full(ins[6]), full(ins[7]), row_spec(D), mod_spec(1),
                full(ins[10]), full(wr), full(br)]
    x1, xc, route, cnt = pl.pallas_call(
        functools.partial(_conv_out_kernel, npt, tps),
        grid=(nt,),
        in_specs=in_specs,
        out_specs=moe_out_specs,
        out_shape=moe_out_shapes,
        scratch_shapes=[pltpu.VMEM((TM + 2 * HALO, D), F32), pltpu.VMEM((SUBLANES, TM + 24, D), F32)],
        compiler_params=_cparams(("arbitrary",)),
        name="conv_out_proj",
    )(*ins)
    ys = _moe(cnt, xc, 1, moe_w_in, moe_b_in, moe_w_out, moe_b_out)

    ins = [ys, x1, route, mod, vec(final_norm_g)]
    y_p, y_s = pl.pallas_call(
        functools.partial(_final_kernel, npt),
        grid=(nt,),
        in_specs=[slots_spec, row_spec(D), row_spec(LANES), mod_spec(1), full(ins[4])],
        out_specs=[prompt_spec, sample_spec],
        out_shape=[jax.ShapeDtypeStruct((tp, D), F32), jax.ShapeDtypeStruct((ts, D), F32)],
        compiler_params=_cparams(("arbitrary",)),
        name="final_norm",
    )(*ins)

    y_prompt = y_p.reshape(bp, sp_, D)
    y_sample = y_s.reshape(bs, ss, D)
    new_ckv = ckv[:tp].reshape(bp, 1, sp_, KV_RANK)
    new_krope = misc[:tp, ROPE_LANE0:ROPE_LANE0 + ROPE_DIM].reshape(bp, 1, sp_, ROPE_DIM)
    new_c = c_p[:, None]
    new_n = n_p[:, None, :, :, 0, :]
    new_m = m_p[:, None, :, :, 0, 0]
    return (y_prompt, y_sample, new_ckv, new_krope, new_c, new_n, new_m)
```

```python
import functools
import math

import jax
import jax.numpy as jnp
from jax import lax
from jax.experimental import pallas as pl
from jax.experimental.pallas import tpu as pltpu

F32 = jnp.float32
BF16 = jnp.bfloat16
I32 = jnp.int32
U32 = jnp.uint32

D = 1024
GRID_W = 64
H_A = 8
Q_RANK = 256
KV_RANK = 128
NOPE = 64
ROPE_DIM = 32
V_A = 64
ROPE_BASE = 10000.0
H_B = 4
DH_B = 128
MB = H_B * DH_B
CONV_K = 31
N_EXPERTS = 32
TOP_K = 4
D_FF = 1024
SWIGLU_LIMIT = 7.0
SWIGLU_ALPHA = 1.702

LANES = 128
SUBLANES = 8
HEAD_PAD = 128
ROPE_LANE0 = 64
N_COND = 16
CTX_ROW = 8

TM = 256
FB = 512
LC = 256
TQ = 256
HPS = 4
CH = SUBLANES
SLOTS = TOP_K * TM + N_EXPERTS * CH
VMEM_LIMIT = 56 * 1024 * 1024

NEG_INF = float("-inf")
LOG2_E = math.log2(math.e)


def _cparams(sem):
    return pltpu.CompilerParams(dimension_semantics=sem, vmem_limit_bytes=VMEM_LIMIT)


def _sigmoid(x):
    return 1.0 / (1.0 + jnp.exp(-x))


def _rms(x, g, eps=1e-6):
    return x * lax.rsqrt(jnp.mean(x * x, axis=-1, keepdims=True) + eps) * g


def _lane(shape):
    return lax.broadcasted_iota(I32, shape, len(shape) - 1)


def _pack_pairs(x):
    w = x.shape[1] // 2
    return pltpu.bitcast(x[:, :w], U32) | (pltpu.bitcast(x[:, w:], U32) >> 16)


def _unpack_pairs(wd):
    a = pltpu.bitcast(wd & jnp.uint32(0xFFFF0000), F32).astype(BF16)
    b = pltpu.bitcast(wd << 16, F32).astype(BF16)
    return a, b


def _mod_kernel(cond_ref, w_ref, b_ref, o_ref):
    c = cond_ref[...]
    s = (c * _sigmoid(c)).astype(BF16)
    o_ref[...] = jnp.dot(s, w_ref[...].astype(BF16), preferred_element_type=F32) + b_ref[...]


def _modulation(cond, ada_w, ada_b):
    depth = ada_w.shape[0]
    nj = ada_w.shape[2] // D
    out = pl.pallas_call(
        _mod_kernel,
        grid=(depth, nj),
        in_specs=[
            pl.BlockSpec((N_COND, D), lambda l, j: (0, 0)),
            pl.BlockSpec((None, D, D), lambda l, j: (l, 0, j)),
            pl.BlockSpec((None, 1, D), lambda l, j: (l, 0, j)),
        ],
        out_specs=pl.BlockSpec((None, N_COND, D), lambda l, j: (l, 0, j)),
        out_shape=jax.ShapeDtypeStruct((depth, N_COND, nj * D), F32),
        compiler_params=_cparams(("arbitrary", "arbitrary")),
        name="ada_modulation",
    )(cond, ada_w, ada_b.reshape(depth, 1, nj * D))
    return out.reshape(depth, N_COND, nj, D)


def _combine(route, rows):
    lane = _lane((route.shape[0], SLOTS)).astype(F32)
    g = jnp.zeros(lane.shape, F32)
    for k in range(TOP_K):
        g = jnp.where(lane == route[:, 8 + k:9 + k], route[:, 4 + k:5 + k], g)
    gb = g.astype(BF16)
    ya, yb = _unpack_pairs(rows)
    return jnp.concatenate([jnp.dot(gb, ya, preferred_element_type=F32),
                            jnp.dot(gb, yb, preferred_element_type=F32)], axis=-1)


def _moe_route(x1, mod_ref, n2g, wr_ref, br_ref, xc_ref, route_ref, cnt_ref):
    h2 = _rms(x1, n2g) * (1.0 + mod_ref[4:5, :]) + mod_ref[3:4, :]
    h2b = h2.astype(BF16)
    logits = jnp.dot(h2b, wr_ref[...], preferred_element_type=F32) + br_ref[...]
    tm = logits.shape[0]
    lane = _lane((tm, LANES))
    lanef = lane.astype(F32)
    l = jnp.where(lane < N_EXPERTS, logits, NEG_INF)
    tops, idxs, sels = [], [], []
    for _ in range(TOP_K):
        mv = jnp.max(l, axis=-1, keepdims=True)
        idx = jnp.min(jnp.where(l == mv, lanef, float(LANES)), axis=-1, keepdims=True)
        sel = lanef == idx
        l = jnp.where(sel, NEG_INF, l)
        tops.append(mv)
        idxs.append(idx)
        sels.append(sel)
    ex = [jnp.exp(t - tops[0]) for t in tops]
    den = ex[0] + ex[1] + ex[2] + ex[3]
    gates = [e / den for e in ex]

    onehot = jnp.zeros((tm, LANES), F32)
    for sel in sels:
        onehot = onehot + jnp.where(sel, 1.0, 0.0)
    row = lax.broadcasted_iota(I32, (tm, tm), 0)
    col = lax.broadcasted_iota(I32, (tm, tm), 1)
    strict = jnp.where(col < row, 1.0, 0.0).astype(BF16)
    before = jnp.dot(strict, onehot.astype(BF16), preferred_element_type=F32)
    n_e = jnp.sum(onehot, axis=0, keepdims=True)
    cnt_ref[...] = jnp.broadcast_to(n_e, cnt_ref.shape)
    nch = jnp.right_shift(n_e.astype(I32) + (CH - 1), CH.bit_length() - 1).astype(F32)
    r128 = lax.broadcasted_iota(I32, (LANES, LANES), 0)
    c128 = lax.broadcasted_iota(I32, (LANES, LANES), 1)
    upper = jnp.where(r128 < c128, 1.0, 0.0).astype(BF16)
    seg0 = float(CH) * jnp.dot(jnp.broadcast_to(nch, (SUBLANES, LANES)).astype(BF16), upper,
                               preferred_element_type=F32)[0:1, :]
    slot_of = seg0 + before
    slots = [jnp.sum(jnp.where(sel, slot_of, 0.0), axis=-1, keepdims=True) for sel in sels]

    r = jnp.zeros((tm, LANES), F32)
    for j, colv in enumerate(idxs + gates + slots):
        r = jnp.where(lane == j, colv, r)
    route_ref[...] = r

    lane_s = _lane((tm, SLOTS)).astype(F32)
    pt = jnp.zeros(lane_s.shape, F32)
    for k in range(TOP_K):
        pt = jnp.where(lane_s == slots[k], 1.0, pt)
    xc = jnp.dot(pt.T.astype(BF16), h2b, preferred_element_type=F32)
    xc_ref[...] = _pack_pairs(xc)


def _values_with_ones(ckvb, wv_ref):
    vv = jnp.dot(ckvb, wv_ref[...], preferred_element_type=F32)
    return jnp.where((_lane(vv.shape) & (HEAD_PAD - 1)) == V_A, 1.0, vv).astype(BF16)


def _rope(x, tab_ref):
    return x * tab_ref[0] + pltpu.roll(x, LANES - 8, 1) * tab_ref[1] + pltpu.roll(x, 8, 1) * tab_ref[2]


def _ab_in_kernel(npt, nt, *refs):
    @pl.when(pl.program_id(0) < nt)
    def _():
        _ab_in_tokens(npt, *refs)

    @pl.when(pl.program_id(0) >= nt)
    def _():
        _ab_in_context(*refs)


def _ab_in_context(xp_ref, xs_ref, cckv_ref, ckr_ref, mod_ref, n1g_ref, win_ref, qg_ref, wq_ref, kvg_ref, wk_ref, wv_ref,
                   gb_ref, tab_ref, q_ref, k_ref, v_ref, ckv_ref, misc_ref, mqkv_ref, mo_ref):
    ckvb = cckv_ref[...].astype(BF16)
    kn = jnp.dot(ckvb, wk_ref[...], preferred_element_type=F32)
    v_ref[...] = _values_with_ones(ckvb, wv_ref)
    kr = ckr_ref[...]
    for hh in range(H_A):
        sl = slice(hh * HEAD_PAD, (hh + 1) * HEAD_PAD)
        k_ref[:, sl] = (kn[:, sl] + kr).astype(BF16)


def _ab_in_tokens(npt, xp_ref, xs_ref, cckv_ref, ckr_ref, mod_ref, n1g_ref, win_ref, qg_ref, wq_ref, kvg_ref, wk_ref,
                  wv_ref, gb_ref, tab_ref, q_ref, k_ref, v_ref, ckv_ref, misc_ref, mqkv_ref, mo_ref):
    x = jnp.where(pl.program_id(0) < npt, xp_ref[...], xs_ref[...])
    h = _rms(x, n1g_ref[...]) * (1.0 + mod_ref[1:2, :]) + mod_ref[0:1, :]
    z = jnp.dot(h.astype(BF16), win_ref[...], preferred_element_type=F32)

    qn = _rms(z[:, :Q_RANK], qg_ref[...]).astype(BF16)
    q = jnp.dot(qn, wq_ref[...], preferred_element_type=F32)
    ckv = _rms(z[:, Q_RANK:Q_RANK + KV_RANK], kvg_ref[...])
    ckv_ref[...] = ckv
    ckvb = ckv.astype(BF16)
    kn = jnp.dot(ckvb, wk_ref[...], preferred_element_type=F32)
    v_ref[...] = _values_with_ones(ckvb, wv_ref)

    misc = z[:, Q_RANK + KV_RANK:Q_RANK + KV_RANK + LANES]
    misc_ref[...] = misc + gb_ref[...]
    lane = _lane(misc.shape)
    kr = jnp.where((lane >= ROPE_LANE0) & (lane < ROPE_LANE0 + ROPE_DIM), _rope(misc, tab_ref), 0.0)
    scale = float((NOPE + ROPE_DIM) ** -0.5 * LOG2_E)
    for hh in range(H_A):
        sl = slice(hh * HEAD_PAD, (hh + 1) * HEAD_PAD)
        q_ref[:, sl] = (_rope(q[:, sl], tab_ref) * scale).astype(BF16)
        k_ref[:, sl] = (kn[:, sl] + kr).astype(BF16)

    m0 = Q_RANK + KV_RANK + LANES
    mqkv_ref[:, :MB] = z[:, m0:m0 + MB].astype(BF16)
    mqkv_ref[:, MB:2 * MB] = (z[:, m0 + MB:m0 + 2 * MB] * float(DH_B ** -0.5)).astype(BF16)
    mqkv_ref[:, 2 * MB:] = z[:, m0 + 2 * MB:m0 + 3 * MB].astype(BF16)
    mo_ref[...] = z[:, m0 + 3 * MB:m0 + 4 * MB]


def _attn_kernel(q_ref, k_ref, v_ref, o_ref):
    outs = []
    for j in range(HPS):
        sl = slice(j * HEAD_PAD, (j + 1) * HEAD_PAD)
        s = lax.dot_general(q_ref[:, sl], k_ref[:, sl], (((1,), (1,)), ((), ())), preferred_element_type=F32)
        m = jnp.max(s, axis=-1, keepdims=True)
        p = jnp.exp2(s - m).astype(BF16)
        o = jnp.dot(p, v_ref[:, sl], preferred_element_type=F32)
        outs.append(o / o[:, V_A:V_A + 1])
    lane = _lane(outs[0].shape)
    for j in range(0, HPS, 2):
        o_ref[:, j * V_A:(j + 2) * V_A] = jnp.where(lane < V_A, outs[j], pltpu.roll(outs[j + 1], V_A, 1)).astype(BF16)


def _attention(q, k, v, qrow0, nb, s, krow0, l):
    tq = min(TQ, s)
    nq = s // tq
    qrow = lambda bi, hp, qi: (qrow0 // tq + bi * nq + qi, hp)
    krow = lambda bi, hp, qi: (krow0 // l + bi, hp)
    return pl.pallas_call(
        _attn_kernel,
        grid=(nb, H_A // HPS, nq),
        in_specs=[pl.BlockSpec((tq, HPS * HEAD_PAD), qrow), pl.BlockSpec((l, HPS * HEAD_PAD), krow),
                  pl.BlockSpec((l, HPS * HEAD_PAD), krow)],
        out_specs=pl.BlockSpec((tq, HPS * V_A), lambda bi, hp, qi: (bi * nq + qi, hp)),
        out_shape=jax.ShapeDtypeStruct((nb * s, H_A * V_A), BF16),
        compiler_params=_cparams(("arbitrary", "arbitrary", "arbitrary")),
        name="mla_attention",
    )(q, k, v)


def _log_sigmoid(x):
    return jnp.minimum(x, 0.0) - jnp.log(1.0 + jnp.exp(-jnp.abs(x)))


def _mlstm_direction(d, q_ref, k_ref, v_ref, g_ref, h_ref, c_s, n_s, m_s):
    gates = g_ref[...]
    lf_all = _log_sigmoid(gates)
    row = lax.broadcasted_iota(I32, (LC, LC), 0)
    col = lax.broadcasted_iota(I32, (LC, LC), 1)
    mask = (col <= row) if d == 0 else (col >= row)
    eye = row == col
    ones = jnp.where(mask, 1.0, 0.0).astype(BF16)
    cum = jnp.zeros((LC, LANES), F32)
    rest = lf_all
    for _ in range(3):
        piece = rest.astype(BF16)
        cum = cum + jnp.dot(ones, piece, preferred_element_type=F32)
        rest = rest - piece.astype(F32)

    for hh in range(H_B):
        sl = slice(hh * DH_B, (hh + 1) * DH_B)
        q = q_ref[:, sl]
        k = k_ref[:, sl]
        v = v_ref[:, sl]
        b_col = cum[:, 8 * d + 4 + hh:8 * d + 5 + hh]
        i_col = gates[:, 8 * d + hh:8 * d + hh + 1]
        m_prev = m_s[d, hh][:, 0:1]
        r_row = jnp.sum(jnp.where(eye, i_col - b_col, 0.0), axis=0, keepdims=True)
        dm = jnp.where(mask, b_col + r_row, NEG_INF)
        a_col = b_col + m_prev
        mt = jnp.maximum(a_col, jnp.max(dm, axis=-1, keepdims=True))
        w_inter = jnp.exp(a_col - mt)
        qk = lax.dot_general(q, k, (((1,), (1,)), ((), ())), preferred_element_type=F32)
        s = qk * jnp.exp(dm - mt)
        c_prev = c_s[d, hh]
        n_prev = n_s[d, hh]
        num = w_inter * jnp.dot(q, c_prev.astype(BF16), preferred_element_type=F32) \
            + jnp.dot(s.astype(BF16), v, preferred_element_type=F32)
        n_rows = jnp.broadcast_to(n_prev, (SUBLANES, DH_B)).astype(BF16)
        qn = lax.dot_general(q, n_rows, (((1,), (1,)), ((), ())), preferred_element_type=F32)[:, 0:1]
        den = w_inter * qn + jnp.sum(s, axis=-1, keepdims=True)
        h_ref[:, sl] = num / jnp.maximum(jnp.abs(den), jnp.exp(-mt))

        b_tot = b_col[LC - 1:LC, :] if d == 0 else b_col[0:1, :]
        g_col = b_tot - b_col + i_col
        m_new = jnp.maximum(b_tot + m_prev, jnp.max(g_col, axis=0, keepdims=True))
        w_c = jnp.exp(b_tot + m_prev - m_new)
        kw = k.astype(F32) * jnp.exp(g_col - m_new)
        c_s[d, hh] = w_c * c_prev + jnp.dot(kw.T.astype(BF16), v, preferred_element_type=F32)
        n_s[d, hh] = w_c * n_prev + jnp.sum(kw, axis=0, keepdims=True)
        m_s[d, hh] = jnp.broadcast_to(m_new, (1, LANES))


def _mlstm_kernel(qf_ref, kf_ref, vf_ref, gf_ref, qb_ref, kb_ref, vb_ref, gb_ref, c0_ref, n0_ref, m0_ref,
                  hf_ref, hb_ref, c_out, n_out, m_out, c_s, n_s, m_s):
    c = pl.program_id(1)

    @pl.when(c == 0)
    def _():
        c_s[...] = c0_ref[...]
        n_s[...] = n0_ref[...]
        m_s[...] = m0_ref[...]

    _mlstm_direction(0, qf_ref, kf_ref, vf_ref, gf_ref, hf_ref, c_s, n_s, m_s)
    _mlstm_direction(1, qb_ref, kb_ref, vb_ref, gb_ref, hb_ref, c_s, n_s, m_s)

    @pl.when(c == pl.num_programs(1) - 1)
    def _():
        c_out[...] = c_s[...]
        n_out[...] = n_s[...]
        m_out[...] = m_s[...]


def _mlstm(mqkv, misc, c0, n0, m0, row0, nb, s):
    nc = s // LC
    blk0 = row0 // LC
    fwd = lambda bi, ci: blk0 + bi * nc + ci
    bwd = lambda bi, ci: blk0 + bi * nc + nc - 1 - ci
    chunk_specs = lambda rows: [
        pl.BlockSpec((LC, MB), lambda bi, ci: (rows(bi, ci), 0)),
        pl.BlockSpec((LC, MB), lambda bi, ci: (rows(bi, ci), 1)),
        pl.BlockSpec((LC, MB), lambda bi, ci: (rows(bi, ci), 2)),
        pl.BlockSpec((LC, LANES), lambda bi, ci: (rows(bi, ci), 0)),
    ]
    state_spec = lambda shape: pl.BlockSpec((None,) + shape, lambda bi, ci: (bi,) + (0,) * len(shape))
    state_specs = [state_spec((2, H_B, DH_B, DH_B)), state_spec((2, H_B, 1, DH_B)), state_spec((2, H_B, 1, LANES))]
    return pl.pallas_call(
        _mlstm_kernel,
        grid=(nb, nc),
        in_specs=chunk_specs(fwd) + chunk_specs(bwd) + state_specs,
        out_specs=[
            pl.BlockSpec((LC, MB), lambda bi, ci: (bi * nc + ci, 0)),
            pl.BlockSpec((LC, MB), lambda bi, ci: (bi * nc + nc - 1 - ci, 0)),
        ] + state_specs,
        out_shape=[
            jax.ShapeDtypeStruct((nb * s, MB), F32),
            jax.ShapeDtypeStruct((nb * s, MB), F32),
            jax.ShapeDtypeStruct((nb, 2, H_B, DH_B, DH_B), F32),
            jax.ShapeDtypeStruct((nb, 2, H_B, 1, DH_B), F32),
            jax.ShapeDtypeStruct((nb, 2, H_B, 1, LANES), F32),
        ],
        scratch_shapes=[
            pltpu.VMEM((2, H_B, DH_B, DH_B), F32),
            pltpu.VMEM((2, H_B, 1, DH_B), F32),
            pltpu.VMEM((2, H_B, 1, LANES), F32),
        ],
        compiler_params=_cparams(("arbitrary", "arbitrary")),
        name="mlstm_scan",
    )(mqkv, mqkv, mqkv, misc, mqkv, mqkv, mqkv, misc, c0, n0, m0)


def _ab_out_kernel(npt, oap_ref, oas_ref, hfp_ref, hfs_ref, hbp_ref, hbs_ref, mo_ref, mg_ref, wout_ref, xp_ref, xs_ref,
                   mod_ref, n2g_ref, wr_ref, br_ref, x1_ref, xc_ref, route_ref, cnt_ref):
    in_prompt = pl.program_id(0) < npt
    pick = lambda p_ref, s_ref: jnp.where(in_prompt, p_ref[...], s_ref[...])
    hsum = pick(hfp_ref, hfs_ref) + pick(hbp_ref, hbs_ref)
    parts = []
    for hh in range(H_B):
        hh_ = hsum[:, hh * DH_B:(hh + 1) * DH_B]
        mu = jnp.mean(hh_, axis=-1, keepdims=True)
        var = jnp.mean(jnp.square(hh_ - mu), axis=-1, keepdims=True)
        parts.append((hh_ - mu) * lax.rsqrt(var + 1e-5))
    hn = jnp.concatenate(parts, axis=-1) * mg_ref[...]
    ob = (hn * _sigmoid(mo_ref[...])).astype(BF16)
    o = jnp.dot(pick(oap_ref, oas_ref), wout_ref[:H_A * V_A, :], preferred_element_type=F32) \
        + jnp.dot(ob, wout_ref[H_A * V_A:, :], preferred_element_type=F32)
    x = pick(xp_ref, xs_ref)
    x1 = x + mod_ref[2:3, :] * o
    x1_ref[...] = x1
    _moe_route(x1, mod_ref, n2g_ref[...], wr_ref, br_ref, xc_ref, route_ref, cnt_ref)


CT = 2


def _conv_in_kernel(rows_ref, x1_ref, route_ref, modp_ref, mod_ref, n1g_ref, w_ref, x2_ref, u_ref):
    y = jnp.concatenate([_combine(route_ref[k * TM:(k + 1) * TM, :], rows_ref[k * SLOTS:(k + 1) * SLOTS, :])
                         for k in range(CT)], axis=0)
    x2 = x1_ref[...] + modp_ref[5:6, :] * y
    x2_ref[...] = x2
    h = _rms(x2, n1g_ref[...]) * (1.0 + mod_ref[1:2, :]) + mod_ref[0:1, :]
    z = jnp.dot(h.astype(BF16), w_ref[...], preferred_element_type=F32)
    u_ref[...] = z[:, :D] * _sigmoid(z[:, D:])


HALO = 16


def _conv_out_kernel(npt, tiles_per_seq, u_ref, up_ref, un_ref, wdw_ref, bdw_ref, lg_ref, lb_ref, w2_ref,
                     x_ref, mod_ref, n2g_ref, wr_ref, br_ref, x1_ref, xc_ref, route_ref, cnt_ref, ext_ref, sh_ref):
    i = pl.program_id(0)
    j = i - npt
    in_prompt = i < npt
    first = in_prompt | (j % tiles_per_seq == 0)
    last = in_prompt | (j % tiles_per_seq == tiles_per_seq - 1)
    ext_ref[0:HALO, :] = jnp.where(first, 0.0, up_ref[...])
    ext_ref[HALO:HALO + TM, :] = u_ref[...]
    ext_ref[HALO + TM:, :] = jnp.where(last, 0.0, un_ref[...])

    for b in range(SUBLANES):
        sh_ref[b] = ext_ref[b + 1:b + 1 + TM + 24, :]
    acc = jnp.zeros((TM, D), F32) + bdw_ref[...]
    for tap in range(CONV_K):
        a, b = divmod(tap, SUBLANES)
        acc = acc + wdw_ref[tap:tap + 1, :] * sh_ref[b, SUBLANES * a:SUBLANES * a + TM, :]
    mu = jnp.mean(acc, axis=-1, keepdims=True)
    var = jnp.mean(jnp.square(acc - mu), axis=-1, keepdims=True)
    ln = (acc - mu) * lax.rsqrt(var + 1e-5) * lg_ref[...] + lb_ref[...]
    act = (ln * _sigmoid(ln)).astype(BF16)
    o = jnp.dot(act, w2_ref[...], preferred_element_type=F32)
    x1 = x_ref[...] + mod_ref[2:3, :] * o
    x1_ref[...] = x1
    _moe_route(x1, mod_ref, n2g_ref[...], wr_ref, br_ref, xc_ref, route_ref, cnt_ref)


CPB = FB // CH


def _ffn_kernel(be_ref, nused_ref, nval_ref, src_ref, xc_in, win_ref, bin_ref, wout_ref, bout_ref, rows_hbm,
                win_s, wout_s, xbuf, ybuf, rsem, wsem):
    del xc_in
    i = pl.program_id(0)
    last = pl.num_programs(0) - 1
    e = be_ref[i]
    e_prev = be_ref[jnp.maximum(i - 1, 0)]

    def copy_in(blk, c):
        row = pl.multiple_of(src_ref[blk * CPB + c], CH)
        return pltpu.make_async_copy(rows_hbm.at[pl.ds(row, CH), :],
                                     xbuf.at[blk % 2, pl.ds(pl.multiple_of(c * CH, CH), CH), :], rsem.at[blk % 2])

    def copy_out(blk, c):
        row = pl.multiple_of(src_ref[blk * CPB + c], CH)
        return pltpu.make_async_copy(ybuf.at[blk % 2, pl.ds(pl.multiple_of(c * CH, CH), CH), :],
                                     rows_hbm.at[pl.ds(row, CH), :], wsem.at[blk % 2])

    def for_chunks(blk, fn):
        nv = nval_ref[blk]

        @pl.when(nv == CPB)
        def _():
            for c in range(CPB):
                fn(c)

        @pl.when(nv < CPB)
        def _():
            def body(c, carry):
                fn(c)
                return carry

            lax.fori_loop(0, nv, body, 0)

    def wait_chunks(blk, chunk_copy, whole_copy):
        nv = nval_ref[blk]

        @pl.when(nv == CPB)
        def _():
            whole_copy.wait()

        @pl.when(nv < CPB)
        def _():
            def body(c, carry):
                chunk_copy(blk, c).wait()
                return carry

            lax.fori_loop(0, nv, body, 0)

    def wait_reads(blk):
        wait_chunks(blk, copy_in, pltpu.make_async_copy(rows_hbm.at[pl.ds(0, FB), :], xbuf.at[blk % 2], rsem.at[blk % 2]))

    def wait_writes(blk):
        wait_chunks(blk, copy_out, pltpu.make_async_copy(ybuf.at[blk % 2], rows_hbm.at[pl.ds(0, FB), :], wsem.at[blk % 2]))

    @pl.when(i == 0)
    def _():
        xbuf[...] = jnp.zeros_like(xbuf)
        for_chunks(i, lambda c: copy_in(i, c).start())

    @pl.when(i < last)
    def _():
        for_chunks(i + 1, lambda c: copy_in(i + 1, c).start())

    @pl.when((i == 0) | (e != e_prev))
    def _():
        win_s[...] = win_ref[...].astype(BF16)
        wout_s[...] = wout_ref[...].astype(BF16)

    @pl.when(i >= 2)
    def _():
        wait_writes(i - 2)

    @pl.when(i < nused_ref[0])
    def _():
        wait_reads(i)
        xa, xb = _unpack_pairs(xbuf[i % 2])
        hb = jnp.dot(xa, win_s[:D // 2, :], preferred_element_type=F32) \
            + jnp.dot(xb, win_s[D // 2:, :], preferred_element_type=F32) + bin_ref[...]
        g = jnp.minimum(hb[:, :D_FF], SWIGLU_LIMIT)
        u = jnp.clip(hb[:, D_FF:], -SWIGLU_LIMIT, SWIGLU_LIMIT)
        act = g * _sigmoid(SWIGLU_ALPHA * g) * (u + 1.0)
        y = jnp.dot(act.astype(BF16), wout_s[...], preferred_element_type=F32) + bout_ref[...]
        ybuf[i % 2] = _pack_pairs(y.astype(BF16).astype(F32))
        for_chunks(i, lambda c: copy_out(i, c).start())

    @pl.when(i == last)
    def _():
        @pl.when(i >= 1)
        def _():
            wait_writes(i - 1)

        wait_writes(i)


def _expert_ffn(block_e, n_used, n_valid, src, xc, layer, w_in, b_in, w_out, b_out):
    n_blocks = block_e.shape[0]
    depth = w_in.shape[0]
    wspec = lambda shape: pl.BlockSpec((None, None) + shape, lambda i, be, *_: (layer, be[i], 0, 0))
    return pl.pallas_call(
        _ffn_kernel,
        grid_spec=pltpu.PrefetchScalarGridSpec(
            num_scalar_prefetch=4,
            grid=(n_blocks,),
            in_specs=[pl.BlockSpec(memory_space=pl.ANY), wspec((D, 2 * D_FF)), wspec((1, 2 * D_FF)),
                      wspec((D_FF, D)), wspec((1, D))],
            out_specs=pl.BlockSpec(memory_space=pl.ANY),
            scratch_shapes=[pltpu.VMEM((D, 2 * D_FF), BF16), pltpu.VMEM((D_FF, D), BF16),
                            pltpu.VMEM((2, FB, D // 2), U32), pltpu.VMEM((2, FB, D // 2), U32),
                            pltpu.SemaphoreType.DMA((2,)), pltpu.SemaphoreType.DMA((2,))],
        ),
        out_shape=jax.ShapeDtypeStruct(xc.shape, U32),
        input_output_aliases={4: 0},
        compiler_params=_cparams(("arbitrary",)),
        name="moe_expert_ffn",
    )(block_e, n_used, n_valid, src, xc, w_in, b_in.reshape(depth, N_EXPERTS, 1, 2 * D_FF), w_out,
      b_out.reshape(depth, N_EXPERTS, 1, D))


def _moe_plan(cnt):
    nt = cnt.shape[0]
    n_blocks = (nt * TM * TOP_K + nt * N_EXPERTS * (CH - 1)) // FB + N_EXPERTS + 1
    n = cnt[:, 0, :N_EXPERTS].astype(I32)
    nch = (n + CH - 1) // CH
    cend = jnp.cumsum(nch, axis=0)
    ce = cend[-1]
    pce = (ce + CPB - 1) // CPB * CPB
    e_end = jnp.cumsum(pce)
    e_start = e_end - pce
    first_slot = CH * (jnp.cumsum(nch, axis=1) - nch)

    blk0 = jnp.arange(n_blocks, dtype=I32) * CPB
    block_e = jnp.minimum(jnp.sum((blk0[:, None] >= e_end[None, :]).astype(I32), axis=-1), N_EXPERTS - 1)
    pick = lambda table, idx: jnp.sum(jnp.where(idx[:, None] == jnp.arange(table.shape[0], dtype=I32), table, 0), axis=-1)
    n_valid = jnp.clip(pick(e_start + ce, block_e) - blk0, 0, CPB)
    n_used = (e_end[-1] // CPB).reshape(1)

    d = jnp.arange(n_blocks * CPB, dtype=I32)
    e_of = jnp.repeat(block_e, CPB)
    local = d - pick(e_start, e_of)
    onehot_e = (e_of[:, None] == jnp.arange(N_EXPERTS, dtype=I32)).astype(F32)
    by_expert = lambda m: jnp.dot(onehot_e, m.T.astype(F32), precision=lax.Precision.HIGHEST).astype(I32)
    cend_e = by_expert(cend)
    in_tile = (local[:, None] >= cend_e - by_expert(nch)) & (local[:, None] < cend_e)
    tile_iota = jnp.arange(nt, dtype=I32)
    at_tile = lambda m: jnp.sum(jnp.where(in_tile, m, 0), axis=-1)
    src = at_tile(tile_iota * SLOTS + by_expert(first_slot) + CH * (local[:, None] - (cend_e - by_expert(nch))))
    return src.astype(I32), block_e.astype(I32), n_valid.astype(I32), n_used.astype(I32)


def _moe(cnt, xc, layer, w_in, b_in, w_out, b_out):
    src, block_e, n_valid, n_used = _moe_plan(cnt)
    return _expert_ffn(block_e, n_used, n_valid, src, xc, layer, w_in, b_in, w_out, b_out)


def _final_kernel(npt, rows_ref, x1_ref, route_ref, modp_ref, g_ref, op_ref, os_ref):
    i = pl.program_id(0)
    x2 = x1_ref[...] + modp_ref[5:6, :] * _combine(route_ref[...], rows_ref[...])
    out = _rms(x2, g_ref[...])

    @pl.when(i < npt)
    def _():
        op_ref[...] = out

    @pl.when(i >= npt)
    def _():
        os_ref[...] = out


def _rope_tables(n_tokens):
    rows = n_tokens // GRID_W
    pos_row = jnp.repeat(jnp.arange(rows, dtype=F32), GRID_W)
    pos_col = jnp.tile(jnp.arange(GRID_W, dtype=F32), rows)
    n_freq = ROPE_DIM // 4
    inv_freq = ROPE_BASE ** (-jnp.arange(n_freq, dtype=F32) / n_freq)
    ang = jnp.stack([pos_row[:, None] * inv_freq, pos_col[:, None] * inv_freq], axis=1)
    cos, sin = jnp.cos(ang), jnp.sin(ang)
    zero = jnp.zeros_like(sin)
    c = jnp.stack([cos, cos], axis=2).reshape(n_tokens, ROPE_DIM)
    sm = jnp.stack([-sin, zero], axis=2).reshape(n_tokens, ROPE_DIM)
    sp = jnp.stack([zero, sin], axis=2).reshape(n_tokens, ROPE_DIM)

    def place(a, fill):
        left = jnp.full((n_tokens, ROPE_LANE0), fill, F32)
        right = jnp.full((n_tokens, LANES - ROPE_LANE0 - ROPE_DIM), fill, F32)
        body = jnp.concatenate([left, a, right], axis=1)
        ident = jnp.full((TM, LANES), fill, F32)
        return jnp.concatenate([body, ident], axis=0)

    return jnp.stack([place(c, 1.0), place(sm, 0.0), place(sp, 0.0)], axis=0)


def kernel(x_prompt, x_sample, c, cache_mla_ckv, cache_mla_krope, state_mlstm_C, state_mlstm_n, state_mlstm_m, c_ctx, ada_w, ada_b, norm1_g, norm2_g, ab_w_in, mla_q_norm_g, mla_w_uq, mla_kv_norm_g, mla_w_ukv, mlstm_gate_b, mlstm_norm_g, ab_w_out, conv_w_pw1, conv_w_dw, conv_b_dw, conv_ln_g, conv_ln_b, conv_w_pw2, router_w, router_b, moe_w_in, moe_b_in, moe_w_out, moe_b_out, final_norm_g):
    bp, sp_, _ = x_prompt.shape
    bs, ss, _ = x_sample.shape
    past = cache_mla_ckv.shape[2]
    tp, ts = bp * sp_, bs * ss
    t = tp + ts
    nt = t // TM
    npt = tp // TM
    tps = ss // TM
    assert sp_ == TM and ss % TM == 0 and bs <= CTX_ROW and tp % ss == 0

    xp2 = x_prompt.reshape(tp, D)
    xs2 = x_sample.reshape(ts, D)
    cond = jnp.zeros((N_COND, D), F32).at[:bs].set(c).at[CTX_ROW].set(c_ctx)
    mod = _modulation(cond, ada_w, ada_b)

    def mod_row(i):
        return jnp.where(i < npt, CTX_ROW, (i - npt) // tps)

    def mod_spec(layer):
        return pl.BlockSpec((None, None, 6, D), lambda i, *_: (layer, mod_row(i), 0, 0))

    row_spec = lambda w: pl.BlockSpec((TM, w), lambda i, *_: (i, 0))
    prompt_spec = pl.BlockSpec((TM, D), lambda i, *_: (jnp.minimum(i, npt - 1), 0))
    sample_spec = pl.BlockSpec((TM, D), lambda i, *_: (jnp.maximum(i - npt, 0), 0))
    full = lambda a: pl.BlockSpec(a.shape, lambda i, *_: (0,) * a.ndim)
    vec = lambda a: a.reshape(1, -1)

    def router_args(layer):
        wr = jnp.pad(router_w[layer], ((0, 0), (0, LANES - N_EXPERTS))).astype(BF16)
        br = jnp.pad(router_b[layer], (0, LANES - N_EXPERTS)).reshape(1, LANES)
        return wr, br

    moe_out_shapes = [
        jax.ShapeDtypeStruct((t, D), F32),
        jax.ShapeDtypeStruct((nt * SLOTS, D // 2), U32),
        jax.ShapeDtypeStruct((t, LANES), F32),
        jax.ShapeDtypeStruct((nt, SUBLANES, LANES), F32),
    ]
    moe_out_specs = [row_spec(D), pl.BlockSpec((SLOTS, D // 2), lambda i: (i, 0)), row_spec(LANES),
                     pl.BlockSpec((None, SUBLANES, LANES), lambda i: (i, 0, 0))]

    j = 0
    w = ab_w_in[j]
    zc = lambda n: jnp.zeros((D, n), F32)
    misc_w = jnp.concatenate([w[:, 2464:2480], zc(ROPE_LANE0 - 16), w[:, 384:416], zc(LANES - ROPE_LANE0 - ROPE_DIM)], axis=1)
    w_in = jnp.concatenate([w[:, :384], misc_w, w[:, 416:2464]], axis=1).astype(BF16)
    wq = jnp.pad(mla_w_uq[j].reshape(Q_RANK, H_A, NOPE + ROPE_DIM), ((0, 0), (0, 0), (0, HEAD_PAD - NOPE - ROPE_DIM)))
    wq = wq.reshape(Q_RANK, H_A * HEAD_PAD).astype(BF16)
    wkv = mla_w_ukv[j].reshape(KV_RANK, H_A, NOPE + V_A)
    wk = jnp.pad(wkv[:, :, :NOPE], ((0, 0), (0, 0), (0, HEAD_PAD - NOPE))).reshape(KV_RANK, H_A * HEAD_PAD).astype(BF16)
    wv = jnp.pad(wkv[:, :, NOPE:], ((0, 0), (0, 0), (0, HEAD_PAD - V_A))).reshape(KV_RANK, H_A * HEAD_PAD).astype(BF16)
    gate_b = jnp.pad(mlstm_gate_b[j], (0, LANES - 4 * H_B)).reshape(1, LANES)
    tabs = _rope_tables(ss)
    pos_blocks = ss // TM

    def tab_block(i):
        return jnp.where(i < npt, pos_blocks, (i - npt) % tps)

    nctx = bs * past
    ctx_ckv = cache_mla_ckv[:, j].reshape(nctx, KV_RANK)
    ctx_kr = jnp.pad(cache_mla_krope[:, j].reshape(nctx, ROPE_DIM), ((0, 0), (ROPE_LANE0, LANES - ROPE_LANE0 - ROPE_DIM)))
    nct, cps, nts = nctx // TM, past // TM, ts // TM
    kps = cps + tps
    kv_rows = bs * (past + ss) + tp

    def kv_block(i):
        own = ((i - npt) // tps) * kps + cps + (i - npt) % tps
        cached = ((i - nt) // cps) * kps + (i - nt) % cps
        return jnp.where(i < npt, bs * kps + i, jnp.where(i < nt, own, cached))

    tile_spec = lambda w: pl.BlockSpec((TM, w), lambda i: (jnp.minimum(i, nt - 1), 0))
    ctx_spec = pl.BlockSpec((TM, LANES), lambda i: (jnp.clip(i - nt, 0, nct - 1), 0))
    kv_spec = pl.BlockSpec((TM, H_A * HEAD_PAD), lambda i: (kv_block(i), 0))
    ins = [xp2, xs2, ctx_ckv, ctx_kr, mod, vec(norm1_g[0]), w_in, vec(mla_q_norm_g[j]), wq, vec(mla_kv_norm_g[j]), wk, wv,
           gate_b, tabs]
    in_specs = [prompt_spec, pl.BlockSpec((TM, D), lambda i: (jnp.clip(i - npt, 0, nts - 1), 0)), ctx_spec, ctx_spec,
                mod_spec(0)] + [full(a) for a in ins[5:13]] + [
        pl.BlockSpec((3, TM, LANES), lambda i: (0, tab_block(i), 0))]
    q, k, v, ckv, misc, mqkv, mo = pl.pallas_call(
        functools.partial(_ab_in_kernel, npt, nt),
        grid=(nt + nct,),
        in_specs=in_specs,
        out_specs=[tile_spec(H_A * HEAD_PAD), kv_spec, kv_spec, tile_spec(KV_RANK), tile_spec(LANES),
                   tile_spec(3 * MB), tile_spec(MB)],
        out_shape=[
            jax.ShapeDtypeStruct((t, H_A * HEAD_PAD), BF16), jax.ShapeDtypeStruct((kv_rows, H_A * HEAD_PAD), BF16),
            jax.ShapeDtypeStruct((kv_rows, H_A * HEAD_PAD), BF16), jax.ShapeDtypeStruct((t, KV_RANK), F32),
            jax.ShapeDtypeStruct((t, LANES), F32), jax.ShapeDtypeStruct((t, 3 * MB), BF16),
            jax.ShapeDtypeStruct((t, MB), F32)],
        compiler_params=_cparams(("arbitrary",)),
        name="ab_in_proj",
    )(*ins)

    oa_p = _attention(q, k, v, 0, bp, sp_, bs * (past + ss), sp_)
    oa_s = _attention(q, k, v, tp, bs, ss, 0, past + ss)

    zc0 = jnp.zeros((bp, 2, H_B, DH_B, DH_B), F32)
    zn0 = jnp.zeros((bp, 2, H_B, 1, DH_B), F32)
    zm0 = jnp.zeros((bp, 2, H_B, 1, LANES), F32)
    hf_p, hb_p, c_p, n_p, m_p = _mlstm(mqkv, misc, zc0, zn0, zm0, 0, bp, sp_)
    sm0 = jnp.broadcast_to(state_mlstm_m[:, j][..., None, None], (bs, 2, H_B, 1, LANES))
    hf_s, hb_s, _, _, _ = _mlstm(mqkv, misc, state_mlstm_C[:, j], state_mlstm_n[:, j][:, :, :, None, :], sm0, tp, bs, ss)

    wr, br = router_args(0)
    prompt_w = lambda w: pl.BlockSpec((TM, w), lambda i: (jnp.minimum(i, npt - 1), 0))
    sample_w = lambda w: pl.BlockSpec((TM, w), lambda i: (jnp.maximum(i - npt, 0), 0))
    ins = [oa_p, oa_s, hf_p, hf_s, hb_p, hb_s, mo, vec(mlstm_norm_g[j]), ab_w_out[j].astype(BF16), xp2, xs2, mod,
           vec(norm2_g[0]), wr, br]
    in_specs = [prompt_w(H_A * V_A), sample_w(H_A * V_A), prompt_w(MB), sample_w(MB), prompt_w(MB), sample_w(MB),
                row_spec(MB), full(ins[7]), full(ins[8]),
                prompt_spec, sample_spec, mod_spec(0), full(ins[12]), full(wr), full(br)]
    x1, xc, route, cnt = pl.pallas_call(
        functools.partial(_ab_out_kernel, npt),
        grid=(nt,),
        in_specs=in_specs,
        out_specs=moe_out_specs,
        out_shape=moe_out_shapes,
        compiler_params=_cparams(("arbitrary",)),
        name="ab_out_proj",
    )(*ins)
    ys = _moe(cnt, xc, 0, moe_w_in, moe_b_in, moe_w_out, moe_b_out)

    slots_spec = pl.BlockSpec((SLOTS, D // 2), lambda i, *_: (i, 0))
    ins = [ys, x1, route, mod, mod, vec(norm1_g[1]), conv_w_pw1[0].astype(BF16)]
    assert npt % CT == 0 and tps % CT == 0
    wide = lambda rows, w: pl.BlockSpec((CT * rows, w), lambda i: (i, 0))
    wide_mod = lambda layer: pl.BlockSpec((None, None, 6, D), lambda i: (layer, mod_row(CT * i), 0, 0))
    x2, u = pl.pallas_call(
        _conv_in_kernel,
        grid=(nt // CT,),
        in_specs=[wide(SLOTS, D // 2), wide(TM, D), wide(TM, LANES), wide_mod(0), wide_mod(1), full(ins[5]), full(ins[6])],
        out_specs=[wide(TM, D), wide(TM, D)],
        out_shape=[jax.ShapeDtypeStruct((t, D), F32), jax.ShapeDtypeStruct((t, D), F32)],
        compiler_params=_cparams(("arbitrary",)),
        name="conv_in_proj",
    )(*ins)

    hpt = TM // HALO
    nhalo = t // HALO
    wr, br = router_args(1)
    ins = [u, u, u, conv_w_dw[0], vec(conv_b_dw[0]), vec(conv_ln_g[0]), vec(conv_ln_b[0]), conv_w_pw2[0].astype(BF16),
           x2, mod, vec(norm2_g[1]), wr, br]
    in_specs = [row_spec(D),
                pl.BlockSpec((HALO, D), lambda i: (jnp.maximum(i * hpt - 1, 0), 0)),
                pl.BlockSpec((HALO, D), lambda i: (jnp.minimum((i + 1) * hpt, nhalo - 1), 0)),
                full(ins[3]), full(ins[4]), full(ins[5]), full(ins[6]), full(ins[7]), row_spec(D), mod_spec(1),
                full(ins[10]), full(wr), full(br)]
    x1, xc, route, cnt = pl.pallas_call(
        functools.partial(_conv_out_kernel, npt, tps),
        grid=(nt,),
        in_specs=in_specs,
        out_specs=moe_out_specs,
        out_shape=moe_out_shapes,
        scratch_shapes=[pltpu.VMEM((TM + 2 * HALO, D), F32), pltpu.VMEM((SUBLANES, TM + 24, D), F32)],
        compiler_params=_cparams(("arbitrary",)),
        name="conv_out_proj",
    )(*ins)
    ys = _moe(cnt, xc, 1, moe_w_in, moe_b_in, moe_w_out, moe_b_out)

    ins = [ys, x1, route, mod, vec(final_norm_g)]
    y_p, y_s = pl.pallas_call(
        functools.partial(_final_kernel, npt),
        grid=(nt,),
        in_specs=[slots_spec, row_spec(D), row_spec(LANES), mod_spec(1), full(ins[4])],
        out_specs=[prompt_spec, sample_spec],
        out_shape=[jax.ShapeDtypeStruct((tp, D), F32), jax.ShapeDtypeStruct((ts, D), F32)],
        compiler_params=_cparams(("arbitrary",)),
        name="final_norm",
    )(*ins)

    y_prompt = y_p.reshape(bp, sp_, D)
    y_sample = y_s.reshape(bs, ss, D)
    new_ckv = ckv[:tp].reshape(bp, 1, sp_, KV_RANK)
    new_krope = misc[:tp, ROPE_LANE0:ROPE_LANE0 + ROPE_DIM].reshape(bp, 1, sp_, ROPE_DIM)
    new_c = c_p[:, None]
    new_n = n_p[:, None, :, :, 0, :]
    new_m = m_p[:, None, :, :, 0, 0]
    return (y_prompt, y_sample, new_ckv, new_krope, new_c, new_n, new_m)
```
